```python
import math
import jax, jax.numpy as jnp
from jax import lax
import numpy as np

D_MODEL = 1024
BATCH = 2
SEQ = 8192
DEPTH = 1

CHUNK = 64
D_MIX = D_MODEL
GM_GROUPS = 8
GM_GROUP_DIM = 64
GM_WIDTH = GM_GROUPS * GM_GROUP_DIM
GM_BLOCK = 128
DN_HEADS = 4
DN_HEAD_DIM = 128
DN_WIDTH = DN_HEADS * DN_HEAD_DIM
DN_CONV = 4
D_IN = 2 * GM_WIDTH + 4 * DN_WIDTH + 2 * DN_HEADS
D_FF = 2816
EPS = 1e-6

kernel_name = "hybrid_gmlp_gated_deltanet_macaron"


def rmsnorm(x, g):
    xf = x.astype(jnp.float32)
    y = xf * lax.rsqrt(jnp.mean(xf * xf, axis=-1, keepdims=True) + EPS)
    return (y * g.astype(jnp.float32)).astype(x.dtype)


def layernorm(x, g, b):
    xf = x.astype(jnp.float32)
    mu = jnp.mean(xf, axis=-1, keepdims=True)
    var = jnp.mean(jnp.square(xf - mu), axis=-1, keepdims=True)
    y = (xf - mu) * lax.rsqrt(var + EPS)
    return (y * g.astype(jnp.float32) + b.astype(jnp.float32)).astype(x.dtype)


def l2norm(x):
    return x * lax.rsqrt(jnp.sum(x * x, axis=-1, keepdims=True) + EPS)


def swiglu(x, w_gate, w_up, w_down):
    return (jax.nn.silu(x @ w_gate) * (x @ w_up)) @ w_down


def chunk_causal_mask(n):
    c = jnp.arange(n) // CHUNK
    return c[None, :] <= c[:, None]


def spatial_gating(u, v, w_s, b_s, ln_g, ln_b):
    B, T, _ = v.shape
    v = layernorm(v, ln_g, ln_b)
    nb = T // GM_BLOCK
    vb = v.reshape(B, nb, GM_BLOCK, GM_GROUPS, GM_GROUP_DIM)
    w = jnp.where(chunk_causal_mask(GM_BLOCK)[None], w_s, 0.0).astype(v.dtype)
    mixed = jnp.einsum('gij,bnjgc->bnigc', w, vb) + b_s.T[:, :, None].astype(v.dtype)
    return u * mixed.reshape(B, T, GM_WIDTH)


def causal_dwconv(x, w):
    K, C = w.shape
    return lax.conv_general_dilated(
        x, w[:, None, :].astype(x.dtype), window_strides=(1,), padding=[(K - 1, 0)],
        dimension_numbers=('NWC', 'WIO', 'NWC'), feature_group_count=C)


def gated_delta_rule(q, k, v, beta, g):
    B, T, H, Dk = q.shape
    Dv = v.shape[-1]
    C = CHUNK
    N = T // C
    scale = Dk ** -0.5

    def to_chunks(t):
        return t.reshape(B, N, C, H, *t.shape[3:]).swapaxes(2, 3)

    q, k, v = to_chunks(q * scale), to_chunks(k), to_chunks(v)
    beta, g = to_chunks(beta), to_chunks(g)
    g = jnp.cumsum(g, axis=-1)
    tri = jnp.tril(jnp.ones((C, C), bool))
    strict = jnp.tril(jnp.ones((C, C), bool), -1)
    decay = jnp.exp(jnp.where(tri, g[..., :, None] - g[..., None, :], -jnp.inf))
    k_beta = k * beta[..., None]
    v_beta = v * beta[..., None]
    L = jnp.where(strict, jnp.einsum('bnhid,bnhjd->bnhij', k_beta, k) * decay, 0.0)
    eye = jnp.eye(C, dtype=L.dtype)
    rhs = jnp.concatenate([v_beta, k_beta * jnp.exp(g)[..., None]], axis=-1)
    sol = lax.linalg.triangular_solve(L + eye, rhs, left_side=True, lower=True,
                                      unit_diagonal=True)
    u_c, w_c = sol[..., :Dv], sol[..., Dv:]
    attn = jnp.where(tri, jnp.einsum('bnhid,bnhjd->bnhij', q, k) * decay, 0.0)
    g_last = g[..., -1]
    k_dec = k * jnp.exp(g_last[..., None] - g)[..., None]
    q_dec = q * jnp.exp(g)[..., None]
    xs = tuple(jnp.moveaxis(t, 1, 0) for t in (q_dec, k_dec, u_c, w_c, attn, g_last))

    def step(S, inp):
        qd, kd, uc, wc, a, gl = inp
        v_new = uc - jnp.einsum('bhcd,bhde->bhce', wc, S)
        o = jnp.einsum('bhcd,bhde->bhce', qd, S) + jnp.einsum('bhij,bhje->bhie', a, v_new)
        S = S * jnp.exp(gl)[..., None, None] + jnp.einsum('bhcd,bhce->bhde', kd, v_new)
        return S, o

    S0 = jnp.zeros((B, H, Dk, Dv), q.dtype)
    _, o = lax.scan(step, S0, xs)
    return jnp.moveaxis(o, 0, 1).swapaxes(2, 3).reshape(B, T, H, Dv)


def setup_inputs(seed: int = 0) -> dict:
    key = jax.random.key(seed)
    ks = jax.random.split(key, 24)
    f32 = jnp.float32
    L = DEPTH

    def nrm(k, shape, scale):
        return jax.random.normal(k, shape, f32) * scale

    def gain(k, shape):
        return 1.0 + 0.05 * jax.random.normal(k, shape, f32)

    dt = jnp.exp(jax.random.uniform(ks[13], (L, DN_HEADS), f32, math.log(1e-3), math.log(1e-1)))
    return {
        "x": jax.random.normal(ks[0], (BATCH, SEQ, D_MODEL), f32),
        "ffn1_norm": gain(ks[1], (L, D_MODEL)),
        "ffn1_w_gate": nrm(ks[2], (L, D_MODEL, D_FF), D_MODEL ** -0.5),
        "ffn1_w_up": nrm(ks[3], (L, D_MODEL, D_FF), D_MODEL ** -0.5),
        "ffn1_w_down": nrm(ks[4], (L, D_FF, D_MODEL), D_FF ** -0.5),
        "mix_norm": gain(ks[5], (L, D_MODEL)),
        "w_in": nrm(ks[6], (L, D_MODEL, D_IN), D_MODEL ** -0.5),
        "gm_ln_g": gain(ks[7], (L, GM_WIDTH)),
        "gm_ln_b": nrm(ks[8], (L, GM_WIDTH), 0.02),
        "gm_w_s": nrm(ks[9], (L, GM_GROUPS, GM_BLOCK, GM_BLOCK), GM_BLOCK ** -0.5),
        "gm_b_s": 1.0 + nrm(ks[10], (L, GM_GROUPS, GM_BLOCK), 0.1),
        "dn_conv_w": nrm(ks[11], (L, DN_CONV, 3 * DN_WIDTH), DN_CONV ** -0.5),
        "dn_a_log": jnp.log(jax.random.uniform(ks[12], (L, DN_HEADS), f32, 1.0, 16.0)),
        "dn_dt_bias": dt + jnp.log(-jnp.expm1(-dt)),
        "dn_norm": gain(ks[14], (L, DN_HEAD_DIM)),
        "w_out": nrm(ks[15], (L, D_MIX, D_MODEL), D_MIX ** -0.5),
        "ffn2_norm": gain(ks[16], (L, D_MODEL)),
        "ffn2_w_gate": nrm(ks[17], (L, D_MODEL, D_FF), D_MODEL ** -0.5),
        "ffn2_w_up": nrm(ks[18], (L, D_MODEL, D_FF), D_MODEL ** -0.5),
        "ffn2_w_down": nrm(ks[19], (L, D_FF, D_MODEL), D_FF ** -0.5),
        "final_norm": gain(ks[20], (D_MODEL,)),
    }


def reference(x, ffn1_norm, ffn1_w_gate, ffn1_w_up, ffn1_w_down, mix_norm, w_in,
              gm_ln_g, gm_ln_b, gm_w_s, gm_b_s, dn_conv_w, dn_a_log, dn_dt_bias, dn_norm,
              w_out, ffn2_norm, ffn2_w_gate, ffn2_w_up, ffn2_w_down, final_norm):
    B, T, _ = x.shape
    split_at = [GM_WIDTH, 2 * GM_WIDTH, 2 * GM_WIDTH + 3 * DN_WIDTH,
                2 * GM_WIDTH + 4 * DN_WIDTH, 2 * GM_WIDTH + 4 * DN_WIDTH + DN_HEADS]
    for l in range(DEPTH):
        x = x + 0.5 * swiglu(rmsnorm(x, ffn1_norm[l]), ffn1_w_gate[l], ffn1_w_up[l], ffn1_w_down[l])

        h = rmsnorm(x, mix_norm[l])
        p = h @ w_in[l]
        u_a, v_a, qkv, z, b_raw, a_raw = jnp.split(p, split_at, axis=-1)

        y_a = spatial_gating(jax.nn.gelu(u_a), jax.nn.gelu(v_a), gm_w_s[l], gm_b_s[l],
                             gm_ln_g[l], gm_ln_b[l])

        qkv = jax.nn.silu(causal_dwconv(qkv, dn_conv_w[l])).astype(jnp.float32)
        q, k, v = jnp.split(qkv, 3, axis=-1)
        q = l2norm(q.reshape(B, T, DN_HEADS, DN_HEAD_DIM))
        k = l2norm(k.reshape(B, T, DN_HEADS, DN_HEAD_DIM))
        v = v.reshape(B, T, DN_HEADS, DN_HEAD_DIM)
        beta = jax.nn.sigmoid(b_raw.astype(jnp.float32))
        g = -jnp.exp(dn_a_log[l].astype(jnp.float32)) * jax.nn.softplus(
            a_raw.astype(jnp.float32) + dn_dt_bias[l].astype(jnp.float32))
        o = gated_delta_rule(q, k, v, beta, g)
        zg = jax.nn.silu(z.astype(jnp.float32)).reshape(B, T, DN_HEADS, DN_HEAD_DIM)
        y_b = (rmsnorm(o, dn_norm[l]) * zg).reshape(B, T, DN_WIDTH).astype(x.dtype)

        x = x + jnp.concatenate([y_a, y_b], axis=-1) @ w_out[l]

        x = x + 0.5 * swiglu(rmsnorm(x, ffn2_norm[l]), ffn2_w_gate[l], ffn2_w_up[l], ffn2_w_down[l])
    return rmsnorm(x, final_norm)
```

```python
import functools

import jax
import jax.numpy as jnp
from jax import lax
from jax.experimental import pallas as pl
from jax.experimental.pallas import tpu as pltpu

F32 = jnp.float32
BF16 = jnp.bfloat16

D_MODEL = 1024
D_FF = 2816
CHUNK = 64
GM_GROUPS = 8
GM_GROUP_DIM = 64
GM_WIDTH = GM_GROUPS * GM_GROUP_DIM
GM_BLOCK = 128
DN_HEADS = 4
DN_HEAD_DIM = 128
DN_WIDTH = DN_HEADS * DN_HEAD_DIM
DN_CONV = 4
EPS = 1e-6

LANES = 128
SUBLANES = 8
VMEM_LIMIT_BYTES = 60000 * 1024

FFN_ROWS = 512
FFN_COLS = 256
MIX_ROWS = 256
CONV_TAIL = SUBLANES


def _dot(a, b):
    return jnp.dot(a, b, preferred_element_type=F32)


def _dot_nt(a, b):
    return lax.dot_general(a, b, (((1,), (1,)), ((), ())), preferred_element_type=F32)


def _dot_tn(a, b):
    return lax.dot_general(a, b, (((0,), (0,)), ((), ())), preferred_element_type=F32)


def _rmsnorm(x, g):
    return (x * lax.rsqrt(jnp.mean(x * x, axis=-1, keepdims=True) + EPS)) * g


def _softplus(x):
    return jnp.maximum(x, 0.0) + jnp.log1p(jnp.exp(-jnp.abs(x)))


def _ffn_body(x_ref, g_ref, wg_ref, wu_ref, wd_ref, *rest, final_norm):
    if final_norm:
        fg_ref, o_ref = rest
    else:
        (o_ref,) = rest
    x = x_ref[...]
    hb = _rmsnorm(x, g_ref[...]).astype(BF16)
    acc = jnp.zeros(x.shape, F32)
    for j in range(D_FF // FFN_COLS):
        cols = slice(j * FFN_COLS, (j + 1) * FFN_COLS)
        gate = _dot(hb, wg_ref[:, cols])
        up = _dot(hb, wu_ref[:, cols])
        act = (jax.nn.silu(gate) * up).astype(BF16)
        acc = acc + _dot(act, wd_ref[cols, :])
    y = x + 0.5 * acc
    if final_norm:
        y = _rmsnorm(y, fg_ref[...])
    o_ref[...] = y


def _ffn_call(x2d, gain, w_gate, w_up, w_down, final_gain=None):
    rows = x2d.shape[0]
    final_norm = final_gain is not None
    const = lambda i: (0, 0)
    in_specs = [
        pl.BlockSpec((FFN_ROWS, D_MODEL), lambda i: (i, 0)),
        pl.BlockSpec((1, D_MODEL), const),
        pl.BlockSpec((D_MODEL, D_FF), const),
        pl.BlockSpec((D_MODEL, D_FF), const),
        pl.BlockSpec((D_FF, D_MODEL), const),
    ]
    args = [x2d, gain.reshape(1, D_MODEL), w_gate.astype(BF16), w_up.astype(BF16),
            w_down.astype(BF16)]
    if final_norm:
        in_specs.append(pl.BlockSpec((1, D_MODEL), const))
        args.append(final_gain.reshape(1, D_MODEL))
    return pl.pallas_call(
        functools.partial(_ffn_body, final_norm=final_norm),
        grid=(rows // FFN_ROWS,),
        in_specs=in_specs,
        out_specs=pl.BlockSpec((FFN_ROWS, D_MODEL), lambda i: (i, 0)),
        out_shape=jax.ShapeDtypeStruct((rows, D_MODEL), F32),
        compiler_params=pltpu.CompilerParams(
            dimension_semantics=("arbitrary",), vmem_limit_bytes=VMEM_LIMIT_BYTES),
        name="ffn_final" if final_norm else "ffn",
    )(*args)


def _split_hi_lo(x):
    hi = x.astype(BF16)
    lo = (x - hi.astype(F32)).astype(BF16)
    return hi, lo


def _mixer_body(x_ref, mg_ref, win_ref, wba_ref, lng_ref, lnb_ref, wcat_ref, bias_ref,
                cw_ref, alog_ref, dtb_ref, dng_ref, wout_ref, o_ref, s_ref, cbuf_ref):
    rows = MIX_ROWS
    n_chunks = rows // CHUNK
    n_blocks = rows // GM_BLOCK

    @pl.when(pl.program_id(1) == 0)
    def _():
        s_ref[...] = jnp.zeros(s_ref.shape, F32)
        cbuf_ref[0:CONV_TAIL, :] = jnp.zeros((CONV_TAIL, 3 * DN_WIDTH), F32)

    x = x_ref[...]
    hb = _rmsnorm(x, mg_ref[...]).astype(BF16)

    u_a = jax.nn.gelu(_dot(hb, win_ref[:, 0:GM_WIDTH]))
    v_a = jax.nn.gelu(_dot(hb, win_ref[:, GM_WIDTH:2 * GM_WIDTH]))
    mu = jnp.mean(v_a, axis=-1, keepdims=True)
    vc = v_a - mu
    var = jnp.mean(vc * vc, axis=-1, keepdims=True)
    v_n = (vc * lax.rsqrt(var + EPS)) * lng_ref[...] + lnb_ref[...]

    wi = lax.broadcasted_iota(jnp.int32, (GM_BLOCK, 2 * GM_BLOCK), 0) // CHUNK
    wj = (lax.broadcasted_iota(jnp.int32, (GM_BLOCK, 2 * GM_BLOCK), 1) % GM_BLOCK) // CHUNK
    w_mask = wj <= wi
    lane_lo = lax.broadcasted_iota(jnp.int32, (GM_BLOCK, LANES), 1) < GM_GROUP_DIM
    mixed_cols = []
    for p in range(GM_GROUPS // 2):
        v_p = v_n[:, p * LANES:(p + 1) * LANES]
        rhs = []
        for b in range(n_blocks):
            blk = v_p[b * GM_BLOCK:(b + 1) * GM_BLOCK]
            rhs.append(jnp.concatenate(
                [jnp.where(lane_lo, blk, 0.0), jnp.where(lane_lo, 0.0, blk)], axis=0))
        rhs = jnp.concatenate(rhs, axis=1).astype(BF16)
        w_p = jnp.where(w_mask, wcat_ref[p], 0.0).astype(BF16)
        res = _dot(w_p, rhs)
        mixed_cols.append(jnp.concatenate(
            [res[:, b * LANES:(b + 1) * LANES] for b in range(n_blocks)], axis=0))
    bias = jnp.concatenate([bias_ref[...]] * n_blocks, axis=0)
    y_a = u_a * (jnp.concatenate(mixed_cols, axis=1) + bias)

    cbuf_ref[CONV_TAIL:CONV_TAIL + rows, :] = _dot(hb, win_ref[:, 2 * GM_WIDTH:2 * GM_WIDTH + 3 * DN_WIDTH])
    conv = None
    for k in range(DN_CONV):
        start = CONV_TAIL - (DN_CONV - 1) + k
        term = cw_ref[k:k + 1, :] * cbuf_ref[start:start + rows, :]
        conv = term if conv is None else conv + term
    cbuf_ref[0:CONV_TAIL, :] = cbuf_ref[rows:rows + CONV_TAIL, :]
    qkv = jax.nn.silu(conv)
    zg = jax.nn.silu(_dot(hb, win_ref[:, 2 * GM_WIDTH + 3 * DN_WIDTH:]))

    pba = _dot(hb, wba_ref[...])
    beta_all = jax.nn.sigmoid(pba)
    g_all = -jnp.exp(alog_ref[...]) * _softplus(pba + dtb_ref[...])

    ri = lax.broadcasted_iota(jnp.int32, (rows, rows), 0)
    ci = lax.broadcasted_iota(jnp.int32, (rows, rows), 1)
    same_chunk = (ri // CHUNK) == (ci // CHUNK)
    tri = same_chunk & (ci <= ri)
    strict = same_chunk & (ci < ri)
    tri_b = jnp.where(tri, 1.0, 0.0).astype(BF16)
    blk_b = jnp.where(same_chunk, 1.0, 0.0).astype(BF16)
    eye = jnp.where(ri == ci, 1.0, 0.0)
    scale = DN_HEAD_DIM ** -0.5

    y_heads = []
    for h in range(DN_HEADS):
        hs = slice(h * DN_HEAD_DIM, (h + 1) * DN_HEAD_DIM)
        q = qkv[:, h * DN_HEAD_DIM:(h + 1) * DN_HEAD_DIM]
        k = qkv[:, DN_WIDTH + h * DN_HEAD_DIM:DN_WIDTH + (h + 1) * DN_HEAD_DIM]
        v = qkv[:, 2 * DN_WIDTH + h * DN_HEAD_DIM:2 * DN_WIDTH + (h + 1) * DN_HEAD_DIM]
        q = q * lax.rsqrt(jnp.sum(q * q, axis=-1, keepdims=True) + EPS) * scale
        k = k * lax.rsqrt(jnp.sum(k * k, axis=-1, keepdims=True) + EPS)
        beta = jnp.broadcast_to(beta_all[:, h:h + 1], (rows, LANES))
        g = jnp.broadcast_to(g_all[:, DN_HEADS + h:DN_HEADS + h + 1], (rows, LANES))
        g_hi, g_lo = _split_hi_lo(g)
        gc = _dot(tri_b, g_hi) + _dot(tri_b, g_lo)
        gl = _dot(blk_b, g_hi) + _dot(blk_b, g_lo)
        e_gc = jnp.exp(gc)
        gc_row = gc.T
        dlog = (jnp.concatenate([gc] * (rows // LANES), axis=1)
                - jnp.concatenate([gc_row] * (rows // LANES), axis=0))
        decay = jnp.exp(jnp.where(tri, dlog, -jnp.inf))

        kb = k.astype(BF16)
        k_beta = k * beta
        lmat = jnp.where(strict, _dot_nt(k_beta.astype(BF16), kb) * decay, 0.0)
        attn = (_dot_nt(q.astype(BF16), kb) * decay).astype(BF16)

        m = -lmat
        inv = eye + m
        for _ in range(5):
            mb = m.astype(BF16)
            m = _dot(mb, mb)
            inv = inv + _dot(inv.astype(BF16), m.astype(BF16))
        rhs = jnp.concatenate([v * beta, k_beta * e_gc], axis=1).astype(BF16)
        uw = _dot(inv.astype(BF16), rhs)
        u_c, w_c = uw[:, :DN_HEAD_DIM], uw[:, DN_HEAD_DIM:]
        q_dec = q * e_gc
        k_dec = k * jnp.exp(gl - gc)
        e_gl = jnp.exp(gl)

        state = s_ref[h]
        v_new, o_inter = [], []
        for c in range(n_chunks):
            cs = slice(c * CHUNK, (c + 1) * CHUNK)
            wq = jnp.concatenate([w_c[cs], q_dec[cs]], axis=0).astype(BF16)
            r = _dot(wq, state.astype(BF16))
            vn_c = u_c[cs] - r[:CHUNK]
            v_new.append(vn_c)
            o_inter.append(r[CHUNK:])
            state = (state * jnp.concatenate([e_gl[cs]] * (DN_HEAD_DIM // CHUNK), axis=0)
                     + _dot_tn(k_dec[cs].astype(BF16), vn_c.astype(BF16)))
        s_ref[h] = state
        o = (jnp.concatenate(o_inter, axis=0)
             + _dot(attn, jnp.concatenate(v_new, axis=0).astype(BF16)))
        y_heads.append(_rmsnorm(o, dng_ref[...]) * zg[:, hs])

    y = jnp.concatenate([y_a] + y_heads, axis=1).astype(BF16)
    o_ref[...] = x + _dot(y, wout_ref[...])


def _mixer_call(x, mix_norm, w_in, gm_ln_g, gm_ln_b, gm_w_s, gm_b_s, dn_conv_w, dn_a_log,
                dn_dt_bias, dn_norm, w_out):
    batch, seq, _ = x.shape
    n_main = 2 * GM_WIDTH + 4 * DN_WIDTH
    w_main = w_in[:, :n_main].astype(BF16)
    w_ba = jnp.pad(w_in[:, n_main:], ((0, 0), (0, LANES - 2 * DN_HEADS))).astype(BF16)
    w_cat = gm_w_s.reshape(GM_GROUPS // 2, 2, GM_BLOCK, GM_BLOCK).transpose(0, 2, 1, 3)
    w_cat = w_cat.reshape(GM_GROUPS // 2, GM_BLOCK, 2 * GM_BLOCK)
    bias = jnp.repeat(gm_b_s.T, GM_GROUP_DIM, axis=1)
    pad_a = lambda a: jnp.pad(a.reshape(1, DN_HEADS), ((0, 0), (DN_HEADS, LANES - 2 * DN_HEADS)))
    const2 = lambda b, t: (0, 0)
    in_specs = [
        pl.BlockSpec((None, MIX_ROWS, D_MODEL), lambda b, t: (b, t, 0)),
        pl.BlockSpec((1, D_MODEL), const2),
        pl.BlockSpec((D_MODEL, n_main), const2),
        pl.BlockSpec((D_MODEL, LANES), const2),
        pl.BlockSpec((1, GM_WIDTH), const2),
        pl.BlockSpec((1, GM_WIDTH), const2),
        pl.BlockSpec((GM_GROUPS // 2, GM_BLOCK, 2 * GM_BLOCK), lambda b, t: (0, 0, 0)),
        pl.BlockSpec((GM_BLOCK, GM_WIDTH), const2),
        pl.BlockSpec((DN_CONV, 3 * DN_WIDTH), const2),
        pl.BlockSpec((1, LANES), const2),
        pl.BlockSpec((1, LANES), const2),
        pl.BlockSpec((1, DN_HEAD_DIM), const2),
        pl.BlockSpec((D_MODEL, D_MODEL), const2),
    ]
    return pl.pallas_call(
        _mixer_body,
        grid=(batch, seq // MIX_ROWS),
        in_specs=in_specs,
        out_specs=pl.BlockSpec((None, MIX_ROWS, D_MODEL), lambda b, t: (b, t, 0)),
        out_shape=jax.ShapeDtypeStruct((batch, seq, D_MODEL), F32),
        scratch_shapes=[
            pltpu.VMEM((DN_HEADS, DN_HEAD_DIM, DN_HEAD_DIM), F32),
            pltpu.VMEM((CONV_TAIL + MIX_ROWS, 3 * DN_WIDTH), F32),
        ],
        compiler_params=pltpu.CompilerParams(
            dimension_semantics=("arbitrary", "arbitrary"), vmem_limit_bytes=VMEM_LIMIT_BYTES),
        name="mixer",
    )(x, mix_norm.reshape(1, D_MODEL), w_main, w_ba, gm_ln_g.reshape(1, GM_WIDTH),
      gm_ln_b.reshape(1, GM_WIDTH), w_cat, bias, dn_conv_w, pad_a(dn_a_log), pad_a(dn_dt_bias),
      dn_norm.reshape(1, DN_HEAD_DIM), w_out.astype(BF16))


def kernel(x, ffn1_norm, ffn1_w_gate, ffn1_w_up, ffn1_w_down, mix_norm, w_in, gm_ln_g, gm_ln_b,
           gm_w_s, gm_b_s, dn_conv_w, dn_a_log, dn_dt_bias, dn_norm, w_out, ffn2_norm,
           ffn2_w_gate, ffn2_w_up, ffn2_w_down, final_norm):
    batch, seq, d = x.shape
    depth = ffn1_norm.shape[0]
    for l in range(depth):
        x = _ffn_call(x.reshape(batch * seq, d), ffn1_norm[l], ffn1_w_gate[l], ffn1_w_up[l],
                      ffn1_w_down[l]).reshape(batch, seq, d)
        x = _mixer_call(x, mix_norm[l], w_in[l], gm_ln_g[l], gm_ln_b[l], gm_w_s[l], gm_b_s[l],
                        dn_conv_w[l], dn_a_log[l], dn_dt_bias[l], dn_norm[l], w_out[l])
        last = l == depth - 1
        x = _ffn_call(x.reshape(batch * seq, d), ffn2_norm[l], ffn2_w_gate[l], ffn2_w_up[l],
                      ffn2_w_down[l], final_norm if last else None).reshape(batch, seq, d)
    if depth == 0:
        raise ValueError("depth must be >= 1")
    return x
```

```python
import functools

import jax
import jax.numpy as jnp
from jax import lax
from jax.experimental import pallas as pl
from jax.experimental.pallas import tpu as pltpu

F32 = jnp.float32
BF16 = jnp.bfloat16
ACC = dict(preferred_element_type=jnp.float32)

D_MODEL = 1024
D_FF = 2816
CHUNK = 64
GM_GROUPS = 8
GM_GROUP_DIM = 64
GM_WIDTH = GM_GROUPS * GM_GROUP_DIM
GM_BLOCK = 128
DN_HEADS = 4
DN_HEAD_DIM = 128
DN_WIDTH = DN_HEADS * DN_HEAD_DIM
DN_CONV = 4
EPS = 1e-6

LANES = 128
SUBLANES = 8
VMEM_LIMIT_BYTES = 60000 * 1024

FFN_ROWS = 512
FFN_COLS = 256
MIX_ROWS = 256
MIX_CHUNKS = MIX_ROWS // CHUNK
CONV_TAIL = SUBLANES

_NT = (((1,), (1,)), ((), ()))
_TN = (((0,), (0,)), ((), ()))


def _rmsnorm(x, g):
    return (x * lax.rsqrt(jnp.mean(x * x, axis=-1, keepdims=True) + EPS)) * g


def _softplus(x):
    return jnp.maximum(x, 0.0) + jnp.log1p(jnp.exp(-jnp.abs(x)))


def _ffn_body(x_ref, g_ref, wg_ref, wu_ref, wd_ref, *rest, final_norm):
    if final_norm:
        fg_ref, o_ref = rest
    else:
        (o_ref,) = rest
    x = x_ref[...]
    hb = _rmsnorm(x, g_ref[...]).astype(BF16)
    acc = jnp.zeros(x.shape, F32)
    for j in range(D_FF // FFN_COLS):
        cols = slice(j * FFN_COLS, (j + 1) * FFN_COLS)
        gate = jnp.dot(hb, wg_ref[:, cols], **ACC)
        up = jnp.dot(hb, wu_ref[:, cols], **ACC)
        act = (jax.nn.silu(gate) * up).astype(BF16)
        acc = acc + jnp.dot(act, wd_ref[cols, :], **ACC)
    y = x + 0.5 * acc
    if final_norm:
        y = _rmsnorm(y, fg_ref[...])
    o_ref[...] = y


def _ffn_call(x2d, gain, w_gate, w_up, w_down, final_gain=None):
    rows = x2d.shape[0]
    final_norm = final_gain is not None
    const = lambda i: (0, 0)
    in_specs = [
        pl.BlockSpec((FFN_ROWS, D_MODEL), lambda i: (i, 0)),
        pl.BlockSpec((1, D_MODEL), const),
        pl.BlockSpec((D_MODEL, D_FF), const),
        pl.BlockSpec((D_MODEL, D_FF), const),
        pl.BlockSpec((D_FF, D_MODEL), const),
    ]
    args = [x2d, gain.reshape(1, D_MODEL), w_gate.astype(BF16), w_up.astype(BF16),
            w_down.astype(BF16)]
    if final_norm:
        in_specs.append(pl.BlockSpec((1, D_MODEL), const))
        args.append(final_gain.reshape(1, D_MODEL))
    return pl.pallas_call(
        functools.partial(_ffn_body, final_norm=final_norm),
        grid=(rows // FFN_ROWS,),
        in_specs=in_specs,
        out_specs=pl.BlockSpec((FFN_ROWS, D_MODEL), lambda i: (i, 0)),
        out_shape=jax.ShapeDtypeStruct((rows, D_MODEL), F32),
        compiler_params=pltpu.CompilerParams(
            dimension_semantics=("arbitrary",), vmem_limit_bytes=VMEM_LIMIT_BYTES),
        name="ffn_final" if final_norm else "ffn",
    )(*args)


def _split_hi_lo(x):
    hi = x.astype(BF16)
    lo = (x - hi.astype(F32)).astype(BF16)
    return hi, lo


def _lane_bcast(x, col):
    return jnp.broadcast_to(x[:, col:col + 1], (x.shape[0], LANES))


def _to_wide(full, lane_lo):
    left = jnp.where(lane_lo, full[0:CHUNK, 0:LANES], full[CHUNK:2 * CHUNK, 0:LANES])
    right = jnp.where(lane_lo, full[2 * CHUNK:3 * CHUNK, LANES:], full[3 * CHUNK:, LANES:])
    return jnp.concatenate([left, right], axis=1)


def _col_to_wide(col, lane_lo):
    left = jnp.where(lane_lo, col[0:CHUNK], col[CHUNK:2 * CHUNK])
    right = jnp.where(lane_lo, col[2 * CHUNK:3 * CHUNK], col[3 * CHUNK:])
    return jnp.concatenate([left, right], axis=1)


def _block_diag(wide_b, bd_mask):
    return jnp.where(bd_mask, jnp.concatenate([wide_b] * MIX_CHUNKS, axis=0), jnp.zeros((), wide_b.dtype))


def _mixer_body(x_ref, mg_ref, win_ref, wba_ref, lng_ref, lnb_ref, wcat_ref, bias_ref,
                cw_ref, alog_ref, dtb_ref, dng_ref, wout_ref, o_ref, s_ref, tail_ref):
    n_seq = x_ref.shape[0]
    rows = MIX_ROWS
    all_rows = n_seq * rows
    n_blocks = all_rows // GM_BLOCK
    chains = [(b, h) for b in range(n_seq) for h in range(DN_HEADS)]

    @pl.when(pl.program_id(0) == 0)
    def _():
        s_ref[...] = jnp.zeros(s_ref.shape, F32)
        tail_ref[...] = jnp.zeros(tail_ref.shape, F32)

    x = x_ref[...].reshape(all_rows, D_MODEL)
    hb = _rmsnorm(x, mg_ref[...]).astype(BF16)

    u_a = jax.nn.gelu(jnp.dot(hb, win_ref[:, 0:GM_WIDTH], **ACC))
    v_a = jax.nn.gelu(jnp.dot(hb, win_ref[:, GM_WIDTH:2 * GM_WIDTH], **ACC))
    mu = jnp.mean(v_a, axis=-1, keepdims=True)
    vc = v_a - mu
    var = jnp.mean(vc * vc, axis=-1, keepdims=True)
    v_n = (vc * lax.rsqrt(var + EPS)) * lng_ref[...] + lnb_ref[...]

    wi = lax.broadcasted_iota(jnp.int32, (GM_BLOCK, 2 * GM_BLOCK), 0) // CHUNK
    wj = (lax.broadcasted_iota(jnp.int32, (GM_BLOCK, 2 * GM_BLOCK), 1) % GM_BLOCK) // CHUNK
    w_mask = wj <= wi
    lane_lo128 = lax.broadcasted_iota(jnp.int32, (GM_BLOCK, LANES), 1) < GM_GROUP_DIM
    mixed_cols = []
    for p in range(GM_GROUPS // 2):
        v_p = v_n[:, p * LANES:(p + 1) * LANES]
        rhs = []
        for blk_i in range(n_blocks):
            blk = v_p[blk_i * GM_BLOCK:(blk_i + 1) * GM_BLOCK]
            rhs.append(jnp.concatenate(
                [jnp.where(lane_lo128, blk, 0.0), jnp.where(lane_lo128, 0.0, blk)], axis=0))
        rhs = jnp.concatenate(rhs, axis=1).astype(BF16)
        w_p = jnp.where(w_mask, wcat_ref[p], 0.0).astype(BF16)
        res = jnp.dot(w_p, rhs, **ACC)
        mixed_cols.append(jnp.concatenate(
            [res[:, i * LANES:(i + 1) * LANES] for i in range(n_blocks)], axis=0))
    bias = jnp.concatenate([bias_ref[...]] * n_blocks, axis=0)
    y_a = u_a * (jnp.concatenate(mixed_cols, axis=1) + bias)

    qkv_pre = jnp.dot(hb, win_ref[:, 2 * GM_WIDTH:2 * GM_WIDTH + 3 * DN_WIDTH], **ACC)
    conv = []
    for b in range(n_seq):
        cur = qkv_pre[b * rows:(b + 1) * rows]
        window = jnp.concatenate([tail_ref[b], cur], axis=0)
        tail_ref[b] = cur[rows - CONV_TAIL:]
        acc = cw_ref[DN_CONV - 1:DN_CONV, :] * cur
        for k in range(DN_CONV - 1):
            shifted = pltpu.roll(window, DN_CONV - 1 - k, 0)[CONV_TAIL:]
            acc = acc + cw_ref[k:k + 1, :] * shifted
        conv.append(acc)
    qkv = jax.nn.silu(jnp.concatenate(conv, axis=0))
    zg = jax.nn.silu(jnp.dot(hb, win_ref[:, 2 * GM_WIDTH + 3 * DN_WIDTH:], **ACC))

    pba = jnp.dot(hb, wba_ref[...], **ACC)
    beta_all = jax.nn.sigmoid(pba)
    g_all = -jnp.exp(alog_ref[...]) * _softplus(pba + dtb_ref[...])

    ri = lax.broadcasted_iota(jnp.int32, (rows, rows), 0)
    ci = lax.broadcasted_iota(jnp.int32, (rows, rows), 1)
    bd_mask = (ri // CHUNK) == (ci // CHUNK)
    tri_b = jnp.where(bd_mask & (ci <= ri), 1.0, 0.0).astype(BF16)
    blk_b = jnp.where(bd_mask, 1.0, 0.0).astype(BF16)
    g_hi, g_lo = _split_hi_lo(jnp.concatenate([g_all[b * rows:(b + 1) * rows] for b in range(n_seq)], axis=1))
    gc_all = jnp.dot(tri_b, g_hi, **ACC) + jnp.dot(tri_b, g_lo, **ACC)
    gl_all = jnp.dot(blk_b, g_hi, **ACC) + jnp.dot(blk_b, g_lo, **ACC)
    e_gc_all = jnp.exp(gc_all)
    e_glgc_all = jnp.exp(gl_all - gc_all)
    e_gl_all = jnp.exp(gl_all)
    gc_rows = [gc_all[:, b * LANES:(b + 1) * LANES].T for b in range(n_seq)]

    wr = lax.broadcasted_iota(jnp.int32, (CHUNK, rows), 0)
    wc = lax.broadcasted_iota(jnp.int32, (CHUNK, rows), 1) % CHUNK
    tri_w = wc <= wr
    strict_w = wc < wr
    eye_w = jnp.where(wc == wr, 1.0, 0.0)
    lane_lo = lax.broadcasted_iota(jnp.int32, (CHUNK, LANES), 1) < CHUNK
    scale = DN_HEAD_DIM ** -0.5

    k_dec, q_dec, rhs_uw, m_w, attn_bd, e_gl = {}, {}, {}, {}, {}, {}
    for (b, h) in chains:
        rs = slice(b * rows, (b + 1) * rows)
        col = b * LANES + DN_HEADS + h
        q = qkv[rs, h * DN_HEAD_DIM:(h + 1) * DN_HEAD_DIM]
        k = qkv[rs, DN_WIDTH + h * DN_HEAD_DIM:DN_WIDTH + (h + 1) * DN_HEAD_DIM]
        v = qkv[rs, 2 * DN_WIDTH + h * DN_HEAD_DIM:2 * DN_WIDTH + (h + 1) * DN_HEAD_DIM]
        q = q * lax.rsqrt(jnp.sum(q * q, axis=-1, keepdims=True) + EPS) * scale
        k = k * lax.rsqrt(jnp.sum(k * k, axis=-1, keepdims=True) + EPS)
        beta = _lane_bcast(beta_all[rs], h)
        e_gc = _lane_bcast(e_gc_all, col)
        k_beta = k * beta
        kb = k.astype(BF16)
        kq = lax.dot_general(jnp.concatenate([k_beta, q], axis=0).astype(BF16), kb, _NT, **ACC)
        dlog = (_col_to_wide(_lane_bcast(gc_all, col), lane_lo)
                - jnp.broadcast_to(gc_rows[b][DN_HEADS + h:DN_HEADS + h + 1, :], (CHUNK, rows)))
        decay = jnp.exp(jnp.where(tri_w, dlog, -jnp.inf))
        m_w[b, h] = -jnp.where(strict_w, _to_wide(kq[:rows], lane_lo) * decay, 0.0)
        attn_bd[b, h] = _block_diag((_to_wide(kq[rows:], lane_lo) * decay).astype(BF16), bd_mask)
        rhs_uw[b, h] = jnp.concatenate([v * beta, k_beta * e_gc], axis=1).astype(BF16)
        q_dec[b, h] = q * e_gc
        k_dec[b, h] = (k * _lane_bcast(e_glgc_all, col)).astype(BF16)
        e_gl[b, h] = _lane_bcast(e_gl_all, col)

    inv_w = {c: eye_w + m_w[c] for c in chains}
    m_bd = {c: _block_diag(m_w[c].astype(BF16), bd_mask) for c in chains}
    for _ in range(5):
        for c in chains:
            m_w[c] = jnp.dot(m_w[c].astype(BF16), m_bd[c], **ACC)
        for c in chains:
            m_bd[c] = _block_diag(m_w[c].astype(BF16), bd_mask)
        for c in chains:
            inv_w[c] = inv_w[c] + jnp.dot(inv_w[c].astype(BF16), m_bd[c], **ACC)
    uw = {c: jnp.dot(_block_diag(inv_w[c].astype(BF16), bd_mask), rhs_uw[c], **ACC) for c in chains}

    state = {(b, h): s_ref[b, h] for (b, h) in chains}
    v_new = {c: [] for c in chains}
    o_inter = {c: [] for c in chains}
    for ch in range(MIX_CHUNKS):
        cs = slice(ch * CHUNK, (ch + 1) * CHUNK)
        r = {}
        for c in chains:
            wq = jnp.concatenate([uw[c][cs, DN_HEAD_DIM:], q_dec[c][cs]], axis=0).astype(BF16)
            r[c] = jnp.dot(wq, state[c].astype(BF16), **ACC)
        for c in chains:
            vn = uw[c][cs, :DN_HEAD_DIM] - r[c][:CHUNK]
            v_new[c].append(vn)
            o_inter[c].append(r[c][CHUNK:])
            decay_s = jnp.concatenate([e_gl[c][cs]] * (DN_HEAD_DIM // CHUNK), axis=0)
            state[c] = state[c] * decay_s + lax.dot_general(k_dec[c][cs], vn.astype(BF16), _TN, **ACC)
    y_b = []
    for b in range(n_seq):
        heads = []
        for h in range(DN_HEADS):
            c = (b, h)
            s_ref[b, h] = state[c]
            o = (jnp.concatenate(o_inter[c], axis=0)
                 + jnp.dot(attn_bd[c], jnp.concatenate(v_new[c], axis=0).astype(BF16), **ACC))
            heads.append(_rmsnorm(o, dng_ref[...]) * zg[b * rows:(b + 1) * rows, h * DN_HEAD_DIM:(h + 1) * DN_HEAD_DIM])
        y_b.append(jnp.concatenate(heads, axis=1))
    y = jnp.concatenate([y_a, jnp.concatenate(y_b, axis=0)], axis=1).astype(BF16)
    o_ref[...] = (x + jnp.dot(y, wout_ref[...], **ACC)).reshape(o_ref.shape)


def _mixer_call(x, mix_norm, w_in, gm_ln_g, gm_ln_b, gm_w_s, gm_b_s, dn_conv_w, dn_a_log,
                dn_dt_bias, dn_norm, w_out):
    batch, seq, _ = x.shape
    n_main = 2 * GM_WIDTH + 4 * DN_WIDTH
    w_main = w_in[:, :n_main].astype(BF16)
    w_ba = jnp.pad(w_in[:, n_main:], ((0, 0), (0, LANES - 2 * DN_HEADS))).astype(BF16)
    w_cat = gm_w_s.reshape(GM_GROUPS // 2, 2, GM_BLOCK, GM_BLOCK).transpose(0, 2, 1, 3)
    w_cat = w_cat.reshape(GM_GROUPS // 2, GM_BLOCK, 2 * GM_BLOCK)
    bias = jnp.repeat(gm_b_s.T, GM_GROUP_DIM, axis=1)
    pad_a = lambda a: jnp.pad(a.reshape(1, DN_HEADS), ((0, 0), (DN_HEADS, LANES - 2 * DN_HEADS)))
    const2 = lambda t: (0, 0)
    in_specs = [
        pl.BlockSpec((batch, MIX_ROWS, D_MODEL), lambda t: (0, t, 0)),
        pl.BlockSpec((1, D_MODEL), const2),
        pl.BlockSpec((D_MODEL, n_main), const2),
        pl.BlockSpec((D_MODEL, LANES), const2),
        pl.BlockSpec((1, GM_WIDTH), const2),
        pl.BlockSpec((1, GM_WIDTH), const2),
        pl.BlockSpec((GM_GROUPS // 2, GM_BLOCK, 2 * GM_BLOCK), lambda t: (0, 0, 0)),
        pl.BlockSpec((GM_BLOCK, GM_WIDTH), const2),
        pl.BlockSpec((DN_CONV, 3 * DN_WIDTH), const2),
        pl.BlockSpec((1, LANES), const2),
        pl.BlockSpec((1, LANES), const2),
        pl.BlockSpec((1, DN_HEAD_DIM), const2),
        pl.BlockSpec((D_MODEL, D_MODEL), const2),
    ]
    return pl.pallas_call(
        _mixer_body,
        grid=(seq // MIX_ROWS,),
        in_specs=in_specs,
        out_specs=pl.BlockSpec((batch, MIX_ROWS, D_MODEL), lambda t: (0, t, 0)),
        out_shape=jax.ShapeDtypeStruct((batch, seq, D_MODEL), F32),
        scratch_shapes=[
            pltpu.VMEM((batch, DN_HEADS, DN_HEAD_DIM, DN_HEAD_DIM), F32),
            pltpu.VMEM((batch, CONV_TAIL, 3 * DN_WIDTH), F32),
        ],
        compiler_params=pltpu.CompilerParams(
            dimension_semantics=("arbitrary",), vmem_limit_bytes=VMEM_LIMIT_BYTES),
        name="mixer",
    )(x, mix_norm.reshape(1, D_MODEL), w_main, w_ba, gm_ln_g.reshape(1, GM_WIDTH),
      gm_ln_b.reshape(1, GM_WIDTH), w_cat, bias, dn_conv_w, pad_a(dn_a_log), pad_a(dn_dt_bias),
      dn_norm.reshape(1, DN_HEAD_DIM), w_out.astype(BF16))


def kernel(x, ffn1_norm, ffn1_w_gate, ffn1_w_up, ffn1_w_down, mix_norm, w_in, gm_ln_g, gm_ln_b,
           gm_w_s, gm_b_s, dn_conv_w, dn_a_log, dn_dt_bias, dn_norm, w_out, ffn2_norm,
           ffn2_w_gate, ffn2_w_up, ffn2_w_down, final_norm):
    batch, seq, d = x.shape
    depth = ffn1_norm.shape[0]
    for l in range(depth):
        x = _ffn_call(x.reshape(batch * seq, d), ffn1_norm[l], ffn1_w_gate[l], ffn1_w_up[l],
                      ffn1_w_down[l]).reshape(batch, seq, d)
        x = _mixer_call(x, mix_norm[l], w_in[l], gm_ln_g[l], gm_ln_b[l], gm_w_s[l], gm_b_s[l],
                        dn_conv_w[l], dn_a_log[l], dn_dt_bias[l], dn_norm[l], w_out[l])
        last = l == depth - 1
        x = _ffn_call(x.reshape(batch * seq, d), ffn2_norm[l], ffn2_w_gate[l], ffn2_w_up[l],
                      ffn2_w_down[l], final_norm if last else None).reshape(batch, seq, d)
    return x
```

```python
import functools

import jax
import jax.numpy as jnp
from jax import lax
from jax.experimental import pallas as pl
from jax.experimental.pallas import tpu as pltpu

F32 = jnp.float32
BF16 = jnp.bfloat16
ACC = dict(preferred_element_type=jnp.float32)

D_MODEL = 1024
D_FF = 2816
CHUNK = 64
GM_GROUPS = 8
GM_GROUP_DIM = 64
GM_WIDTH = GM_GROUPS * GM_GROUP_DIM
GM_BLOCK = 128
DN_HEADS = 4
DN_HEAD_DIM = 128
DN_WIDTH = DN_HEADS * DN_HEAD_DIM
DN_CONV = 4
EPS = 1e-6

LANES = 128
SUBLANES = 8
VMEM_LIMIT_BYTES = 60000 * 1024

FFN_ROWS = 512
FFN_COLS = 256
MIX_ROWS = 256
MIX_CHUNKS = MIX_ROWS // CHUNK
CONV_TAIL = SUBLANES
PROJ_COLS = 256

_NT = (((1,), (1,)), ((), ()))
_TN = (((0,), (0,)), ((), ()))


def _rmsnorm(x, g):
    return (x * lax.rsqrt(jnp.mean(x * x, axis=-1, keepdims=True) + EPS)) * g


def _softplus(x):
    return jnp.maximum(x, 0.0) + jnp.log1p(jnp.exp(-jnp.abs(x)))


def _ffn_body(x_ref, g_ref, wg_ref, wu_ref, wd_ref, *rest, final_norm):
    if final_norm:
        fg_ref, o_ref = rest
    else:
        (o_ref,) = rest
    x = x_ref[...]
    hb = _rmsnorm(x, g_ref[...]).astype(BF16)
    acc = jnp.zeros(x.shape, F32)
    for j in range(D_FF // FFN_COLS):
        cols = slice(j * FFN_COLS, (j + 1) * FFN_COLS)
        gate = jnp.dot(hb, wg_ref[:, cols], **ACC)
        up = jnp.dot(hb, wu_ref[:, cols], **ACC)
        act = (jax.nn.silu(gate) * up).astype(BF16)
        acc = acc + jnp.dot(act, wd_ref[cols, :], **ACC)
    y = x + 0.5 * acc
    if final_norm:
        y = _rmsnorm(y, fg_ref[...])
    o_ref[...] = y


def _ffn_call(x2d, gain, w_gate, w_up, w_down, final_gain=None):
    rows = x2d.shape[0]
    final_norm = final_gain is not None
    const = lambda i: (0, 0)
    in_specs = [
        pl.BlockSpec((FFN_ROWS, D_MODEL), lambda i: (i, 0)),
        pl.BlockSpec((1, D_MODEL), const),
        pl.BlockSpec((D_MODEL, D_FF), const),
        pl.BlockSpec((D_MODEL, D_FF), const),
        pl.BlockSpec((D_FF, D_MODEL), const),
    ]
    args = [x2d, gain.reshape(1, D_MODEL), w_gate.astype(BF16), w_up.astype(BF16),
            w_down.astype(BF16)]
    if final_norm:
        in_specs.append(pl.BlockSpec((1, D_MODEL), const))
        args.append(final_gain.reshape(1, D_MODEL))
    return pl.pallas_call(
        functools.partial(_ffn_body, final_norm=final_norm),
        grid=(rows // FFN_ROWS,),
        in_specs=in_specs,
        out_specs=pl.BlockSpec((FFN_ROWS, D_MODEL), lambda i: (i, 0)),
        out_shape=jax.ShapeDtypeStruct((rows, D_MODEL), F32),
        compiler_params=pltpu.CompilerParams(
            dimension_semantics=("arbitrary",), vmem_limit_bytes=VMEM_LIMIT_BYTES),
        name="ffn_final" if final_norm else "ffn",
    )(*args)


def _split_hi_lo(x):
    hi = x.astype(BF16)
    lo = (x - hi.astype(F32)).astype(BF16)
    return hi, lo


def _lane_bcast(x, col):
    return jnp.broadcast_to(x[:, col:col + 1], (x.shape[0], LANES))


def _to_wide(full, lane_lo):
    left = jnp.where(lane_lo, full[0:CHUNK, 0:LANES], full[CHUNK:2 * CHUNK, 0:LANES])
    right = jnp.where(lane_lo, full[2 * CHUNK:3 * CHUNK, LANES:], full[3 * CHUNK:, LANES:])
    return jnp.concatenate([left, right], axis=1)


def _col_to_wide(col, lane_lo):
    left = jnp.where(lane_lo, col[0:CHUNK], col[CHUNK:2 * CHUNK])
    right = jnp.where(lane_lo, col[2 * CHUNK:3 * CHUNK], col[3 * CHUNK:])
    return jnp.concatenate([left, right], axis=1)


def _block_diag(wide_b, bd_mask):
    return jnp.where(bd_mask, jnp.concatenate([wide_b] * MIX_CHUNKS, axis=0), jnp.zeros((), wide_b.dtype))


def _handoff_shapes(n_seq):
    n_chain = n_seq * DN_HEADS
    return [
        pltpu.VMEM((n_seq * MIX_ROWS, GM_WIDTH), BF16),
        pltpu.VMEM((n_seq * MIX_ROWS, DN_WIDTH), F32),
        pltpu.VMEM((n_chain, CHUNK, MIX_ROWS), F32),
        pltpu.VMEM((n_chain, CHUNK, MIX_ROWS), BF16),
        pltpu.VMEM((n_chain, MIX_ROWS, 2 * DN_HEAD_DIM), BF16),
        pltpu.VMEM((n_chain, MIX_ROWS, DN_HEAD_DIM), BF16),
        pltpu.VMEM((n_chain, MIX_ROWS, DN_HEAD_DIM), BF16),
        pltpu.VMEM((n_chain, MIX_ROWS, DN_HEAD_DIM), F32),
    ]


N_HANDOFF = 8


def _mixer_stage_a(x_ref, mg_ref, win_ref, wba_ref, lng_ref, lnb_ref, wcat_ref, bias_ref, cw_ref,
                   alog_ref, dtb_ref, tail_ref, out):
    ya_ref, zg_ref, m_ref, attn_ref, rhs_ref, qd_ref, kd_ref, egl_ref = out
    n_seq = x_ref.shape[0]
    rows = MIX_ROWS
    all_rows = n_seq * rows
    n_blocks = all_rows // GM_BLOCK

    x = x_ref[...].reshape(all_rows, D_MODEL)
    hb = _rmsnorm(x, mg_ref[...]).astype(BF16)
    yield

    def project(lo, hi):
        parts = []
        for c0 in range(lo, hi, PROJ_COLS):
            parts.append(jnp.dot(hb, win_ref[:, c0:min(c0 + PROJ_COLS, hi)], **ACC))
            yield
        return jnp.concatenate(parts, axis=1)

    qkv_pre = yield from project(2 * GM_WIDTH, 2 * GM_WIDTH + 3 * DN_WIDTH)
    pba = jnp.dot(hb, wba_ref[...], **ACC)
    yield
    conv = []
    for b in range(n_seq):
        cur = qkv_pre[b * rows:(b + 1) * rows]
        window = jnp.concatenate([tail_ref[b], cur], axis=0)
        tail_ref[b] = cur[rows - CONV_TAIL:]
        acc = cw_ref[DN_CONV - 1:DN_CONV, :] * cur
        for k in range(DN_CONV - 1):
            shifted = pltpu.roll(window, DN_CONV - 1 - k, 0)[CONV_TAIL:]
            acc = acc + cw_ref[k:k + 1, :] * shifted
        conv.append(acc)
    qkv = jax.nn.silu(jnp.concatenate(conv, axis=0))
    beta_all = jax.nn.sigmoid(pba)
    g_all = -jnp.exp(alog_ref[...]) * _softplus(pba + dtb_ref[...])

    u_a = jax.nn.gelu((yield from project(0, GM_WIDTH)))
    v_a = jax.nn.gelu((yield from project(GM_WIDTH, 2 * GM_WIDTH)))
    zg_ref[...] = jax.nn.silu((yield from project(2 * GM_WIDTH + 3 * DN_WIDTH, 2 * GM_WIDTH + 4 * DN_WIDTH)))

    ri = lax.broadcasted_iota(jnp.int32, (rows, rows), 0)
    ci = lax.broadcasted_iota(jnp.int32, (rows, rows), 1)
    bd_mask = (ri // CHUNK) == (ci // CHUNK)
    tri_b = jnp.where(bd_mask & (ci <= ri), 1.0, 0.0).astype(BF16)
    blk_b = jnp.where(bd_mask, 1.0, 0.0).astype(BF16)
    g_hi, g_lo = _split_hi_lo(jnp.concatenate([g_all[b * rows:(b + 1) * rows] for b in range(n_seq)], axis=1))
    gc_all = jnp.dot(tri_b, g_hi, **ACC) + jnp.dot(tri_b, g_lo, **ACC)
    gl_all = jnp.dot(blk_b, g_hi, **ACC) + jnp.dot(blk_b, g_lo, **ACC)
    yield
    e_gc_all = jnp.exp(gc_all)
    e_glgc_all = jnp.exp(gl_all - gc_all)
    e_gl_all = jnp.exp(gl_all)
    gc_rows = [gc_all[:, b * LANES:(b + 1) * LANES].T for b in range(n_seq)]

    mu = jnp.mean(v_a, axis=-1, keepdims=True)
    vc = v_a - mu
    var = jnp.mean(vc * vc, axis=-1, keepdims=True)
    v_n = (vc * lax.rsqrt(var + EPS)) * lng_ref[...] + lnb_ref[...]

    wi = lax.broadcasted_iota(jnp.int32, (GM_BLOCK, 2 * GM_BLOCK), 0) // CHUNK
    wj = (lax.broadcasted_iota(jnp.int32, (GM_BLOCK, 2 * GM_BLOCK), 1) % GM_BLOCK) // CHUNK
    w_mask = wj <= wi
    lane_lo128 = lax.broadcasted_iota(jnp.int32, (GM_BLOCK, LANES), 1) < GM_GROUP_DIM
    mixed_cols = []
    for p in range(GM_GROUPS // 2):
        v_p = v_n[:, p * LANES:(p + 1) * LANES]
        rhs = []
        for blk_i in range(n_blocks):
            blk = v_p[blk_i * GM_BLOCK:(blk_i + 1) * GM_BLOCK]
            rhs.append(jnp.concatenate(
                [jnp.where(lane_lo128, blk, 0.0), jnp.where(lane_lo128, 0.0, blk)], axis=0))
        rhs = jnp.concatenate(rhs, axis=1).astype(BF16)
        w_p = jnp.where(w_mask, wcat_ref[p], 0.0).astype(BF16)
        res = jnp.dot(w_p, rhs, **ACC)
        mixed_cols.append(jnp.concatenate(
            [res[:, i * LANES:(i + 1) * LANES] for i in range(n_blocks)], axis=0))
        if p % 2 == 1:
            yield
    bias = jnp.concatenate([bias_ref[...]] * n_blocks, axis=0)
    ya_ref[...] = (u_a * (jnp.concatenate(mixed_cols, axis=1) + bias)).astype(BF16)

    wr = lax.broadcasted_iota(jnp.int32, (CHUNK, rows), 0)
    wc = lax.broadcasted_iota(jnp.int32, (CHUNK, rows), 1) % CHUNK
    tri_w = wc <= wr
    strict_w = wc < wr
    lane_lo = lax.broadcasted_iota(jnp.int32, (CHUNK, LANES), 1) < CHUNK
    scale = DN_HEAD_DIM ** -0.5

    for b in range(n_seq):
        for h in range(DN_HEADS):
            chain = b * DN_HEADS + h
            rs = slice(b * rows, (b + 1) * rows)
            col = b * LANES + DN_HEADS + h
            q = qkv[rs, h * DN_HEAD_DIM:(h + 1) * DN_HEAD_DIM]
            k = qkv[rs, DN_WIDTH + h * DN_HEAD_DIM:DN_WIDTH + (h + 1) * DN_HEAD_DIM]
            v = qkv[rs, 2 * DN_WIDTH + h * DN_HEAD_DIM:2 * DN_WIDTH + (h + 1) * DN_HEAD_DIM]
            q = q * lax.rsqrt(jnp.sum(q * q, axis=-1, keepdims=True) + EPS) * scale
            k = k * lax.rsqrt(jnp.sum(k * k, axis=-1, keepdims=True) + EPS)
            beta = _lane_bcast(beta_all[rs], h)
            e_gc = _lane_bcast(e_gc_all, col)
            k_beta = k * beta
            kq = lax.dot_general(jnp.concatenate([k_beta, q], axis=0).astype(BF16), k.astype(BF16),
                                 _NT, **ACC)
            dlog = (_col_to_wide(_lane_bcast(gc_all, col), lane_lo)
                    - jnp.broadcast_to(gc_rows[b][DN_HEADS + h:DN_HEADS + h + 1, :], (CHUNK, rows)))
            decay = jnp.exp(jnp.where(tri_w, dlog, -jnp.inf))
            m_ref[chain] = -jnp.where(strict_w, _to_wide(kq[:rows], lane_lo) * decay, 0.0)
            attn_ref[chain] = (_to_wide(kq[rows:], lane_lo) * decay).astype(BF16)
            rhs_ref[chain] = jnp.concatenate([v * beta, k_beta * e_gc], axis=1).astype(BF16)
            qd_ref[chain] = (q * e_gc).astype(BF16)
            kd_ref[chain] = (k * _lane_bcast(e_glgc_all, col)).astype(BF16)
            egl_ref[chain] = _lane_bcast(e_gl_all, col)
            if h % 2 == 1:
                yield


def _mixer_stage_b(x_ref, dng_ref, wout_ref, s_ref, o_ref, inp):
    ya_ref, zg_ref, m_ref, attn_ref, rhs_ref, qd_ref, kd_ref, egl_ref = inp
    n_seq = x_ref.shape[0]
    rows = MIX_ROWS
    all_rows = n_seq * rows
    chains = range(n_seq * DN_HEADS)

    ri = lax.broadcasted_iota(jnp.int32, (rows, rows), 0)
    ci = lax.broadcasted_iota(jnp.int32, (rows, rows), 1)
    bd_mask = (ri // CHUNK) == (ci // CHUNK)
    wr = lax.broadcasted_iota(jnp.int32, (CHUNK, rows), 0)
    wc = lax.broadcasted_iota(jnp.int32, (CHUNK, rows), 1) % CHUNK
    eye_w = jnp.where(wc == wr, 1.0, 0.0)

    m_w = {c: m_ref[c] for c in chains}
    inv_w = {c: eye_w + m_w[c] for c in chains}
    m_bd = {c: _block_diag(m_w[c].astype(BF16), bd_mask) for c in chains}
    for _ in range(5):
        for c in chains:
            m_w[c] = jnp.dot(m_w[c].astype(BF16), m_bd[c], **ACC)
        yield
        for c in chains:
            m_bd[c] = _block_diag(m_w[c].astype(BF16), bd_mask)
        for c in chains:
            inv_w[c] = inv_w[c] + jnp.dot(inv_w[c].astype(BF16), m_bd[c], **ACC)
        yield
    uw = {c: jnp.dot(_block_diag(inv_w[c].astype(BF16), bd_mask), rhs_ref[c], **ACC) for c in chains}
    yield

    state = {c: s_ref[c] for c in chains}
    v_new = {c: [] for c in chains}
    o_inter = {c: [] for c in chains}
    for ch in range(MIX_CHUNKS):
        cs = slice(ch * CHUNK, (ch + 1) * CHUNK)
        r = {}
        for c in chains:
            wq = jnp.concatenate([uw[c][cs, DN_HEAD_DIM:].astype(BF16), qd_ref[c, cs, :]], axis=0)
            r[c] = jnp.dot(wq, state[c].astype(BF16), **ACC)
        yield
        for c in chains:
            vn = uw[c][cs, :DN_HEAD_DIM] - r[c][:CHUNK]
            v_new[c].append(vn)
            o_inter[c].append(r[c][CHUNK:])
            decay_s = jnp.concatenate([egl_ref[c, cs, :]] * (DN_HEAD_DIM // CHUNK), axis=0)
            state[c] = state[c] * decay_s + lax.dot_general(kd_ref[c, cs, :], vn.astype(BF16), _TN, **ACC)
        yield
    y_b = []
    for b in range(n_seq):
        heads = []
        for h in range(DN_HEADS):
            c = b * DN_HEADS + h
            s_ref[c] = state[c]
            o = (jnp.concatenate(o_inter[c], axis=0)
                 + jnp.dot(_block_diag(attn_ref[c], bd_mask),
                           jnp.concatenate(v_new[c], axis=0).astype(BF16), **ACC))
            heads.append(_rmsnorm(o, dng_ref[...])
                         * zg_ref[b * rows:(b + 1) * rows, h * DN_HEAD_DIM:(h + 1) * DN_HEAD_DIM])
        y_b.append(jnp.concatenate(heads, axis=1).astype(BF16))
        yield
    y = jnp.concatenate([ya_ref[...], jnp.concatenate(y_b, axis=0)], axis=1)
    for c0 in range(0, D_MODEL, PROJ_COLS):
        cols = slice(c0, c0 + PROJ_COLS)
        x = x_ref[:, :, cols].reshape(all_rows, PROJ_COLS)
        o_ref[:, :, cols] = (x + jnp.dot(y, wout_ref[:, cols], **ACC)).reshape(n_seq, rows, PROJ_COLS)
        yield


def _round_robin(*stages):
    stages = list(stages)
    while stages:
        for stage in list(stages):
            try:
                next(stage)
            except StopIteration:
                stages.remove(stage)


def _mixer_body(xa_ref, xb_ref, mg_ref, win_ref, wba_ref, lng_ref, lnb_ref, wcat_ref, bias_ref,
                cw_ref, alog_ref, dtb_ref, dng_ref, wout_ref, o_ref, s_ref, tail_ref, *handoff):
    t = pl.program_id(0)
    sets = (handoff[:N_HANDOFF], handoff[N_HANDOFF:])

    @pl.when(t == 0)
    def _():
        s_ref[...] = jnp.zeros(s_ref.shape, F32)
        tail_ref[...] = jnp.zeros(tail_ref.shape, F32)
        for ref in sets[1]:
            ref[...] = jnp.zeros(ref.shape, ref.dtype)

    def step(write_set, read_set):
        _round_robin(
            _mixer_stage_b(xb_ref, dng_ref, wout_ref, s_ref, o_ref, read_set),
            _mixer_stage_a(xa_ref, mg_ref, win_ref, wba_ref, lng_ref, lnb_ref, wcat_ref, bias_ref,
                           cw_ref, alog_ref, dtb_ref, tail_ref, write_set))

    @pl.when(t % 2 == 0)
    def _():
        step(sets[0], sets[1])

    @pl.when(t % 2 == 1)
    def _():
        step(sets[1], sets[0])


def _mixer_call(x, mix_norm, w_in, gm_ln_g, gm_ln_b, gm_w_s, gm_b_s, dn_conv_w, dn_a_log,
                dn_dt_bias, dn_norm, w_out):
    batch, seq, _ = x.shape
    n_tiles = seq // MIX_ROWS
    n_main = 2 * GM_WIDTH + 4 * DN_WIDTH
    w_main = w_in[:, :n_main].astype(BF16)
    w_ba = jnp.pad(w_in[:, n_main:], ((0, 0), (0, LANES - 2 * DN_HEADS))).astype(BF16)
    w_cat = gm_w_s.reshape(GM_GROUPS // 2, 2, GM_BLOCK, GM_BLOCK).transpose(0, 2, 1, 3)
    w_cat = w_cat.reshape(GM_GROUPS // 2, GM_BLOCK, 2 * GM_BLOCK)
    bias = jnp.repeat(gm_b_s.T, GM_GROUP_DIM, axis=1)
    pad_a = lambda a: jnp.pad(a.reshape(1, DN_HEADS), ((0, 0), (DN_HEADS, LANES - 2 * DN_HEADS)))
    const2 = lambda t: (0, 0)
    tile_a = lambda t: (0, jnp.minimum(t, n_tiles - 1), 0)
    tile_b = lambda t: (0, jnp.maximum(t - 1, 0), 0)
    in_specs = [
        pl.BlockSpec((batch, MIX_ROWS, D_MODEL), tile_a),
        pl.BlockSpec((batch, MIX_ROWS, D_MODEL), tile_b),
        pl.BlockSpec((1, D_MODEL), const2),
        pl.BlockSpec((D_MODEL, n_main), const2),
        pl.BlockSpec((D_MODEL, LANES), const2),
        pl.BlockSpec((1, GM_WIDTH), const2),
        pl.BlockSpec((1, GM_WIDTH), const2),
        pl.BlockSpec((GM_GROUPS // 2, GM_BLOCK, 2 * GM_BLOCK), lambda t: (0, 0, 0)),
        pl.BlockSpec((GM_BLOCK, GM_WIDTH), const2),
        pl.BlockSpec((DN_CONV, 3 * DN_WIDTH), const2),
        pl.BlockSpec((1, LANES), const2),
        pl.BlockSpec((1, LANES), const2),
        pl.BlockSpec((1, DN_HEAD_DIM), const2),
        pl.BlockSpec((D_MODEL, D_MODEL), const2),
    ]
    return pl.pallas_call(
        _mixer_body,
        grid=(n_tiles + 1,),
        in_specs=in_specs,
        out_specs=pl.BlockSpec((batch, MIX_ROWS, D_MODEL), tile_b),
        out_shape=jax.ShapeDtypeStruct((batch, seq, D_MODEL), F32),
        scratch_shapes=[
            pltpu.VMEM((batch * DN_HEADS, DN_HEAD_DIM, DN_HEAD_DIM), F32),
            pltpu.VMEM((batch, CONV_TAIL, 3 * DN_WIDTH), F32),
        ] + _handoff_shapes(batch) + _handoff_shapes(batch),
        compiler_params=pltpu.CompilerParams(
            dimension_semantics=("arbitrary",), vmem_limit_bytes=VMEM_LIMIT_BYTES),
        name="mixer",
    )(x, x, mix_norm.reshape(1, D_MODEL), w_main, w_ba, gm_ln_g.reshape(1, GM_WIDTH),
      gm_ln_b.reshape(1, GM_WIDTH), w_cat, bias, dn_conv_w, pad_a(dn_a_log), pad_a(dn_dt_bias),
      dn_norm.reshape(1, DN_HEAD_DIM), w_out.astype(BF16))


def kernel(x, ffn1_norm, ffn1_w_gate, ffn1_w_up, ffn1_w_down, mix_norm, w_in, gm_ln_g, gm_ln_b,
           gm_w_s, gm_b_s, dn_conv_w, dn_a_log, dn_dt_bias, dn_norm, w_out, ffn2_norm,
           ffn2_w_gate, ffn2_w_up, ffn2_w_down, final_norm):
    batch, seq, d = x.shape
    depth = ffn1_norm.shape[0]
    for l in range(depth):
        x = _ffn_call(x.reshape(batch * seq, d), ffn1_norm[l], ffn1_w_gate[l], ffn1_w_up[l],
                      ffn1_w_down[l]).reshape(batch, seq, d)
        x = _mixer_call(x, mix_norm[l], w_in[l], gm_ln_g[l], gm_ln_b[l], gm_w_s[l], gm_b_s[l],
                        dn_conv_w[l], dn_a_log[l], dn_dt_bias[l], dn_norm[l], w_out[l])
        last = l == depth - 1
        x = _ffn_call(x.reshape(batch * seq, d), ffn2_norm[l], ffn2_w_gate[l], ffn2_w_up[l],
                      ffn2_w_down[l], final_norm if last else None).reshape(batch, seq, d)
    return x
```

```python
import functools

import jax
import jax.numpy as jnp
from jax import lax
from jax.experimental import pallas as pl
from jax.experimental.pallas import tpu as pltpu

F32 = jnp.float32
BF16 = jnp.bfloat16
ACC = dict(preferred_element_type=jnp.float32)

D_MODEL = 1024
D_FF = 2816
CHUNK = 64
GM_GROUPS = 8
GM_GROUP_DIM = 64
GM_WIDTH = GM_GROUPS * GM_GROUP_DIM
GM_BLOCK = 128
DN_HEADS = 4
DN_HEAD_DIM = 128
DN_WIDTH = DN_HEADS * DN_HEAD_DIM
DN_CONV = 4
EPS = 1e-6

LANES = 128
SUBLANES = 8
VMEM_LIMIT_BYTES = 60000 * 1024

FFN_ROWS = 512
FFN_COLS = 256
MIX_ROWS = 256
MIX_CHUNKS = MIX_ROWS // CHUNK
CONV_TAIL = SUBLANES
PROJ_COLS = 256

_NT = (((1,), (1,)), ((), ()))
_TN = (((0,), (0,)), ((), ()))


def _rmsnorm(x, g):
    return (x * lax.rsqrt(jnp.mean(x * x, axis=-1, keepdims=True) + EPS)) * g


def _softplus(x):
    return jnp.maximum(x, 0.0) + jnp.log1p(jnp.exp(-jnp.abs(x)))


def _ffn_body(x_ref, g_ref, wg_ref, wu_ref, wd_ref, *rest, final_norm):
    if final_norm:
        fg_ref, o_ref = rest
    else:
        (o_ref,) = rest
    x = x_ref[...]
    hb = _rmsnorm(x, g_ref[...]).astype(BF16)
    acc = jnp.zeros(x.shape, F32)
    for j in range(D_FF // FFN_COLS):
        cols = slice(j * FFN_COLS, (j + 1) * FFN_COLS)
        gate = jnp.dot(hb, wg_ref[:, cols], **ACC)
        up = jnp.dot(hb, wu_ref[:, cols], **ACC)
        act = (jax.nn.silu(gate) * up).astype(BF16)
        acc = acc + jnp.dot(act, wd_ref[cols, :], **ACC)
    y = x + 0.5 * acc
    if final_norm:
        y = _rmsnorm(y, fg_ref[...])
    o_ref[...] = y


def _ffn_call(x2d, gain, w_gate, w_up, w_down, final_gain=None):
    rows = x2d.shape[0]
    final_norm = final_gain is not None
    const = lambda i: (0, 0)
    in_specs = [
        pl.BlockSpec((FFN_ROWS, D_MODEL), lambda i: (i, 0)),
        pl.BlockSpec((1, D_MODEL), const),
        pl.BlockSpec((D_MODEL, D_FF), const),
        pl.BlockSpec((D_MODEL, D_FF), const),
        pl.BlockSpec((D_FF, D_MODEL), const),
    ]
    args = [x2d, gain.reshape(1, D_MODEL), w_gate.astype(BF16), w_up.astype(BF16),
            w_down.astype(BF16)]
    if final_norm:
        in_specs.append(pl.BlockSpec((1, D_MODEL), const))
        args.append(final_gain.reshape(1, D_MODEL))
    return pl.pallas_call(
        functools.partial(_ffn_body, final_norm=final_norm),
        grid=(rows // FFN_ROWS,),
        in_specs=in_specs,
        out_specs=pl.BlockSpec((FFN_ROWS, D_MODEL), lambda i: (i, 0)),
        out_shape=jax.ShapeDtypeStruct((rows, D_MODEL), F32),
        compiler_params=pltpu.CompilerParams(
            dimension_semantics=("arbitrary",), vmem_limit_bytes=VMEM_LIMIT_BYTES),
        name="ffn_final" if final_norm else "ffn",
    )(*args)


def _split_hi_lo(x):
    hi = x.astype(BF16)
    lo = (x - hi.astype(F32)).astype(BF16)
    return hi, lo


def _lane_bcast(x, col):
    return jnp.broadcast_to(x[:, col:col + 1], (x.shape[0], LANES))


def _col_to_wide(col, lane_lo):
    left = jnp.where(lane_lo, col[0:CHUNK], col[CHUNK:2 * CHUNK])
    right = jnp.where(lane_lo, col[2 * CHUNK:3 * CHUNK], col[3 * CHUNK:])
    return jnp.concatenate([left, right], axis=1)


def _block_diag(wide_b, bd_mask):
    return jnp.where(bd_mask, jnp.concatenate([wide_b] * MIX_CHUNKS, axis=0), jnp.zeros((), wide_b.dtype))


def _handoff_shapes(n_seq):
    n_chain = n_seq * DN_HEADS
    return [
        pltpu.VMEM((n_seq * MIX_ROWS, GM_WIDTH), BF16),
        pltpu.VMEM((n_seq * MIX_ROWS, DN_WIDTH), F32),
        pltpu.VMEM((n_chain, CHUNK, MIX_ROWS), F32),
        pltpu.VMEM((n_chain, CHUNK, MIX_ROWS), BF16),
        pltpu.VMEM((n_chain, MIX_ROWS, 2 * DN_HEAD_DIM), BF16),
        pltpu.VMEM((n_chain, MIX_ROWS, DN_HEAD_DIM), BF16),
        pltpu.VMEM((n_chain, MIX_ROWS, DN_HEAD_DIM), BF16),
        pltpu.VMEM((n_chain, MIX_ROWS, DN_HEAD_DIM), F32),
    ]


N_HANDOFF = 8


def _mixer_stage_a(x_ref, mg_ref, win_ref, wba_ref, lng_ref, lnb_ref, wcat_ref, bias_ref, cw_ref,
                   alog_ref, dtb_ref, cin_ref, qkv_ref, out):
    ya_ref, zg_ref, m_ref, attn_ref, rhs_ref, qd_ref, kd_ref, egl_ref = out
    n_seq = x_ref.shape[0]
    rows = MIX_ROWS
    all_rows = n_seq * rows
    n_blocks = all_rows // GM_BLOCK

    x = x_ref[...].reshape(all_rows, D_MODEL)
    hb = _rmsnorm(x, mg_ref[...]).astype(BF16)
    yield

    def project(lo, hi):
        parts = []
        for c0 in range(lo, hi, PROJ_COLS):
            parts.append(jnp.dot(hb, win_ref[:, c0:min(c0 + PROJ_COLS, hi)], **ACC))
            yield
        return jnp.concatenate(parts, axis=1)

    for c0 in range(0, 3 * DN_WIDTH, PROJ_COLS):
        blk = jnp.dot(hb, win_ref[:, 2 * GM_WIDTH + c0:2 * GM_WIDTH + c0 + PROJ_COLS], **ACC)
        for b in range(n_seq):
            for j in range(PROJ_COLS // LANES):
                slab = cin_ref.at[b, c0 // LANES + j]
                slab[0:CONV_TAIL, :] = slab[rows:rows + CONV_TAIL, :]
                slab[CONV_TAIL:, :] = blk[b * rows:(b + 1) * rows, j * LANES:(j + 1) * LANES]
        yield
    pba = jnp.dot(hb, wba_ref[...], **ACC)
    yield
    half = rows // 2
    for b in range(n_seq):
        for s in range(3 * DN_WIDTH // LANES):
            src, dst = cin_ref.at[b, s], qkv_ref.at[b, s]
            w = cw_ref[:, s * LANES:(s + 1) * LANES]
            tap = lambda off: src[pl.ds(CONV_TAIL + off, half, stride=2), :]
            e0, e2, o1, om1, om3 = tap(0), tap(-2), tap(1), tap(-1), tap(-3)
            even = w[3:4] * e0 + w[2:3] * om1 + w[1:2] * e2 + w[0:1] * om3
            odd = w[3:4] * o1 + w[2:3] * e0 + w[1:2] * om1 + w[0:1] * e2
            dst[pl.ds(0, half, stride=2), :] = jax.nn.silu(even)
            dst[pl.ds(1, half, stride=2), :] = jax.nn.silu(odd)
    beta_all = jax.nn.sigmoid(pba)
    g_all = -jnp.exp(alog_ref[...]) * _softplus(pba + dtb_ref[...])

    u_a = jax.nn.gelu((yield from project(0, GM_WIDTH)))
    v_a = jax.nn.gelu((yield from project(GM_WIDTH, 2 * GM_WIDTH)))
    zg_ref[...] = jax.nn.silu((yield from project(2 * GM_WIDTH + 3 * DN_WIDTH, 2 * GM_WIDTH + 4 * DN_WIDTH)))

    ri = lax.broadcasted_iota(jnp.int32, (rows, rows), 0)
    ci = lax.broadcasted_iota(jnp.int32, (rows, rows), 1)
    bd_mask = (ri // CHUNK) == (ci // CHUNK)
    tri_b = jnp.where(bd_mask & (ci <= ri), 1.0, 0.0).astype(BF16)
    blk_b = jnp.where(bd_mask, 1.0, 0.0).astype(BF16)
    g_hi, g_lo = _split_hi_lo(jnp.concatenate([g_all[b * rows:(b + 1) * rows] for b in range(n_seq)], axis=1))
    gc_all = jnp.dot(tri_b, g_hi, **ACC) + jnp.dot(tri_b, g_lo, **ACC)
    gl_all = jnp.dot(blk_b, g_hi, **ACC) + jnp.dot(blk_b, g_lo, **ACC)
    yield
    e_gc_all = jnp.exp(gc_all)
    e_glgc_all = jnp.exp(gl_all - gc_all)
    e_gl_all = jnp.exp(gl_all)
    gc_rows = [gc_all[:, b * LANES:(b + 1) * LANES].T for b in range(n_seq)]

    mu = jnp.mean(v_a, axis=-1, keepdims=True)
    vc = v_a - mu
    var = jnp.mean(vc * vc, axis=-1, keepdims=True)
    v_n = (vc * lax.rsqrt(var + EPS)) * lng_ref[...] + lnb_ref[...]

    wi = lax.broadcasted_iota(jnp.int32, (GM_BLOCK, 2 * GM_BLOCK), 0) // CHUNK
    wj = (lax.broadcasted_iota(jnp.int32, (GM_BLOCK, 2 * GM_BLOCK), 1) % GM_BLOCK) // CHUNK
    w_mask = wj <= wi
    lane_lo128 = lax.broadcasted_iota(jnp.int32, (GM_BLOCK, LANES), 1) < GM_GROUP_DIM
    mixed_cols = []
    for p in range(GM_GROUPS // 2):
        v_p = v_n[:, p * LANES:(p + 1) * LANES]
        rhs = []
        for blk_i in range(n_blocks):
            blk = v_p[blk_i * GM_BLOCK:(blk_i + 1) * GM_BLOCK]
            rhs.append(jnp.concatenate(
                [jnp.where(lane_lo128, blk, 0.0), jnp.where(lane_lo128, 0.0, blk)], axis=0))
        rhs = jnp.concatenate(rhs, axis=1).astype(BF16)
        w_p = jnp.where(w_mask, wcat_ref[p], 0.0).astype(BF16)
        res = jnp.dot(w_p, rhs, **ACC)
        mixed_cols.append(jnp.concatenate(
            [res[:, i * LANES:(i + 1) * LANES] for i in range(n_blocks)], axis=0))
        if p % 2 == 1:
            yield
    bias = jnp.concatenate([bias_ref[...]] * n_blocks, axis=0)
    ya_ref[...] = (u_a * (jnp.concatenate(mixed_cols, axis=1) + bias)).astype(BF16)

    wr = lax.broadcasted_iota(jnp.int32, (CHUNK, rows), 0)
    wc = lax.broadcasted_iota(jnp.int32, (CHUNK, rows), 1) % CHUNK
    tri_w = wc <= wr
    strict_w = wc < wr
    lane_lo = lax.broadcasted_iota(jnp.int32, (CHUNK, LANES), 1) < CHUNK
    scale = DN_HEAD_DIM ** -0.5

    zeros_k = jnp.zeros((2 * CHUNK, DN_HEAD_DIM), BF16)
    for b in range(n_seq):
        rs = slice(b * rows, (b + 1) * rows)
        for h0 in range(0, DN_HEADS, 2):
            q, k, k_beta = {}, {}, {}
            for h in (h0, h0 + 1):
                chain = b * DN_HEADS + h
                col = b * LANES + DN_HEADS + h
                q_h, k_h, v_h = qkv_ref[b, h], qkv_ref[b, DN_HEADS + h], qkv_ref[b, 2 * DN_HEADS + h]
                q[h] = q_h * lax.rsqrt(jnp.sum(q_h * q_h, axis=-1, keepdims=True) + EPS) * scale
                k[h] = k_h * lax.rsqrt(jnp.sum(k_h * k_h, axis=-1, keepdims=True) + EPS)
                beta = _lane_bcast(beta_all[rs], h)
                e_gc = _lane_bcast(e_gc_all, col)
                k_beta[h] = k[h] * beta
                rhs_ref[chain] = jnp.concatenate([v_h * beta, k_beta[h] * e_gc], axis=1).astype(BF16)
                qd_ref[chain] = (q[h] * e_gc).astype(BF16)
                kd_ref[chain] = (k[h] * _lane_bcast(e_glgc_all, col)).astype(BF16)
                egl_ref[chain] = _lane_bcast(e_gl_all, col)
            kk_w = {h: [] for h in (h0, h0 + 1)}
            qk_w = {h: [] for h in (h0, h0 + 1)}
            for half_i in range(rows // LANES):
                hr = slice(half_i * LANES, (half_i + 1) * LANES)
                lhs = jnp.concatenate(
                    [jnp.concatenate([k_beta[h][hr], q[h][hr]], axis=0) for h in (h0, h0 + 1)], axis=1)
                kb0, kb1 = k[h0][hr].astype(BF16), k[h0 + 1][hr].astype(BF16)
                rhs_k = jnp.concatenate([jnp.concatenate([kb0, zeros_k], axis=1),
                                         jnp.concatenate([zeros_k, kb1], axis=1)], axis=0)
                out = lax.dot_general(lhs.astype(BF16), rhs_k, _NT, **ACC)
                for i, h in enumerate((h0, h0 + 1)):
                    kk = out[0:LANES, i * LANES:(i + 1) * LANES]
                    qk = out[LANES:, i * LANES:(i + 1) * LANES]
                    kk_w[h].append(jnp.where(lane_lo, kk[0:CHUNK], kk[CHUNK:]))
                    qk_w[h].append(jnp.where(lane_lo, qk[0:CHUNK], qk[CHUNK:]))
            for h in (h0, h0 + 1):
                chain = b * DN_HEADS + h
                col = b * LANES + DN_HEADS + h
                dlog = (_col_to_wide(_lane_bcast(gc_all, col), lane_lo)
                        - jnp.broadcast_to(gc_rows[b][DN_HEADS + h:DN_HEADS + h + 1, :], (CHUNK, rows)))
                decay = jnp.exp(jnp.where(tri_w, dlog, -jnp.inf))
                m_ref[chain] = -jnp.where(strict_w, jnp.concatenate(kk_w[h], axis=1) * decay, 0.0)
                attn_ref[chain] = (jnp.concatenate(qk_w[h], axis=1) * decay).astype(BF16)
            yield


def _mixer_stage_b(x_ref, dng_ref, wout_ref, s_ref, o_ref, inp):
    ya_ref, zg_ref, m_ref, attn_ref, rhs_ref, qd_ref, kd_ref, egl_ref = inp
    n_seq = x_ref.shape[0]
    rows = MIX_ROWS
    all_rows = n_seq * rows
    chains = range(n_seq * DN_HEADS)

    ri = lax.broadcasted_iota(jnp.int32, (rows, rows), 0)
    ci = lax.broadcasted_iota(jnp.int32, (rows, rows), 1)
    bd_mask = (ri // CHUNK) == (ci // CHUNK)
    wr = lax.broadcasted_iota(jnp.int32, (CHUNK, rows), 0)
    wc = lax.broadcasted_iota(jnp.int32, (CHUNK, rows), 1) % CHUNK
    eye_w = jnp.where(wc == wr, 1.0, 0.0)

    m_w = {c: m_ref[c] for c in chains}
    inv_w = {c: eye_w + m_w[c] for c in chains}
    m_bd = {c: _block_diag(m_w[c].astype(BF16), bd_mask) for c in chains}
    for _ in range(5):
        for c in chains:
            m_w[c] = jnp.dot(m_w[c].astype(BF16), m_bd[c], **ACC)
        yield
        for c in chains:
            m_bd[c] = _block_diag(m_w[c].astype(BF16), bd_mask)
        for c in chains:
            inv_w[c] = inv_w[c] + jnp.dot(inv_w[c].astype(BF16), m_bd[c], **ACC)
        yield
    uw = {c: jnp.dot(_block_diag(inv_w[c].astype(BF16), bd_mask), rhs_ref[c], **ACC) for c in chains}
    yield

    state = {c: s_ref[c] for c in chains}
    v_new = {c: [] for c in chains}
    o_inter = {c: [] for c in chains}
    pairs = [(c, c + 1) for c in range(0, n_seq * DN_HEADS, 2)]
    zeros_s = jnp.zeros((DN_HEAD_DIM, DN_HEAD_DIM), BF16)
    zeros_v = jnp.zeros((CHUNK, DN_HEAD_DIM), BF16)

    def block_diag2(a, b, zeros):
        return jnp.concatenate([jnp.concatenate([a, zeros], axis=1), jnp.concatenate([zeros, b], axis=1)], axis=0)

    for ch in range(MIX_CHUNKS):
        cs = slice(ch * CHUNK, (ch + 1) * CHUNK)
        r = {}
        for pair in pairs:
            wq = jnp.concatenate(
                [jnp.concatenate([uw[c][cs, DN_HEAD_DIM:].astype(BF16), qd_ref[c, cs, :]], axis=0) for c in pair],
                axis=1)
            s_bd = block_diag2(state[pair[0]].astype(BF16), state[pair[1]].astype(BF16), zeros_s)
            r[pair] = jnp.dot(wq, s_bd, **ACC)
        yield
        for pair in pairs:
            vn = {}
            for i, c in enumerate(pair):
                r_c = r[pair][:, i * DN_HEAD_DIM:(i + 1) * DN_HEAD_DIM]
                vn[c] = uw[c][cs, :DN_HEAD_DIM] - r_c[:CHUNK]
                v_new[c].append(vn[c])
                o_inter[c].append(r_c[CHUNK:])
            kd = jnp.concatenate([kd_ref[c, cs, :] for c in pair], axis=0)
            vn_bd = block_diag2(vn[pair[0]].astype(BF16), vn[pair[1]].astype(BF16), zeros_v)
            upd = lax.dot_general(kd, vn_bd, _TN, **ACC)
            for i, c in enumerate(pair):
                decay_s = jnp.concatenate([egl_ref[c, cs, :]] * (DN_HEAD_DIM // CHUNK), axis=0)
                state[c] = state[c] * decay_s + upd[:, i * DN_HEAD_DIM:(i + 1) * DN_HEAD_DIM]
        yield
    y_b = []
    for b in range(n_seq):
        heads = []
        for h in range(DN_HEADS):
            c = b * DN_HEADS + h
            s_ref[c] = state[c]
            o = (jnp.concatenate(o_inter[c], axis=0)
                 + jnp.dot(_block_diag(attn_ref[c], bd_mask),
                           jnp.concatenate(v_new[c], axis=0).astype(BF16), **ACC))
            heads.append(_rmsnorm(o, dng_ref[...])
                         * zg_ref[b * rows:(b + 1) * rows, h * DN_HEAD_DIM:(h + 1) * DN_HEAD_DIM])
        y_b.append(jnp.concatenate(heads, axis=1).astype(BF16))
        yield
    y = jnp.concatenate([ya_ref[...], jnp.concatenate(y_b, axis=0)], axis=1)
    for c0 in range(0, D_MODEL, PROJ_COLS):
        cols = slice(c0, c0 + PROJ_COLS)
        x = x_ref[:, :, cols].reshape(all_rows, PROJ_COLS)
        o_ref[:, :, cols] = (x + jnp.dot(y, wout_ref[:, cols], **ACC)).reshape(n_seq, rows, PROJ_COLS)
        yield


def _round_robin(*stages):
    stages = list(stages)
    while stages:
        for stage in list(stages):
            try:
                next(stage)
            except StopIteration:
                stages.remove(stage)


def _mixer_body(xa_ref, xb_ref, mg_ref, win_ref, wba_ref, lng_ref, lnb_ref, wcat_ref, bias_ref,
                cw_ref, alog_ref, dtb_ref, dng_ref, wout_ref, o_ref, s_ref, cin_ref, qkv_ref, *handoff):
    t = pl.program_id(0)
    sets = (handoff[:N_HANDOFF], handoff[N_HANDOFF:])

    @pl.when(t == 0)
    def _():
        s_ref[...] = jnp.zeros(s_ref.shape, F32)
        cin_ref[...] = jnp.zeros(cin_ref.shape, F32)
        for ref in sets[1]:
            ref[...] = jnp.zeros(ref.shape, ref.dtype)

    def step(write_set, read_set):
        _round_robin(
            _mixer_stage_b(xb_ref, dng_ref, wout_ref, s_ref, o_ref, read_set),
            _mixer_stage_a(xa_ref, mg_ref, win_ref, wba_ref, lng_ref, lnb_ref, wcat_ref, bias_ref,
                           cw_ref, alog_ref, dtb_ref, cin_ref, qkv_ref, write_set))

    @pl.when(t % 2 == 0)
    def _():
        step(sets[0], sets[1])

    @pl.when(t % 2 == 1)
    def _():
        step(sets[1], sets[0])


def _mixer_call(x, mix_norm, w_in, gm_ln_g, gm_ln_b, gm_w_s, gm_b_s, dn_conv_w, dn_a_log,
                dn_dt_bias, dn_norm, w_out):
    batch, seq, _ = x.shape
    n_tiles = seq // MIX_ROWS
    n_main = 2 * GM_WIDTH + 4 * DN_WIDTH
    w_main = w_in[:, :n_main].astype(BF16)
    w_ba = jnp.pad(w_in[:, n_main:], ((0, 0), (0, LANES - 2 * DN_HEADS))).astype(BF16)
    w_cat = gm_w_s.reshape(GM_GROUPS // 2, 2, GM_BLOCK, GM_BLOCK).transpose(0, 2, 1, 3)
    w_cat = w_cat.reshape(GM_GROUPS // 2, GM_BLOCK, 2 * GM_BLOCK)
    bias = jnp.repeat(gm_b_s.T, GM_GROUP_DIM, axis=1)
    pad_a = lambda a: jnp.pad(a.reshape(1, DN_HEADS), ((0, 0), (DN_HEADS, LANES - 2 * DN_HEADS)))
    const2 = lambda t: (0, 0)
    tile_a = lambda t: (0, jnp.minimum(t, n_tiles - 1), 0)
    tile_b = lambda t: (0, jnp.maximum(t - 1, 0), 0)
    in_specs = [
        pl.BlockSpec((batch, MIX_ROWS, D_MODEL), tile_a),
        pl.BlockSpec((batch, MIX_ROWS, D_MODEL), tile_b),
        pl.BlockSpec((1, D_MODEL), const2),
        pl.BlockSpec((D_MODEL, n_main), const2),
        pl.BlockSpec((D_MODEL, LANES), const2),
        pl.BlockSpec((1, GM_WIDTH), const2),
        pl.BlockSpec((1, GM_WIDTH), const2),
        pl.BlockSpec((GM_GROUPS // 2, GM_BLOCK, 2 * GM_BLOCK), lambda t: (0, 0, 0)),
        pl.BlockSpec((GM_BLOCK, GM_WIDTH), const2),
        pl.BlockSpec((DN_CONV, 3 * DN_WIDTH), const2),
        pl.BlockSpec((1, LANES), const2),
        pl.BlockSpec((1, LANES), const2),
        pl.BlockSpec((1, DN_HEAD_DIM), const2),
        pl.BlockSpec((D_MODEL, D_MODEL), const2),
    ]
    return pl.pallas_call(
        _mixer_body,
        grid=(n_tiles + 1,),
        in_specs=in_specs,
        out_specs=pl.BlockSpec((batch, MIX_ROWS, D_MODEL), tile_b),
        out_shape=jax.ShapeDtypeStruct((batch, seq, D_MODEL), F32),
        scratch_shapes=[
            pltpu.VMEM((batch * DN_HEADS, DN_HEAD_DIM, DN_HEAD_DIM), F32),
            pltpu.VMEM((batch, 3 * DN_HEADS, CONV_TAIL + MIX_ROWS, LANES), F32),
            pltpu.VMEM((batch, 3 * DN_HEADS, MIX_ROWS, LANES), F32),
        ] + _handoff_shapes(batch) + _handoff_shapes(batch),
        compiler_params=pltpu.CompilerParams(
            dimension_semantics=("arbitrary",), vmem_limit_bytes=VMEM_LIMIT_BYTES),
        name="mixer",
    )(x, x, mix_norm.reshape(1, D_MODEL), w_main, w_ba, gm_ln_g.reshape(1, GM_WIDTH),
      gm_ln_b.reshape(1, GM_WIDTH), w_cat, bias, dn_conv_w, pad_a(dn_a_log), pad_a(dn_dt_bias),
      dn_norm.reshape(1, DN_HEAD_DIM), w_out.astype(BF16))


def kernel(x, ffn1_norm, ffn1_w_gate, ffn1_w_up, ffn1_w_down, mix_norm, w_in, gm_ln_g, gm_ln_b,
           gm_w_s, gm_b_s, dn_conv_w, dn_a_log, dn_dt_bias, dn_norm, w_out, ffn2_norm,
           ffn2_w_gate, ffn2_w_up, ffn2_w_down, final_norm):
    batch, seq, d = x.shape
    depth = ffn1_norm.shape[0]
    for l in range(depth):
        x = _ffn_call(x.reshape(batch * seq, d), ffn1_norm[l], ffn1_w_gate[l], ffn1_w_up[l],
                      ffn1_w_down[l]).reshape(batch, seq, d)
        x = _mixer_call(x, mix_norm[l], w_in[l], gm_ln_g[l], gm_ln_b[l], gm_w_s[l], gm_b_s[l],
                        dn_conv_w[l], dn_a_log[l], dn_dt_bias[l], dn_norm[l], w_out[l])
        last = l == depth - 1
        x = _ffn_call(x.reshape(batch * seq, d), ffn2_norm[l], ffn2_w_gate[l], ffn2_w_up[l],
                      ffn2_w_down[l], final_norm if last else None).reshape(batch, seq, d)
    return x
```

```python
import functools

import jax
import jax.numpy as jnp
from jax import lax
from jax.experimental import pallas as pl
from jax.experimental.pallas import tpu as pltpu

F32 = jnp.float32
BF16 = jnp.bfloat16
ACC = dict(preferred_element_type=jnp.float32)

D_MODEL = 1024
D_FF = 2816
CHUNK = 64
GM_GROUPS = 8
GM_GROUP_DIM = 64
GM_WIDTH = GM_GROUPS * GM_GROUP_DIM
GM_BLOCK = 128
DN_HEADS = 4
DN_HEAD_DIM = 128
DN_WIDTH = DN_HEADS * DN_HEAD_DIM
DN_CONV = 4
EPS = 1e-6

LANES = 128
SUBLANES = 8
VMEM_LIMIT_BYTES = 60000 * 1024

FFN_ROWS = 512
FFN_COLS = 256
MIX_ROWS = 256
MIX_CHUNKS = MIX_ROWS // CHUNK
CONV_TAIL = SUBLANES
PROJ_COLS = 256

_NT = (((1,), (1,)), ((), ()))
_TN = (((0,), (0,)), ((), ()))


def _rmsnorm(x, g):
    return (x * lax.rsqrt(jnp.mean(x * x, axis=-1, keepdims=True) + EPS)) * g


def _softplus(x):
    return jnp.maximum(x, 0.0) + jnp.log1p(jnp.exp(-jnp.abs(x)))


def _ffn_body(x_ref, g_ref, wg_hbm, wu_hbm, wd_hbm, *rest, final_norm):
    if final_norm:
        fg_ref, o_ref, wg_ref, wu_ref, wd_ref, stage_gu, stage_d, sem = rest
    else:
        o_ref, wg_ref, wu_ref, wd_ref, stage_gu, stage_d, sem = rest
    n_chunks = D_FF // FFN_COLS

    def chunk_copies(j, slot):
        cols = pl.ds(j * FFN_COLS, FFN_COLS)
        return (pltpu.make_async_copy(wg_hbm.at[:, cols], stage_gu.at[slot, 0], sem.at[slot, 0]),
                pltpu.make_async_copy(wu_hbm.at[:, cols], stage_gu.at[slot, 1], sem.at[slot, 1]),
                pltpu.make_async_copy(wd_hbm.at[cols, :], stage_d.at[slot], sem.at[slot, 2]))

    def run(load_weights):
        x = x_ref[...]
        hb = _rmsnorm(x, g_ref[...]).astype(BF16)
        acc = jnp.zeros(x.shape, F32)
        if load_weights:
            for cp in chunk_copies(0, 0):
                cp.start()
        for j in range(n_chunks):
            cols = slice(j * FFN_COLS, (j + 1) * FFN_COLS)
            if load_weights:
                slot = j % 2
                if j + 1 < n_chunks:
                    for cp in chunk_copies(j + 1, 1 - slot):
                        cp.start()
                for cp in chunk_copies(j, slot):
                    cp.wait()
                wg_ref[:, cols] = stage_gu[slot, 0].astype(BF16)
                wu_ref[:, cols] = stage_gu[slot, 1].astype(BF16)
                wd_ref[cols, :] = stage_d[slot].astype(BF16)
            gate = jnp.dot(hb, wg_ref[:, cols], **ACC)
            up = jnp.dot(hb, wu_ref[:, cols], **ACC)
            act = (jax.nn.silu(gate) * up).astype(BF16)
            acc = acc + jnp.dot(act, wd_ref[cols, :], **ACC)
        y = x + 0.5 * acc
        if final_norm:
            y = _rmsnorm(y, fg_ref[...])
        o_ref[...] = y

    first = pl.program_id(0) == 0
    pl.when(first)(functools.partial(run, True))
    pl.when(jnp.logical_not(first))(functools.partial(run, False))


def _ffn_call(x2d, gain, w_gate, w_up, w_down, final_gain=None):
    rows = x2d.shape[0]
    final_norm = final_gain is not None
    const = lambda i: (0, 0)
    hbm = pl.BlockSpec(memory_space=pl.ANY)
    in_specs = [
        pl.BlockSpec((FFN_ROWS, D_MODEL), lambda i: (i, 0)),
        pl.BlockSpec((1, D_MODEL), const),
        hbm, hbm, hbm,
    ]
    args = [x2d, gain.reshape(1, D_MODEL), w_gate, w_up, w_down]
    if final_norm:
        in_specs.append(pl.BlockSpec((1, D_MODEL), const))
        args.append(final_gain.reshape(1, D_MODEL))
    return pl.pallas_call(
        functools.partial(_ffn_body, final_norm=final_norm),
        grid=(rows // FFN_ROWS,),
        in_specs=in_specs,
        out_specs=pl.BlockSpec((FFN_ROWS, D_MODEL), lambda i: (i, 0)),
        out_shape=jax.ShapeDtypeStruct((rows, D_MODEL), F32),
        scratch_shapes=[
            pltpu.VMEM((D_MODEL, D_FF), BF16),
            pltpu.VMEM((D_MODEL, D_FF), BF16),
            pltpu.VMEM((D_FF, D_MODEL), BF16),
            pltpu.VMEM((2, 2, D_MODEL, FFN_COLS), F32),
            pltpu.VMEM((2, FFN_COLS, D_MODEL), F32),
            pltpu.SemaphoreType.DMA((2, 3)),
        ],
        compiler_params=pltpu.CompilerParams(
            dimension_semantics=("arbitrary",), vmem_limit_bytes=VMEM_LIMIT_BYTES),
        name="ffn_final" if final_norm else "ffn",
    )(*args)


def _split_hi_lo(x):
    hi = x.astype(BF16)
    lo = (x - hi.astype(F32)).astype(BF16)
    return hi, lo


def _lane_bcast(x, col):
    return jnp.broadcast_to(x[:, col:col + 1], (x.shape[0], LANES))


def _col_to_wide(col, lane_lo):
    left = jnp.where(lane_lo, col[0:CHUNK], col[CHUNK:2 * CHUNK])
    right = jnp.where(lane_lo, col[2 * CHUNK:3 * CHUNK], col[3 * CHUNK:])
    return jnp.concatenate([left, right], axis=1)


def _block_diag(wide_b, bd_mask):
    return jnp.where(bd_mask, jnp.concatenate([wide_b] * MIX_CHUNKS, axis=0), jnp.zeros((), wide_b.dtype))


def _handoff_shapes(n_seq):
    n_chain = n_seq * DN_HEADS
    return [
        pltpu.VMEM((n_seq * MIX_ROWS, GM_WIDTH), BF16),
        pltpu.VMEM((n_seq * MIX_ROWS, DN_WIDTH), F32),
        pltpu.VMEM((n_chain, CHUNK, MIX_ROWS), F32),
        pltpu.VMEM((n_chain, CHUNK, MIX_ROWS), BF16),
        pltpu.VMEM((n_chain, MIX_ROWS, 2 * DN_HEAD_DIM), BF16),
        pltpu.VMEM((n_chain, MIX_ROWS, DN_HEAD_DIM), BF16),
        pltpu.VMEM((n_chain, MIX_ROWS, DN_HEAD_DIM), BF16),
        pltpu.VMEM((n_chain, MIX_ROWS, DN_HEAD_DIM), F32),
    ]


N_HANDOFF = 8


def _mixer_stage_a(x_ref, mg_ref, win_ref, wba_ref, lng_ref, lnb_ref, wcat_ref, bias_ref, cw_ref,
                   alog_ref, dtb_ref, cin_ref, qkv_ref, out):
    ya_ref, zg_ref, m_ref, attn_ref, rhs_ref, qd_ref, kd_ref, egl_ref = out
    n_seq = x_ref.shape[0]
    rows = MIX_ROWS
    all_rows = n_seq * rows
    n_blocks = all_rows // GM_BLOCK

    x = x_ref[...].reshape(all_rows, D_MODEL)
    hb = _rmsnorm(x, mg_ref[...]).astype(BF16)
    yield

    def project(lo, hi):
        parts = []
        for c0 in range(lo, hi, PROJ_COLS):
            parts.append(jnp.dot(hb, win_ref[:, c0:min(c0 + PROJ_COLS, hi)], **ACC))
            yield
        return jnp.concatenate(parts, axis=1)

    for c0 in range(0, 3 * DN_WIDTH, PROJ_COLS):
        blk = jnp.dot(hb, win_ref[:, 2 * GM_WIDTH + c0:2 * GM_WIDTH + c0 + PROJ_COLS], **ACC)
        for b in range(n_seq):
            for j in range(PROJ_COLS // LANES):
                slab = cin_ref.at[b, c0 // LANES + j]
                slab[0:CONV_TAIL, :] = slab[rows:rows + CONV_TAIL, :]
                slab[CONV_TAIL:, :] = blk[b * rows:(b + 1) * rows, j * LANES:(j + 1) * LANES]
        yield
    pba = jnp.dot(hb, wba_ref[...], **ACC)
    yield
    half = rows // 2
    for b in range(n_seq):
        for s in range(3 * DN_WIDTH // LANES):
            src, dst = cin_ref.at[b, s], qkv_ref.at[b, s]
            w = cw_ref[:, s * LANES:(s + 1) * LANES]
            tap = lambda off: src[pl.ds(CONV_TAIL + off, half, stride=2), :]
            e0, e2, o1, om1, om3 = tap(0), tap(-2), tap(1), tap(-1), tap(-3)
            even = w[3:4] * e0 + w[2:3] * om1 + w[1:2] * e2 + w[0:1] * om3
            odd = w[3:4] * o1 + w[2:3] * e0 + w[1:2] * om1 + w[0:1] * e2
            dst[pl.ds(0, half, stride=2), :] = jax.nn.silu(even)
            dst[pl.ds(1, half, stride=2), :] = jax.nn.silu(odd)
    beta_all = jax.nn.sigmoid(pba)
    g_all = -jnp.exp(alog_ref[...]) * _softplus(pba + dtb_ref[...])

    u_a = jax.nn.gelu((yield from project(0, GM_WIDTH)))
    v_a = jax.nn.gelu((yield from project(GM_WIDTH, 2 * GM_WIDTH)))
    zg_ref[...] = jax.nn.silu((yield from project(2 * GM_WIDTH + 3 * DN_WIDTH, 2 * GM_WIDTH + 4 * DN_WIDTH)))

    ri = lax.broadcasted_iota(jnp.int32, (rows, rows), 0)
    ci = lax.broadcasted_iota(jnp.int32, (rows, rows), 1)
    bd_mask = (ri // CHUNK) == (ci // CHUNK)
    tri_b = jnp.where(bd_mask & (ci <= ri), 1.0, 0.0).astype(BF16)
    blk_b = jnp.where(bd_mask, 1.0, 0.0).astype(BF16)
    g_hi, g_lo = _split_hi_lo(jnp.concatenate([g_all[b * rows:(b + 1) * rows] for b in range(n_seq)], axis=1))
    gc_all = jnp.dot(tri_b, g_hi, **ACC) + jnp.dot(tri_b, g_lo, **ACC)
    gl_all = jnp.dot(blk_b, g_hi, **ACC) + jnp.dot(blk_b, g_lo, **ACC)
    yield
    e_gc_all = jnp.exp(gc_all)
    e_glgc_all = jnp.exp(gl_all - gc_all)
    e_gl_all = jnp.exp(gl_all)
    gc_rows = [gc_all[:, b * LANES:(b + 1) * LANES].T for b in range(n_seq)]

    mu = jnp.mean(v_a, axis=-1, keepdims=True)
    vc = v_a - mu
    var = jnp.mean(vc * vc, axis=-1, keepdims=True)
    v_n = (vc * lax.rsqrt(var + EPS)) * lng_ref[...] + lnb_ref[...]

    wi = lax.broadcasted_iota(jnp.int32, (GM_BLOCK, 2 * GM_BLOCK), 0) // CHUNK
    wj = (lax.broadcasted_iota(jnp.int32, (GM_BLOCK, 2 * GM_BLOCK), 1) % GM_BLOCK) // CHUNK
    w_mask = wj <= wi
    lane_lo128 = lax.broadcasted_iota(jnp.int32, (GM_BLOCK, LANES), 1) < GM_GROUP_DIM
    mixed_cols = []
    for p in range(GM_GROUPS // 2):
        v_p = v_n[:, p * LANES:(p + 1) * LANES]
        rhs = []
        for blk_i in range(n_blocks):
            blk = v_p[blk_i * GM_BLOCK:(blk_i + 1) * GM_BLOCK]
            rhs.append(jnp.concatenate(
                [jnp.where(lane_lo128, blk, 0.0), jnp.where(lane_lo128, 0.0, blk)], axis=0))
        rhs = jnp.concatenate(rhs, axis=1).astype(BF16)
        w_p = jnp.where(w_mask, wcat_ref[p], 0.0).astype(BF16)
        res = jnp.dot(w_p, rhs, **ACC)
        mixed_cols.append(jnp.concatenate(
            [res[:, i * LANES:(i + 1) * LANES] for i in range(n_blocks)], axis=0))
        if p % 2 == 1:
            yield
    bias = jnp.concatenate([bias_ref[...]] * n_blocks, axis=0)
    ya_ref[...] = (u_a * (jnp.concatenate(mixed_cols, axis=1) + bias)).astype(BF16)

    wr = lax.broadcasted_iota(jnp.int32, (CHUNK, rows), 0)
    wc = lax.broadcasted_iota(jnp.int32, (CHUNK, rows), 1) % CHUNK
    tri_w = wc <= wr
    strict_w = wc < wr
    lane_lo = lax.broadcasted_iota(jnp.int32, (CHUNK, LANES), 1) < CHUNK
    scale = DN_HEAD_DIM ** -0.5

    zeros_k = jnp.zeros((2 * CHUNK, DN_HEAD_DIM), BF16)
    for b in range(n_seq):
        rs = slice(b * rows, (b + 1) * rows)
        for h0 in range(0, DN_HEADS, 2):
            q, k, k_beta = {}, {}, {}
            for h in (h0, h0 + 1):
                chain = b * DN_HEADS + h
                col = b * LANES + DN_HEADS + h
                q_h, k_h, v_h = qkv_ref[b, h], qkv_ref[b, DN_HEADS + h], qkv_ref[b, 2 * DN_HEADS + h]
                q[h] = q_h * lax.rsqrt(jnp.sum(q_h * q_h, axis=-1, keepdims=True) + EPS) * scale
                k[h] = k_h * lax.rsqrt(jnp.sum(k_h * k_h, axis=-1, keepdims=True) + EPS)
                beta = _lane_bcast(beta_all[rs], h)
                e_gc = _lane_bcast(e_gc_all, col)
                k_beta[h] = k[h] * beta
                rhs_ref[chain] = jnp.concatenate([v_h * beta, k_beta[h] * e_gc], axis=1).astype(BF16)
                qd_ref[chain] = (q[h] * e_gc).astype(BF16)
                kd_ref[chain] = (k[h] * _lane_bcast(e_glgc_all, col)).astype(BF16)
                egl_ref[chain] = _lane_bcast(e_gl_all, col)
            kk_w = {h: [] for h in (h0, h0 + 1)}
            qk_w = {h: [] for h in (h0, h0 + 1)}
            for half_i in range(rows // LANES):
                hr = slice(half_i * LANES, (half_i + 1) * LANES)
                lhs = jnp.concatenate(
                    [jnp.concatenate([k_beta[h][hr], q[h][hr]], axis=0) for h in (h0, h0 + 1)], axis=1)
                kb0, kb1 = k[h0][hr].astype(BF16), k[h0 + 1][hr].astype(BF16)
                rhs_k = jnp.concatenate([jnp.concatenate([kb0, zeros_k], axis=1),
                                         jnp.concatenate([zeros_k, kb1], axis=1)], axis=0)
                out = lax.dot_general(lhs.astype(BF16), rhs_k, _NT, **ACC)
                for i, h in enumerate((h0, h0 + 1)):
                    kk = out[0:LANES, i * LANES:(i + 1) * LANES]
                    qk = out[LANES:, i * LANES:(i + 1) * LANES]
                    kk_w[h].append(jnp.where(lane_lo, kk[0:CHUNK], kk[CHUNK:]))
                    qk_w[h].append(jnp.where(lane_lo, qk[0:CHUNK], qk[CHUNK:]))
            for h in (h0, h0 + 1):
                chain = b * DN_HEADS + h
                col = b * LANES + DN_HEADS + h
                dlog = (_col_to_wide(_lane_bcast(gc_all, col), lane_lo)
                        - jnp.broadcast_to(gc_rows[b][DN_HEADS + h:DN_HEADS + h + 1, :], (CHUNK, rows)))
                decay = jnp.exp(jnp.where(tri_w, dlog, -jnp.inf))
                m_ref[chain] = -jnp.where(strict_w, jnp.concatenate(kk_w[h], axis=1) * decay, 0.0)
                attn_ref[chain] = (jnp.concatenate(qk_w[h], axis=1) * decay).astype(BF16)
            yield


def _mixer_stage_b(x_ref, dng_ref, wout_ref, s_ref, o_ref, inp):
    ya_ref, zg_ref, m_ref, attn_ref, rhs_ref, qd_ref, kd_ref, egl_ref = inp
    n_seq = x_ref.shape[0]
    rows = MIX_ROWS
    all_rows = n_seq * rows
    chains = range(n_seq * DN_HEADS)

    ri = lax.broadcasted_iota(jnp.int32, (rows, rows), 0)
    ci = lax.broadcasted_iota(jnp.int32, (rows, rows), 1)
    bd_mask = (ri // CHUNK) == (ci // CHUNK)
    wr = lax.broadcasted_iota(jnp.int32, (CHUNK, rows), 0)
    wc = lax.broadcasted_iota(jnp.int32, (CHUNK, rows), 1) % CHUNK
    eye_w = jnp.where(wc == wr, 1.0, 0.0)

    m_w = {c: m_ref[c] for c in chains}
    inv_w = {c: eye_w + m_w[c] for c in chains}
    m_bd = {c: _block_diag(m_w[c].astype(BF16), bd_mask) for c in chains}
    for _ in range(5):
        for c in chains:
            m_w[c] = jnp.dot(m_w[c].astype(BF16), m_bd[c], **ACC)
        yield
        for c in chains:
            m_bd[c] = _block_diag(m_w[c].astype(BF16), bd_mask)
        for c in chains:
            inv_w[c] = inv_w[c] + jnp.dot(inv_w[c].astype(BF16), m_bd[c], **ACC)
        yield
    uw = {c: jnp.dot(_block_diag(inv_w[c].astype(BF16), bd_mask), rhs_ref[c], **ACC) for c in chains}
    yield

    state = {c: s_ref[c] for c in chains}
    v_new = {c: [] for c in chains}
    o_inter = {c: [] for c in chains}
    pairs = [(c, c + 1) for c in range(0, n_seq * DN_HEADS, 2)]
    zeros_s = jnp.zeros((DN_HEAD_DIM, DN_HEAD_DIM), BF16)
    zeros_v = jnp.zeros((CHUNK, DN_HEAD_DIM), BF16)

    def block_diag2(a, b, zeros):
        return jnp.concatenate([jnp.concatenate([a, zeros], axis=1), jnp.concatenate([zeros, b], axis=1)], axis=0)

    for ch in range(MIX_CHUNKS):
        cs = slice(ch * CHUNK, (ch + 1) * CHUNK)
        r = {}
        for pair in pairs:
            wq = jnp.concatenate(
                [jnp.concatenate([uw[c][cs, DN_HEAD_DIM:].astype(BF16), qd_ref[c, cs, :]], axis=0) for c in pair],
                axis=1)
            s_bd = block_diag2(state[pair[0]].astype(BF16), state[pair[1]].astype(BF16), zeros_s)
            r[pair] = jnp.dot(wq, s_bd, **ACC)
        yield
        for pair in pairs:
            vn = {}
            for i, c in enumerate(pair):
                r_c = r[pair][:, i * DN_HEAD_DIM:(i + 1) * DN_HEAD_DIM]
                vn[c] = uw[c][cs, :DN_HEAD_DIM] - r_c[:CHUNK]
                v_new[c].append(vn[c])
                o_inter[c].append(r_c[CHUNK:])
            kd = jnp.concatenate([kd_ref[c, cs, :] for c in pair], axis=0)
            vn_bd = block_diag2(vn[pair[0]].astype(BF16), vn[pair[1]].astype(BF16), zeros_v)
            upd = lax.dot_general(kd, vn_bd, _TN, **ACC)
            for i, c in enumerate(pair):
                decay_s = jnp.concatenate([egl_ref[c, cs, :]] * (DN_HEAD_DIM // CHUNK), axis=0)
                state[c] = state[c] * decay_s + upd[:, i * DN_HEAD_DIM:(i + 1) * DN_HEAD_DIM]
        yield
    y_b = []
    for b in range(n_seq):
        heads = []
        for h in range(DN_HEADS):
            c = b * DN_HEADS + h
            s_ref[c] = state[c]
            o = (jnp.concatenate(o_inter[c], axis=0)
                 + jnp.dot(_block_diag(attn_ref[c], bd_mask),
                           jnp.concatenate(v_new[c], axis=0).astype(BF16), **ACC))
            heads.append(_rmsnorm(o, dng_ref[...])
                         * zg_ref[b * rows:(b + 1) * rows, h * DN_HEAD_DIM:(h + 1) * DN_HEAD_DIM])
        y_b.append(jnp.concatenate(heads, axis=1).astype(BF16))
        yield
    y = jnp.concatenate([ya_ref[...], jnp.concatenate(y_b, axis=0)], axis=1)
    for c0 in range(0, D_MODEL, PROJ_COLS):
        cols = slice(c0, c0 + PROJ_COLS)
        x = x_ref[:, :, cols].reshape(all_rows, PROJ_COLS)
        o_ref[:, :, cols] = (x + jnp.dot(y, wout_ref[:, cols], **ACC)).reshape(n_seq, rows, PROJ_COLS)
        yield


def _round_robin(*stages):
    stages = list(stages)
    while stages:
        for stage in list(stages):
            try:
                next(stage)
            except StopIteration:
                stages.remove(stage)


def _mixer_body(xa_ref, xb_ref, mg_ref, win_ref, wba_ref, lng_ref, lnb_ref, wcat_ref, bias_ref,
                cw_ref, alog_ref, dtb_ref, dng_ref, wout_ref, o_ref, s_ref, cin_ref, qkv_ref, *handoff):
    t = pl.program_id(0)
    sets = (handoff[:N_HANDOFF], handoff[N_HANDOFF:])

    @pl.when(t == 0)
    def _():
        s_ref[...] = jnp.zeros(s_ref.shape, F32)
        cin_ref[...] = jnp.zeros(cin_ref.shape, F32)
        for ref in sets[1]:
            ref[...] = jnp.zeros(ref.shape, ref.dtype)

    def step(write_set, read_set):
        _round_robin(
            _mixer_stage_b(xb_ref, dng_ref, wout_ref, s_ref, o_ref, read_set),
            _mixer_stage_a(xa_ref, mg_ref, win_ref, wba_ref, lng_ref, lnb_ref, wcat_ref, bias_ref,
                           cw_ref, alog_ref, dtb_ref, cin_ref, qkv_ref, write_set))

    @pl.when(t % 2 == 0)
    def _():
        step(sets[0], sets[1])

    @pl.when(t % 2 == 1)
    def _():
        step(sets[1], sets[0])


def _mixer_call(x, mix_norm, w_in, gm_ln_g, gm_ln_b, gm_w_s, gm_b_s, dn_conv_w, dn_a_log,
                dn_dt_bias, dn_norm, w_out):
    batch, seq, _ = x.shape
    n_tiles = seq // MIX_ROWS
    n_main = 2 * GM_WIDTH + 4 * DN_WIDTH
    w_main = w_in[:, :n_main].astype(BF16)
    w_ba = jnp.pad(w_in[:, n_main:], ((0, 0), (0, LANES - 2 * DN_HEADS))).astype(BF16)
    w_cat = gm_w_s.reshape(GM_GROUPS // 2, 2, GM_BLOCK, GM_BLOCK).transpose(0, 2, 1, 3)
    w_cat = w_cat.reshape(GM_GROUPS // 2, GM_BLOCK, 2 * GM_BLOCK)
    bias = jnp.repeat(gm_b_s.T, GM_GROUP_DIM, axis=1)
    pad_a = lambda a: jnp.pad(a.reshape(1, DN_HEADS), ((0, 0), (DN_HEADS, LANES - 2 * DN_HEADS)))
    const2 = lambda t: (0, 0)
    tile_a = lambda t: (0, jnp.minimum(t, n_tiles - 1), 0)
    tile_b = lambda t: (0, jnp.maximum(t - 1, 0), 0)
    in_specs = [
        pl.BlockSpec((batch, MIX_ROWS, D_MODEL), tile_a),
        pl.BlockSpec((batch, MIX_ROWS, D_MODEL), tile_b),
        pl.BlockSpec((1, D_MODEL), const2),
        pl.BlockSpec((D_MODEL, n_main), const2),
        pl.BlockSpec((D_MODEL, LANES), const2),
        pl.BlockSpec((1, GM_WIDTH), const2),
        pl.BlockSpec((1, GM_WIDTH), const2),
        pl.BlockSpec((GM_GROUPS // 2, GM_BLOCK, 2 * GM_BLOCK), lambda t: (0, 0, 0)),
        pl.BlockSpec((GM_BLOCK, GM_WIDTH), const2),
        pl.BlockSpec((DN_CONV, 3 * DN_WIDTH), const2),
        pl.BlockSpec((1, LANES), const2),
        pl.BlockSpec((1, LANES), const2),
        pl.BlockSpec((1, DN_HEAD_DIM), const2),
        pl.BlockSpec((D_MODEL, D_MODEL), const2),
    ]
    return pl.pallas_call(
        _mixer_body,
        grid=(n_tiles + 1,),
        in_specs=in_specs,
        out_specs=pl.BlockSpec((batch, MIX_ROWS, D_MODEL), tile_b),
        out_shape=jax.ShapeDtypeStruct((batch, seq, D_MODEL), F32),
        scratch_shapes=[
            pltpu.VMEM((batch * DN_HEADS, DN_HEAD_DIM, DN_HEAD_DIM), F32),
            pltpu.VMEM((batch, 3 * DN_HEADS, CONV_TAIL + MIX_ROWS, LANES), F32),
            pltpu.VMEM((batch, 3 * DN_HEADS, MIX_ROWS, LANES), F32),
        ] + _handoff_shapes(batch) + _handoff_shapes(batch),
        compiler_params=pltpu.CompilerParams(
            dimension_semantics=("arbitrary",), vmem_limit_bytes=VMEM_LIMIT_BYTES),
        name="mixer",
    )(x, x, mix_norm.reshape(1, D_MODEL), w_main, w_ba, gm_ln_g.reshape(1, GM_WIDTH),
      gm_ln_b.reshape(1, GM_WIDTH), w_cat, bias, dn_conv_w, pad_a(dn_a_log), pad_a(dn_dt_bias),
      dn_norm.reshape(1, DN_HEAD_DIM), w_out.astype(BF16))


def kernel(x, ffn1_norm, ffn1_w_gate, ffn1_w_up, ffn1_w_down, mix_norm, w_in, gm_ln_g, gm_ln_b,
           gm_w_s, gm_b_s, dn_conv_w, dn_a_log, dn_dt_bias, dn_norm, w_out, ffn2_norm,
           ffn2_w_gate, ffn2_w_up, ffn2_w_down, final_norm):
    batch, seq, d = x.shape
    depth = ffn1_norm.shape[0]
    for l in range(depth):
        x = _ffn_call(x.reshape(batch * seq, d), ffn1_norm[l], ffn1_w_gate[l], ffn1_w_up[l],
                      ffn1_w_down[l]).reshape(batch, seq, d)
        x = _mixer_call(x, mix_norm[l], w_in[l], gm_ln_g[l], gm_ln_b[l], gm_w_s[l], gm_b_s[l],
                        dn_conv_w[l], dn_a_log[l], dn_dt_bias[l], dn_norm[l], w_out[l])
        last = l == depth - 1
        x = _ffn_call(x.reshape(batch * seq, d), ffn2_norm[l], ffn2_w_gate[l], ffn2_w_up[l],
                      ffn2_w_down[l], final_norm if last else None).reshape(batch, seq, d)
    return x
```

```python
import functools

import jax
import jax.numpy as jnp
from jax import lax
from jax.experimental import pallas as pl
from jax.experimental.pallas import tpu as pltpu

F32 = jnp.float32
BF16 = jnp.bfloat16
ACC = dict(preferred_element_type=jnp.float32)

D_MODEL = 1024
D_FF = 2816
CHUNK = 64
GM_GROUPS = 8
GM_GROUP_DIM = 64
GM_WIDTH = GM_GROUPS * GM_GROUP_DIM
GM_BLOCK = 128
DN_HEADS = 4
DN_HEAD_DIM = 128
DN_WIDTH = DN_HEADS * DN_HEAD_DIM
DN_CONV = 4
EPS = 1e-6

LANES = 128
SUBLANES = 8
VMEM_LIMIT_BYTES = 60000 * 1024

FFN_ROWS = 512
FFN_COLS = 256
MIX_ROWS = 256
MIX_CHUNKS = MIX_ROWS // CHUNK
CONV_TAIL = SUBLANES
PROJ_COLS = 256

_NT = (((1,), (1,)), ((), ()))
_TN = (((0,), (0,)), ((), ()))


def _rmsnorm(x, g):
    return (x * lax.rsqrt(jnp.mean(x * x, axis=-1, keepdims=True) + EPS)) * g


def _softplus(x):
    return jnp.maximum(x, 0.0) + jnp.log1p(jnp.exp(-jnp.abs(x)))


def _ffn_body(x_ref, g_ref, wg_hbm, wu_hbm, wd_hbm, *rest, final_norm):
    if final_norm:
        fg_ref, o_ref, wg_ref, wu_ref, wd_ref, stage_gu, stage_d, sem = rest
    else:
        o_ref, wg_ref, wu_ref, wd_ref, stage_gu, stage_d, sem = rest
    n_chunks = D_FF // FFN_COLS

    def chunk_copies(j, slot):
        cols = pl.ds(j * FFN_COLS, FFN_COLS)
        return (pltpu.make_async_copy(wg_hbm.at[:, cols], stage_gu.at[slot, 0], sem.at[slot, 0]),
                pltpu.make_async_copy(wu_hbm.at[:, cols], stage_gu.at[slot, 1], sem.at[slot, 1]),
                pltpu.make_async_copy(wd_hbm.at[cols, :], stage_d.at[slot], sem.at[slot, 2]))

    def run(load_weights):
        x = x_ref[...]
        hb = _rmsnorm(x, g_ref[...]).astype(BF16)
        acc = jnp.zeros(x.shape, F32)
        if load_weights:
            for cp in chunk_copies(0, 0):
                cp.start()
        for j in range(n_chunks):
            cols = slice(j * FFN_COLS, (j + 1) * FFN_COLS)
            if load_weights:
                slot = j % 2
                if j + 1 < n_chunks:
                    for cp in chunk_copies(j + 1, 1 - slot):
                        cp.start()
                for cp in chunk_copies(j, slot):
                    cp.wait()
                wg_ref[:, cols] = stage_gu[slot, 0].astype(BF16)
                wu_ref[:, cols] = stage_gu[slot, 1].astype(BF16)
                wd_ref[cols, :] = stage_d[slot].astype(BF16)
            gate = jnp.dot(hb, wg_ref[:, cols], **ACC)
            up = jnp.dot(hb, wu_ref[:, cols], **ACC)
            act = (jax.nn.silu(gate) * up).astype(BF16)
            acc = acc + jnp.dot(act, wd_ref[cols, :], **ACC)
        y = x + 0.5 * acc
        if final_norm:
            y = _rmsnorm(y, fg_ref[...])
        o_ref[...] = y

    first = pl.program_id(0) == 0
    pl.when(first)(functools.partial(run, True))
    pl.when(jnp.logical_not(first))(functools.partial(run, False))


def _ffn_call(x2d, gain, w_gate, w_up, w_down, final_gain=None):
    rows = x2d.shape[0]
    final_norm = final_gain is not None
    const = lambda i: (0, 0)
    hbm = pl.BlockSpec(memory_space=pl.ANY)
    in_specs = [
        pl.BlockSpec((FFN_ROWS, D_MODEL), lambda i: (i, 0)),
        pl.BlockSpec((1, D_MODEL), const),
        hbm, hbm, hbm,
    ]
    args = [x2d, gain.reshape(1, D_MODEL), w_gate, w_up, w_down]
    if final_norm:
        in_specs.append(pl.BlockSpec((1, D_MODEL), const))
        args.append(final_gain.reshape(1, D_MODEL))
    return pl.pallas_call(
        functools.partial(_ffn_body, final_norm=final_norm),
        grid=(rows // FFN_ROWS,),
        in_specs=in_specs,
        out_specs=pl.BlockSpec((FFN_ROWS, D_MODEL), lambda i: (i, 0)),
        out_shape=jax.ShapeDtypeStruct((rows, D_MODEL), F32),
        scratch_shapes=[
            pltpu.VMEM((D_MODEL, D_FF), BF16),
            pltpu.VMEM((D_MODEL, D_FF), BF16),
            pltpu.VMEM((D_FF, D_MODEL), BF16),
            pltpu.VMEM((2, 2, D_MODEL, FFN_COLS), F32),
            pltpu.VMEM((2, FFN_COLS, D_MODEL), F32),
            pltpu.SemaphoreType.DMA((2, 3)),
        ],
        compiler_params=pltpu.CompilerParams(
            dimension_semantics=("arbitrary",), vmem_limit_bytes=VMEM_LIMIT_BYTES),
        name="ffn_final" if final_norm else "ffn",
    )(*args)


def _split_hi_lo(x):
    hi = x.astype(BF16)
    lo = (x - hi.astype(F32)).astype(BF16)
    return hi, lo


def _lane_bcast(x, col):
    return jnp.broadcast_to(x[:, col:col + 1], (x.shape[0], LANES))


def _col_to_wide(col, lane_lo):
    left = jnp.where(lane_lo, col[0:CHUNK], col[CHUNK:2 * CHUNK])
    right = jnp.where(lane_lo, col[2 * CHUNK:3 * CHUNK], col[3 * CHUNK:])
    return jnp.concatenate([left, right], axis=1)


def _block_diag(wide_b, bd_mask):
    return jnp.where(bd_mask, jnp.concatenate([wide_b] * MIX_CHUNKS, axis=0), jnp.zeros((), wide_b.dtype))


def _handoff_shapes(n_seq):
    n_chain = n_seq * DN_HEADS
    return [
        pltpu.VMEM((n_seq * MIX_ROWS, GM_WIDTH), BF16),
        pltpu.VMEM((n_seq * MIX_ROWS, DN_WIDTH), F32),
        pltpu.VMEM((n_chain, CHUNK, MIX_ROWS), F32),
        pltpu.VMEM((n_chain, CHUNK, MIX_ROWS), BF16),
        pltpu.VMEM((n_chain, MIX_ROWS, 2 * DN_HEAD_DIM), BF16),
        pltpu.VMEM((n_chain, MIX_ROWS, DN_HEAD_DIM), BF16),
        pltpu.VMEM((n_chain, MIX_ROWS, DN_HEAD_DIM), BF16),
        pltpu.VMEM((n_chain, MIX_ROWS, DN_HEAD_DIM), F32),
    ]


N_HANDOFF = 8


def _mixer_stage_a(x_ref, mg_ref, win_ref, wba_ref, lng_ref, lnb_ref, wcat_ref, bias_ref, cw_ref,
                   alog_ref, dtb_ref, cin_ref, qkv_ref, out):
    ya_ref, zg_ref, m_ref, attn_ref, rhs_ref, qd_ref, kd_ref, egl_ref = out
    n_seq = x_ref.shape[0]
    rows = MIX_ROWS
    all_rows = n_seq * rows
    n_blocks = all_rows // GM_BLOCK

    x = x_ref[...].reshape(all_rows, D_MODEL)
    hb = _rmsnorm(x, mg_ref[...]).astype(BF16)
    yield

    def project(lo, hi):
        parts = []
        for c0 in range(lo, hi, PROJ_COLS):
            parts.append(lax.dot_general(hb, win_ref[c0:min(c0 + PROJ_COLS, hi), :], _NT, **ACC))
            yield
        return jnp.concatenate(parts, axis=1)

    for c0 in range(0, 3 * DN_WIDTH, PROJ_COLS):
        blk = lax.dot_general(hb, win_ref[2 * GM_WIDTH + c0:2 * GM_WIDTH + c0 + PROJ_COLS, :], _NT, **ACC)
        for b in range(n_seq):
            for j in range(PROJ_COLS // LANES):
                slab = cin_ref.at[b, c0 // LANES + j]
                slab[0:CONV_TAIL, :] = slab[rows:rows + CONV_TAIL, :]
                slab[CONV_TAIL:, :] = blk[b * rows:(b + 1) * rows, j * LANES:(j + 1) * LANES]
        yield
    pba = lax.dot_general(hb, wba_ref[...], _NT, **ACC)
    yield
    half = rows // 2
    for b in range(n_seq):
        for s in range(3 * DN_WIDTH // LANES):
            src, dst = cin_ref.at[b, s], qkv_ref.at[b, s]
            w = cw_ref[:, s * LANES:(s + 1) * LANES]
            tap = lambda off: src[pl.ds(CONV_TAIL + off, half, stride=2), :]
            e0, e2, o1, om1, om3 = tap(0), tap(-2), tap(1), tap(-1), tap(-3)
            even = w[3:4] * e0 + w[2:3] * om1 + w[1:2] * e2 + w[0:1] * om3
            odd = w[3:4] * o1 + w[2:3] * e0 + w[1:2] * om1 + w[0:1] * e2
            dst[pl.ds(0, half, stride=2), :] = jax.nn.silu(even)
            dst[pl.ds(1, half, stride=2), :] = jax.nn.silu(odd)
    beta_all = jax.nn.sigmoid(pba)
    g_all = -jnp.exp(alog_ref[...]) * _softplus(pba + dtb_ref[...])

    u_a = jax.nn.gelu((yield from project(0, GM_WIDTH)))
    v_a = jax.nn.gelu((yield from project(GM_WIDTH, 2 * GM_WIDTH)))
    zg_ref[...] = jax.nn.silu((yield from project(2 * GM_WIDTH + 3 * DN_WIDTH, 2 * GM_WIDTH + 4 * DN_WIDTH)))

    ri = lax.broadcasted_iota(jnp.int32, (rows, rows), 0)
    ci = lax.broadcasted_iota(jnp.int32, (rows, rows), 1)
    bd_mask = (ri // CHUNK) == (ci // CHUNK)
    tri_b = jnp.where(bd_mask & (ci <= ri), 1.0, 0.0).astype(BF16)
    blk_b = jnp.where(bd_mask, 1.0, 0.0).astype(BF16)
    g_hi, g_lo = _split_hi_lo(jnp.concatenate([g_all[b * rows:(b + 1) * rows] for b in range(n_seq)], axis=1))
    gc_all = jnp.dot(tri_b, g_hi, **ACC) + jnp.dot(tri_b, g_lo, **ACC)
    gl_all = jnp.dot(blk_b, g_hi, **ACC) + jnp.dot(blk_b, g_lo, **ACC)
    yield
    e_gc_all = jnp.exp(gc_all)
    e_glgc_all = jnp.exp(gl_all - gc_all)
    e_gl_all = jnp.exp(gl_all)
    gc_rows = [gc_all[:, b * LANES:(b + 1) * LANES].T for b in range(n_seq)]

    mu = jnp.mean(v_a, axis=-1, keepdims=True)
    vc = v_a - mu
    var = jnp.mean(vc * vc, axis=-1, keepdims=True)
    v_n = (vc * lax.rsqrt(var + EPS)) * lng_ref[...] + lnb_ref[...]

    wi = lax.broadcasted_iota(jnp.int32, (GM_BLOCK, 2 * GM_BLOCK), 0) // CHUNK
    wj = (lax.broadcasted_iota(jnp.int32, (GM_BLOCK, 2 * GM_BLOCK), 1) % GM_BLOCK) // CHUNK
    w_mask = wj <= wi
    lane_lo128 = lax.broadcasted_iota(jnp.int32, (GM_BLOCK, LANES), 1) < GM_GROUP_DIM
    mixed_cols = []
    for p in range(GM_GROUPS // 2):
        v_p = v_n[:, p * LANES:(p + 1) * LANES]
        rhs = []
        for blk_i in range(n_blocks):
            blk = v_p[blk_i * GM_BLOCK:(blk_i + 1) * GM_BLOCK]
            rhs.append(jnp.concatenate(
                [jnp.where(lane_lo128, blk, 0.0), jnp.where(lane_lo128, 0.0, blk)], axis=0))
        rhs = jnp.concatenate(rhs, axis=1).astype(BF16)
        w_p = jnp.where(w_mask, wcat_ref[p], 0.0).astype(BF16)
        res = jnp.dot(w_p, rhs, **ACC)
        mixed_cols.append(jnp.concatenate(
            [res[:, i * LANES:(i + 1) * LANES] for i in range(n_blocks)], axis=0))
        if p % 2 == 1:
            yield
    bias = jnp.concatenate([bias_ref[...]] * n_blocks, axis=0)
    ya_ref[...] = (u_a * (jnp.concatenate(mixed_cols, axis=1) + bias)).astype(BF16)

    wr = lax.broadcasted_iota(jnp.int32, (CHUNK, rows), 0)
    wc = lax.broadcasted_iota(jnp.int32, (CHUNK, rows), 1) % CHUNK
    tri_w = wc <= wr
    strict_w = wc < wr
    lane_lo = lax.broadcasted_iota(jnp.int32, (CHUNK, LANES), 1) < CHUNK
    scale = DN_HEAD_DIM ** -0.5

    zeros_k = jnp.zeros((2 * CHUNK, DN_HEAD_DIM), BF16)
    for b in range(n_seq):
        rs = slice(b * rows, (b + 1) * rows)
        for h0 in range(0, DN_HEADS, 2):
            q, k, k_beta = {}, {}, {}
            for h in (h0, h0 + 1):
                chain = b * DN_HEADS + h
                col = b * LANES + DN_HEADS + h
                q_h, k_h, v_h = qkv_ref[b, h], qkv_ref[b, DN_HEADS + h], qkv_ref[b, 2 * DN_HEADS + h]
                q[h] = q_h * lax.rsqrt(jnp.sum(q_h * q_h, axis=-1, keepdims=True) + EPS) * scale
                k[h] = k_h * lax.rsqrt(jnp.sum(k_h * k_h, axis=-1, keepdims=True) + EPS)
                beta = _lane_bcast(beta_all[rs], h)
                e_gc = _lane_bcast(e_gc_all, col)
                k_beta[h] = k[h] * beta
                rhs_ref[chain] = jnp.concatenate([v_h * beta, k_beta[h] * e_gc], axis=1).astype(BF16)
                qd_ref[chain] = (q[h] * e_gc).astype(BF16)
                kd_ref[chain] = (k[h] * _lane_bcast(e_glgc_all, col)).astype(BF16)
                egl_ref[chain] = _lane_bcast(e_gl_all, col)
            kk_w = {h: [] for h in (h0, h0 + 1)}
            qk_w = {h: [] for h in (h0, h0 + 1)}
            for half_i in range(rows // LANES):
                hr = slice(half_i * LANES, (half_i + 1) * LANES)
                lhs = jnp.concatenate(
                    [jnp.concatenate([k_beta[h][hr], q[h][hr]], axis=0) for h in (h0, h0 + 1)], axis=1)
                kb0, kb1 = k[h0][hr].astype(BF16), k[h0 + 1][hr].astype(BF16)
                rhs_k = jnp.concatenate([jnp.concatenate([kb0, zeros_k], axis=1),
                                         jnp.concatenate([zeros_k, kb1], axis=1)], axis=0)
                out = lax.dot_general(lhs.astype(BF16), rhs_k, _NT, **ACC)
                for i, h in enumerate((h0, h0 + 1)):
                    kk = out[0:LANES, i * LANES:(i + 1) * LANES]
                    qk = out[LANES:, i * LANES:(i + 1) * LANES]
                    kk_w[h].append(jnp.where(lane_lo, kk[0:CHUNK], kk[CHUNK:]))
                    qk_w[h].append(jnp.where(lane_lo, qk[0:CHUNK], qk[CHUNK:]))
            for h in (h0, h0 + 1):
                chain = b * DN_HEADS + h
                col = b * LANES + DN_HEADS + h
                dlog = (_col_to_wide(_lane_bcast(gc_all, col), lane_lo)
                        - jnp.broadcast_to(gc_rows[b][DN_HEADS + h:DN_HEADS + h + 1, :], (CHUNK, rows)))
                decay = jnp.exp(jnp.where(tri_w, dlog, -jnp.inf))
                m_ref[chain] = -jnp.where(strict_w, jnp.concatenate(kk_w[h], axis=1) * decay, 0.0)
                attn_ref[chain] = (jnp.concatenate(qk_w[h], axis=1) * decay).astype(BF16)
            yield


def _mixer_stage_b(x_ref, dng_ref, wout_ref, s_ref, o_ref, inp):
    ya_ref, zg_ref, m_ref, attn_ref, rhs_ref, qd_ref, kd_ref, egl_ref = inp
    n_seq = x_ref.shape[0]
    rows = MIX_ROWS
    all_rows = n_seq * rows
    chains = range(n_seq * DN_HEADS)

    ri = lax.broadcasted_iota(jnp.int32, (rows, rows), 0)
    ci = lax.broadcasted_iota(jnp.int32, (rows, rows), 1)
    bd_mask = (ri // CHUNK) == (ci // CHUNK)
    wr = lax.broadcasted_iota(jnp.int32, (CHUNK, rows), 0)
    wc = lax.broadcasted_iota(jnp.int32, (CHUNK, rows), 1) % CHUNK
    eye_w = jnp.where(wc == wr, 1.0, 0.0)

    m_w = {c: m_ref[c] for c in chains}
    inv_w = {c: eye_w + m_w[c] for c in chains}
    m_bd = {c: _block_diag(m_w[c].astype(BF16), bd_mask) for c in chains}
    for _ in range(5):
        for c in chains:
            m_w[c] = jnp.dot(m_w[c].astype(BF16), m_bd[c], **ACC)
        yield
        for c in chains:
            m_bd[c] = _block_diag(m_w[c].astype(BF16), bd_mask)
        for c in chains:
            inv_w[c] = inv_w[c] + jnp.dot(inv_w[c].astype(BF16), m_bd[c], **ACC)
        yield
    uw = {c: jnp.dot(_block_diag(inv_w[c].astype(BF16), bd_mask), rhs_ref[c], **ACC) for c in chains}
    yield

    state = {c: s_ref[c] for c in chains}
    v_new = {c: [] for c in chains}
    o_inter = {c: [] for c in chains}
    pairs = [(c, c + 1) for c in range(0, n_seq * DN_HEADS, 2)]
    zeros_s = jnp.zeros((DN_HEAD_DIM, DN_HEAD_DIM), BF16)
    zeros_v = jnp.zeros((CHUNK, DN_HEAD_DIM), BF16)

    def block_diag2(a, b, zeros):
        return jnp.concatenate([jnp.concatenate([a, zeros], axis=1), jnp.concatenate([zeros, b], axis=1)], axis=0)

    for ch in range(MIX_CHUNKS):
        cs = slice(ch * CHUNK, (ch + 1) * CHUNK)
        r = {}
        for pair in pairs:
            wq = jnp.concatenate(
                [jnp.concatenate([uw[c][cs, DN_HEAD_DIM:].astype(BF16), qd_ref[c, cs, :]], axis=0) for c in pair],
                axis=1)
            s_bd = block_diag2(state[pair[0]].astype(BF16), state[pair[1]].astype(BF16), zeros_s)
            r[pair] = jnp.dot(wq, s_bd, **ACC)
        yield
        for pair in pairs:
            vn = {}
            for i, c in enumerate(pair):
                r_c = r[pair][:, i * DN_HEAD_DIM:(i + 1) * DN_HEAD_DIM]
                vn[c] = uw[c][cs, :DN_HEAD_DIM] - r_c[:CHUNK]
                v_new[c].append(vn[c])
                o_inter[c].append(r_c[CHUNK:])
            kd = jnp.concatenate([kd_ref[c, cs, :] for c in pair], axis=0)
            vn_bd = block_diag2(vn[pair[0]].astype(BF16), vn[pair[1]].astype(BF16), zeros_v)
            upd = lax.dot_general(kd, vn_bd, _TN, **ACC)
            for i, c in enumerate(pair):
                decay_s = jnp.concatenate([egl_ref[c, cs, :]] * (DN_HEAD_DIM // CHUNK), axis=0)
                state[c] = state[c] * decay_s + upd[:, i * DN_HEAD_DIM:(i + 1) * DN_HEAD_DIM]
        yield
    y_b = []
    for b in range(n_seq):
        heads = []
        for h in range(DN_HEADS):
            c = b * DN_HEADS + h
            s_ref[c] = state[c]
            o = (jnp.concatenate(o_inter[c], axis=0)
                 + jnp.dot(_block_diag(attn_ref[c], bd_mask),
                           jnp.concatenate(v_new[c], axis=0).astype(BF16), **ACC))
            heads.append(_rmsnorm(o, dng_ref[...])
                         * zg_ref[b * rows:(b + 1) * rows, h * DN_HEAD_DIM:(h + 1) * DN_HEAD_DIM])
        y_b.append(jnp.concatenate(heads, axis=1).astype(BF16))
        yield
    y = jnp.concatenate([ya_ref[...], jnp.concatenate(y_b, axis=0)], axis=1)
    for c0 in range(0, D_MODEL, PROJ_COLS):
        cols = slice(c0, c0 + PROJ_COLS)
        x = x_ref[:, :, cols].reshape(all_rows, PROJ_COLS)
        o_ref[:, :, cols] = (x + jnp.dot(y, wout_ref[:, cols], **ACC)).reshape(n_seq, rows, PROJ_COLS)
        yield


def _round_robin(*stages):
    stages = list(stages)
    while stages:
        for stage in list(stages):
            try:
                next(stage)
            except StopIteration:
                stages.remove(stage)


def _mixer_body(xa_ref, xb_ref, mg_ref, win_ref, wba_ref, lng_ref, lnb_ref, wcat_ref, bias_ref,
                cw_ref, alog_ref, dtb_ref, dng_ref, wout_ref, o_ref, s_ref, cin_ref, qkv_ref, *handoff):
    t = pl.program_id(0)
    n_tiles = pl.num_programs(0) - 1
    sets = (handoff[:N_HANDOFF], handoff[N_HANDOFF:])
    stage_a = functools.partial(_mixer_stage_a, xa_ref, mg_ref, win_ref, wba_ref, lng_ref, lnb_ref, wcat_ref,
                                bias_ref, cw_ref, alog_ref, dtb_ref, cin_ref, qkv_ref)
    stage_b = functools.partial(_mixer_stage_b, xb_ref, dng_ref, wout_ref, s_ref, o_ref)
    both = jnp.logical_and(t > 0, t < n_tiles)

    @pl.when(t == 0)
    def _():
        s_ref[...] = jnp.zeros(s_ref.shape, F32)
        cin_ref[...] = jnp.zeros(cin_ref.shape, F32)
        _round_robin(stage_a(sets[0]))

    @pl.when(jnp.logical_and(both, t % 2 == 0))
    def _():
        _round_robin(stage_b(sets[1]), stage_a(sets[0]))

    @pl.when(jnp.logical_and(both, t % 2 == 1))
    def _():
        _round_robin(stage_b(sets[0]), stage_a(sets[1]))

    for parity in range(2):
        @pl.when(jnp.logical_and(t == n_tiles, t % 2 == parity))
        def _():
            _round_robin(stage_b(sets[1 - parity]))


def _mixer_call(x, mix_norm, w_in, gm_ln_g, gm_ln_b, gm_w_s, gm_b_s, dn_conv_w, dn_a_log,
                dn_dt_bias, dn_norm, w_out):
    batch, seq, _ = x.shape
    n_tiles = seq // MIX_ROWS
    n_main = 2 * GM_WIDTH + 4 * DN_WIDTH
    w_in_t = w_in.T
    w_main = w_in_t[:n_main].astype(BF16)
    w_ba = jnp.pad(w_in_t[n_main:], ((0, LANES - 2 * DN_HEADS), (0, 0))).astype(BF16)
    w_cat = gm_w_s.reshape(GM_GROUPS // 2, 2, GM_BLOCK, GM_BLOCK).transpose(0, 2, 1, 3)
    w_cat = w_cat.reshape(GM_GROUPS // 2, GM_BLOCK, 2 * GM_BLOCK)
    bias = jnp.repeat(gm_b_s.T, GM_GROUP_DIM, axis=1)
    pad_a = lambda a: jnp.pad(a.reshape(1, DN_HEADS), ((0, 0), (DN_HEADS, LANES - 2 * DN_HEADS)))
    const2 = lambda t: (0, 0)
    tile_a = lambda t: (0, jnp.minimum(t, n_tiles - 1), 0)
    tile_b = lambda t: (0, jnp.maximum(t - 1, 0), 0)
    in_specs = [
        pl.BlockSpec((batch, MIX_ROWS, D_MODEL), tile_a),
        pl.BlockSpec((batch, MIX_ROWS, D_MODEL), tile_b),
        pl.BlockSpec((1, D_MODEL), const2),
        pl.BlockSpec((n_main, D_MODEL), const2),
        pl.BlockSpec((LANES, D_MODEL), const2),
        pl.BlockSpec((1, GM_WIDTH), const2),
        pl.BlockSpec((1, GM_WIDTH), const2),
        pl.BlockSpec((GM_GROUPS // 2, GM_BLOCK, 2 * GM_BLOCK), lambda t: (0, 0, 0)),
        pl.BlockSpec((GM_BLOCK, GM_WIDTH), const2),
        pl.BlockSpec((DN_CONV, 3 * DN_WIDTH), const2),
        pl.BlockSpec((1, LANES), const2),
        pl.BlockSpec((1, LANES), const2),
        pl.BlockSpec((1, DN_HEAD_DIM), const2),
        pl.BlockSpec((D_MODEL, D_MODEL), const2),
    ]
    return pl.pallas_call(
        _mixer_body,
        grid=(n_tiles + 1,),
        in_specs=in_specs,
        out_specs=pl.BlockSpec((batch, MIX_ROWS, D_MODEL), tile_b),
        out_shape=jax.ShapeDtypeStruct((batch, seq, D_MODEL), F32),
        scratch_shapes=[
            pltpu.VMEM((batch * DN_HEADS, DN_HEAD_DIM, DN_HEAD_DIM), F32),
            pltpu.VMEM((batch, 3 * DN_HEADS, CONV_TAIL + MIX_ROWS, LANES), F32),
            pltpu.VMEM((batch, 3 * DN_HEADS, MIX_ROWS, LANES), F32),
        ] + _handoff_shapes(batch) + _handoff_shapes(batch),
        compiler_params=pltpu.CompilerParams(
            dimension_semantics=("arbitrary",), vmem_limit_bytes=VMEM_LIMIT_BYTES),
        name="mixer",
    )(x, x, mix_norm.reshape(1, D_MODEL), w_main, w_ba, gm_ln_g.reshape(1, GM_WIDTH),
      gm_ln_b.reshape(1, GM_WIDTH), w_cat, bias, dn_conv_w, pad_a(dn_a_log), pad_a(dn_dt_bias),
      dn_norm.reshape(1, DN_HEAD_DIM), w_out.astype(BF16))


def kernel(x, ffn1_norm, ffn1_w_gate, ffn1_w_up, ffn1_w_down, mix_norm, w_in, gm_ln_g, gm_ln_b,
           gm_w_s, gm_b_s, dn_conv_w, dn_a_log, dn_dt_bias, dn_norm, w_out, ffn2_norm,
           ffn2_w_gate, ffn2_w_up, ffn2_w_down, final_norm):
    batch, seq, d = x.shape
    depth = ffn1_norm.shape[0]
    for l in range(depth):
        x = _ffn_call(x.reshape(batch * seq, d), ffn1_norm[l], ffn1_w_gate[l], ffn1_w_up[l],
                      ffn1_w_down[l]).reshape(batch, seq, d)
        x = _mixer_call(x, mix_norm[l], w_in[l], gm_ln_g[l], gm_ln_b[l], gm_w_s[l], gm_b_s[l],
                        dn_conv_w[l], dn_a_log[l], dn_dt_bias[l], dn_norm[l], w_out[l])
        last = l == depth - 1
        x = _ffn_call(x.reshape(batch * seq, d), ffn2_norm[l], ffn2_w_gate[l], ffn2_w_up[l],
                      ffn2_w_down[l], final_norm if last else None).reshape(batch, seq, d)
    return x
```

```python
import functools

import jax
import jax.numpy as jnp
from jax import lax
from jax.experimental import pallas as pl
from jax.experimental.pallas import tpu as pltpu

F32 = jnp.float32
BF16 = jnp.bfloat16
ACC = dict(preferred_element_type=jnp.float32)

D_MODEL = 1024
D_FF = 2816
CHUNK = 64
GM_GROUPS = 8
GM_GROUP_DIM = 64
GM_WIDTH = GM_GROUPS * GM_GROUP_DIM
GM_BLOCK = 128
DN_HEADS = 4
DN_HEAD_DIM = 128
DN_WIDTH = DN_HEADS * DN_HEAD_DIM
DN_CONV = 4
EPS = 1e-6

LANES = 128
SUBLANES = 8
VMEM_LIMIT_BYTES = 60000 * 1024

FFN_ROWS = 512
FFN_COLS = 256
MIX_ROWS = 256
MIX_CHUNKS = MIX_ROWS // CHUNK
CONV_TAIL = SUBLANES
PROJ_COLS = 256

_NT = (((1,), (1,)), ((), ()))
_TN = (((0,), (0,)), ((), ()))


def _rmsnorm(x, g):
    return (x * lax.rsqrt(jnp.mean(x * x, axis=-1, keepdims=True) + EPS)) * g


def _softplus(x):
    return jnp.maximum(x, 0.0) + jnp.log1p(jnp.exp(-jnp.abs(x)))


def _ffn_body(x_ref, g_ref, wg_hbm, wu_hbm, wd_hbm, *rest, final_norm):
    if final_norm:
        fg_ref, o_ref, wg_ref, wu_ref, wd_ref, stage_gu, stage_d, sem = rest
    else:
        o_ref, wg_ref, wu_ref, wd_ref, stage_gu, stage_d, sem = rest
    n_chunks = D_FF // FFN_COLS

    def chunk_copies(j, slot):
        cols = pl.ds(j * FFN_COLS, FFN_COLS)
        return (pltpu.make_async_copy(wg_hbm.at[:, cols], stage_gu.at[slot, 0], sem.at[slot, 0]),
                pltpu.make_async_copy(wu_hbm.at[:, cols], stage_gu.at[slot, 1], sem.at[slot, 1]),
                pltpu.make_async_copy(wd_hbm.at[cols, :], stage_d.at[slot], sem.at[slot, 2]))

    def run(load_weights):
        x = x_ref[...]
        hb = _rmsnorm(x, g_ref[...]).astype(BF16)
        acc = jnp.zeros(x.shape, F32)
        if load_weights:
            for cp in chunk_copies(0, 0):
                cp.start()
        for j in range(n_chunks):
            cols = slice(j * FFN_COLS, (j + 1) * FFN_COLS)
            if load_weights:
                slot = j % 2
                if j + 1 < n_chunks:
                    for cp in chunk_copies(j + 1, 1 - slot):
                        cp.start()
                for cp in chunk_copies(j, slot):
                    cp.wait()
                wg_ref[:, cols] = stage_gu[slot, 0].astype(BF16)
                wu_ref[:, cols] = stage_gu[slot, 1].astype(BF16)
                wd_ref[cols, :] = stage_d[slot].astype(BF16)
            gate = jnp.dot(hb, wg_ref[:, cols], **ACC)
            up = jnp.dot(hb, wu_ref[:, cols], **ACC)
            act = (jax.nn.silu(gate) * up).astype(BF16)
            acc = acc + jnp.dot(act, wd_ref[cols, :], **ACC)
        y = x + 0.5 * acc
        if final_norm:
            y = _rmsnorm(y, fg_ref[...])
        o_ref[...] = y

    first = pl.program_id(0) == 0
    pl.when(first)(functools.partial(run, True))
    pl.when(jnp.logical_not(first))(functools.partial(run, False))


def _ffn_call(x2d, gain, w_gate, w_up, w_down, final_gain=None):
    rows = x2d.shape[0]
    final_norm = final_gain is not None
    const = lambda i: (0, 0)
    hbm = pl.BlockSpec(memory_space=pl.ANY)
    in_specs = [
        pl.BlockSpec((FFN_ROWS, D_MODEL), lambda i: (i, 0)),
        pl.BlockSpec((1, D_MODEL), const),
        hbm, hbm, hbm,
    ]
    args = [x2d, gain.reshape(1, D_MODEL), w_gate, w_up, w_down]
    if final_norm:
        in_specs.append(pl.BlockSpec((1, D_MODEL), const))
        args.append(final_gain.reshape(1, D_MODEL))
    return pl.pallas_call(
        functools.partial(_ffn_body, final_norm=final_norm),
        grid=(rows // FFN_ROWS,),
        in_specs=in_specs,
        out_specs=pl.BlockSpec((FFN_ROWS, D_MODEL), lambda i: (i, 0)),
        out_shape=jax.ShapeDtypeStruct((rows, D_MODEL), F32),
        scratch_shapes=[
            pltpu.VMEM((D_MODEL, D_FF), BF16),
            pltpu.VMEM((D_MODEL, D_FF), BF16),
            pltpu.VMEM((D_FF, D_MODEL), BF16),
            pltpu.VMEM((2, 2, D_MODEL, FFN_COLS), F32),
            pltpu.VMEM((2, FFN_COLS, D_MODEL), F32),
            pltpu.SemaphoreType.DMA((2, 3)),
        ],
        compiler_params=pltpu.CompilerParams(
            dimension_semantics=("arbitrary",), vmem_limit_bytes=VMEM_LIMIT_BYTES),
        name="ffn_final" if final_norm else "ffn",
    )(*args)


def _split_hi_lo(x):
    hi = x.astype(BF16)
    lo = (x - hi.astype(F32)).astype(BF16)
    return hi, lo


def _lane_bcast(x, col):
    return jnp.broadcast_to(x[:, col:col + 1], (x.shape[0], LANES))


def _col_to_wide(col, lane_lo):
    left = jnp.where(lane_lo, col[0:CHUNK], col[CHUNK:2 * CHUNK])
    right = jnp.where(lane_lo, col[2 * CHUNK:3 * CHUNK], col[3 * CHUNK:])
    return jnp.concatenate([left, right], axis=1)


def _block_diag(wide_b, bd_mask):
    return jnp.where(bd_mask, jnp.concatenate([wide_b] * MIX_CHUNKS, axis=0), jnp.zeros((), wide_b.dtype))


def _handoff_shapes(n_seq):
    n_chain = n_seq * DN_HEADS
    return [
        pltpu.VMEM((n_seq * MIX_ROWS, GM_WIDTH), BF16),
        pltpu.VMEM((n_seq * MIX_ROWS, DN_WIDTH), F32),
        pltpu.VMEM((n_chain, CHUNK, MIX_ROWS), F32),
        pltpu.VMEM((n_chain, CHUNK, MIX_ROWS), BF16),
        pltpu.VMEM((n_chain, MIX_ROWS, 2 * DN_HEAD_DIM), BF16),
        pltpu.VMEM((n_chain, MIX_ROWS, DN_HEAD_DIM), BF16),
        pltpu.VMEM((n_chain, MIX_ROWS, DN_HEAD_DIM), BF16),
        pltpu.VMEM((n_chain, MIX_ROWS, DN_HEAD_DIM), F32),
    ]


N_HANDOFF = 8


def _mixer_stage_a(x_ref, mg_ref, win_ref, wba_ref, lng_ref, lnb_ref, wcat_ref, bias_ref, cw_ref,
                   alog_ref, dtb_ref, cin_ref, qkv_ref, out):
    ya_ref, zg_ref, m_ref, attn_ref, rhs_ref, qd_ref, kd_ref, egl_ref = out
    n_seq = x_ref.shape[0]
    rows = MIX_ROWS
    all_rows = n_seq * rows
    n_blocks = all_rows // GM_BLOCK

    x = x_ref[...].reshape(all_rows, D_MODEL)
    hb = _rmsnorm(x, mg_ref[...]).astype(BF16)
    yield

    def project(lo, hi):
        parts = []
        for c0 in range(lo, hi, PROJ_COLS):
            parts.append(jnp.dot(hb, win_ref[:, c0:min(c0 + PROJ_COLS, hi)], **ACC))
            yield
        return jnp.concatenate(parts, axis=1)

    for c0 in range(0, 3 * DN_WIDTH, PROJ_COLS):
        blk = jnp.dot(hb, win_ref[:, 2 * GM_WIDTH + c0:2 * GM_WIDTH + c0 + PROJ_COLS], **ACC)
        for b in range(n_seq):
            for j in range(PROJ_COLS // LANES):
                slab = cin_ref.at[b, c0 // LANES + j]
                slab[0:CONV_TAIL, :] = slab[rows:rows + CONV_TAIL, :]
                slab[CONV_TAIL:, :] = blk[b * rows:(b + 1) * rows, j * LANES:(j + 1) * LANES]
        yield
    pba = jnp.dot(hb, wba_ref[...], **ACC)
    yield
    half = rows // 2
    for b in range(n_seq):
        for s in range(3 * DN_WIDTH // LANES):
            src, dst = cin_ref.at[b, s], qkv_ref.at[b, s]
            w = cw_ref[:, s * LANES:(s + 1) * LANES]
            tap = lambda off: src[pl.ds(CONV_TAIL + off, half, stride=2), :]
            e0, e2, o1, om1, om3 = tap(0), tap(-2), tap(1), tap(-1), tap(-3)
            even = w[3:4] * e0 + w[2:3] * om1 + w[1:2] * e2 + w[0:1] * om3
            odd = w[3:4] * o1 + w[2:3] * e0 + w[1:2] * om1 + w[0:1] * e2
            dst[pl.ds(0, half, stride=2), :] = jax.nn.silu(even)
            dst[pl.ds(1, half, stride=2), :] = jax.nn.silu(odd)
    beta_all = jax.nn.sigmoid(pba)
    g_all = -jnp.exp(alog_ref[...]) * _softplus(pba + dtb_ref[...])

    u_a = jax.nn.gelu((yield from project(0, GM_WIDTH)))
    v_a = jax.nn.gelu((yield from project(GM_WIDTH, 2 * GM_WIDTH)))
    zg_ref[...] = jax.nn.silu((yield from project(2 * GM_WIDTH + 3 * DN_WIDTH, 2 * GM_WIDTH + 4 * DN_WIDTH)))

    ri = lax.broadcasted_iota(jnp.int32, (rows, rows), 0)
    ci = lax.broadcasted_iota(jnp.int32, (rows, rows), 1)
    bd_mask = (ri // CHUNK) == (ci // CHUNK)
    tri_b = jnp.where(bd_mask & (ci <= ri), 1.0, 0.0).astype(BF16)
    blk_b = jnp.where(bd_mask, 1.0, 0.0).astype(BF16)
    g_hi, g_lo = _split_hi_lo(jnp.concatenate([g_all[b * rows:(b + 1) * rows] for b in range(n_seq)], axis=1))
    gc_all = jnp.dot(tri_b, g_hi, **ACC) + jnp.dot(tri_b, g_lo, **ACC)
    gl_all = jnp.dot(blk_b, g_hi, **ACC) + jnp.dot(blk_b, g_lo, **ACC)
    yield
    e_gc_all = jnp.exp(gc_all)
    e_glgc_all = jnp.exp(gl_all - gc_all)
    e_gl_all = jnp.exp(gl_all)
    gc_rows = [gc_all[:, b * LANES:(b + 1) * LANES].T for b in range(n_seq)]

    mu = jnp.mean(v_a, axis=-1, keepdims=True)
    vc = v_a - mu
    var = jnp.mean(vc * vc, axis=-1, keepdims=True)
    v_n = (vc * lax.rsqrt(var + EPS)) * lng_ref[...] + lnb_ref[...]

    wi = lax.broadcasted_iota(jnp.int32, (GM_BLOCK, 2 * GM_BLOCK), 0) // CHUNK
    wj = (lax.broadcasted_iota(jnp.int32, (GM_BLOCK, 2 * GM_BLOCK), 1) % GM_BLOCK) // CHUNK
    w_mask = wj <= wi
    lane_lo128 = lax.broadcasted_iota(jnp.int32, (GM_BLOCK, LANES), 1) < GM_GROUP_DIM
    mixed_cols = []
    for p in range(GM_GROUPS // 2):
        v_p = v_n[:, p * LANES:(p + 1) * LANES]
        rhs = []
        for blk_i in range(n_blocks):
            blk = v_p[blk_i * GM_BLOCK:(blk_i + 1) * GM_BLOCK]
            rhs.append(jnp.concatenate(
                [jnp.where(lane_lo128, blk, 0.0), jnp.where(lane_lo128, 0.0, blk)], axis=0))
        rhs = jnp.concatenate(rhs, axis=1).astype(BF16)
        w_p = jnp.where(w_mask, wcat_ref[p], 0.0).astype(BF16)
        res = jnp.dot(w_p, rhs, **ACC)
        mixed_cols.append(jnp.concatenate(
            [res[:, i * LANES:(i + 1) * LANES] for i in range(n_blocks)], axis=0))
        if p % 2 == 1:
            yield
    bias = jnp.concatenate([bias_ref[...]] * n_blocks, axis=0)
    ya_ref[...] = (u_a * (jnp.concatenate(mixed_cols, axis=1) + bias)).astype(BF16)

    wr = lax.broadcasted_iota(jnp.int32, (CHUNK, rows), 0)
    wc = lax.broadcasted_iota(jnp.int32, (CHUNK, rows), 1) % CHUNK
    tri_w = wc <= wr
    strict_w = wc < wr
    lane_lo = lax.broadcasted_iota(jnp.int32, (CHUNK, LANES), 1) < CHUNK
    scale = DN_HEAD_DIM ** -0.5

    zeros_k = jnp.zeros((2 * CHUNK, DN_HEAD_DIM), BF16)
    for b in range(n_seq):
        rs = slice(b * rows, (b + 1) * rows)
        for h0 in range(0, DN_HEADS, 2):
            q, k, k_beta = {}, {}, {}
            for h in (h0, h0 + 1):
                chain = b * DN_HEADS + h
                col = b * LANES + DN_HEADS + h
                q_h, k_h, v_h = qkv_ref[b, h], qkv_ref[b, DN_HEADS + h], qkv_ref[b, 2 * DN_HEADS + h]
                q[h] = q_h * lax.rsqrt(jnp.sum(q_h * q_h, axis=-1, keepdims=True) + EPS) * scale
                k[h] = k_h * lax.rsqrt(jnp.sum(k_h * k_h, axis=-1, keepdims=True) + EPS)
                beta = _lane_bcast(beta_all[rs], h)
                e_gc = _lane_bcast(e_gc_all, col)
                k_beta[h] = k[h] * beta
                rhs_ref[chain] = jnp.concatenate([v_h * beta, k_beta[h] * e_gc], axis=1).astype(BF16)
                qd_ref[chain] = (q[h] * e_gc).astype(BF16)
                kd_ref[chain] = (k[h] * _lane_bcast(e_glgc_all, col)).astype(BF16)
                egl_ref[chain] = _lane_bcast(e_gl_all, col)
            kk_w = {h: [] for h in (h0, h0 + 1)}
            qk_w = {h: [] for h in (h0, h0 + 1)}
            for half_i in range(rows // LANES):
                hr = slice(half_i * LANES, (half_i + 1) * LANES)
                lhs = jnp.concatenate(
                    [jnp.concatenate([k_beta[h][hr], q[h][hr]], axis=0) for h in (h0, h0 + 1)], axis=1)
                kb0, kb1 = k[h0][hr].astype(BF16), k[h0 + 1][hr].astype(BF16)
                rhs_k = jnp.concatenate([jnp.concatenate([kb0, zeros_k], axis=1),
                                         jnp.concatenate([zeros_k, kb1], axis=1)], axis=0)
                out = lax.dot_general(lhs.astype(BF16), rhs_k, _NT, **ACC)
                for i, h in enumerate((h0, h0 + 1)):
                    kk = out[0:LANES, i * LANES:(i + 1) * LANES]
                    qk = out[LANES:, i * LANES:(i + 1) * LANES]
                    kk_w[h].append(jnp.where(lane_lo, kk[0:CHUNK], kk[CHUNK:]))
                    qk_w[h].append(jnp.where(lane_lo, qk[0:CHUNK], qk[CHUNK:]))
            for h in (h0, h0 + 1):
                chain = b * DN_HEADS + h
                col = b * LANES + DN_HEADS + h
                dlog = (_col_to_wide(_lane_bcast(gc_all, col), lane_lo)
                        - jnp.broadcast_to(gc_rows[b][DN_HEADS + h:DN_HEADS + h + 1, :], (CHUNK, rows)))
                decay = jnp.exp(jnp.where(tri_w, dlog, -jnp.inf))
                m_ref[chain] = -jnp.where(strict_w, jnp.concatenate(kk_w[h], axis=1) * decay, 0.0)
                attn_ref[chain] = (jnp.concatenate(qk_w[h], axis=1) * decay).astype(BF16)
            yield


def _mixer_stage_b(x_ref, dng_ref, wout_ref, s_ref, o_ref, inp):
    ya_ref, zg_ref, m_ref, attn_ref, rhs_ref, qd_ref, kd_ref, egl_ref = inp
    n_seq = x_ref.shape[0]
    rows = MIX_ROWS
    all_rows = n_seq * rows
    chains = range(n_seq * DN_HEADS)

    ri = lax.broadcasted_iota(jnp.int32, (rows, rows), 0)
    ci = lax.broadcasted_iota(jnp.int32, (rows, rows), 1)
    bd_mask = (ri // CHUNK) == (ci // CHUNK)
    wr = lax.broadcasted_iota(jnp.int32, (CHUNK, rows), 0)
    wc = lax.broadcasted_iota(jnp.int32, (CHUNK, rows), 1) % CHUNK
    eye_w = jnp.where(wc == wr, 1.0, 0.0)

    m_w = {c: m_ref[c] for c in chains}
    inv_w = {c: eye_w + m_w[c] for c in chains}
    m_bd = {c: _block_diag(m_w[c].astype(BF16), bd_mask) for c in chains}
    for _ in range(5):
        for c in chains:
            m_w[c] = jnp.dot(m_w[c].astype(BF16), m_bd[c], **ACC)
        yield
        for c in chains:
            m_bd[c] = _block_diag(m_w[c].astype(BF16), bd_mask)
        for c in chains:
            inv_w[c] = inv_w[c] + jnp.dot(inv_w[c].astype(BF16), m_bd[c], **ACC)
        yield
    uw = {c: jnp.dot(_block_diag(inv_w[c].astype(BF16), bd_mask), rhs_ref[c], **ACC) for c in chains}
    yield

    state = {c: s_ref[c] for c in chains}
    v_new = {c: [] for c in chains}
    o_inter = {c: [] for c in chains}
    pairs = [(c, c + 1) for c in range(0, n_seq * DN_HEADS, 2)]
    zeros_s = jnp.zeros((DN_HEAD_DIM, DN_HEAD_DIM), BF16)
    zeros_v = jnp.zeros((CHUNK, DN_HEAD_DIM), BF16)

    def block_diag2(a, b, zeros):
        return jnp.concatenate([jnp.concatenate([a, zeros], axis=1), jnp.concatenate([zeros, b], axis=1)], axis=0)

    for ch in range(MIX_CHUNKS):
        cs = slice(ch * CHUNK, (ch + 1) * CHUNK)
        r = {}
        for pair in pairs:
            wq = jnp.concatenate(
                [jnp.concatenate([uw[c][cs, DN_HEAD_DIM:].astype(BF16), qd_ref[c, cs, :]], axis=0) for c in pair],
                axis=1)
            s_bd = block_diag2(state[pair[0]].astype(BF16), state[pair[1]].astype(BF16), zeros_s)
            r[pair] = jnp.dot(wq, s_bd, **ACC)
        yield
        for pair in pairs:
            vn = {}
            for i, c in enumerate(pair):
                r_c = r[pair][:, i * DN_HEAD_DIM:(i + 1) * DN_HEAD_DIM]
                vn[c] = uw[c][cs, :DN_HEAD_DIM] - r_c[:CHUNK]
                v_new[c].append(vn[c])
                o_inter[c].append(r_c[CHUNK:])
            kd = jnp.concatenate([kd_ref[c, cs, :] for c in pair], axis=0)
            vn_bd = block_diag2(vn[pair[0]].astype(BF16), vn[pair[1]].astype(BF16), zeros_v)
            upd = lax.dot_general(kd, vn_bd, _TN, **ACC)
            for i, c in enumerate(pair):
                decay_s = jnp.concatenate([egl_ref[c, cs, :]] * (DN_HEAD_DIM // CHUNK), axis=0)
                state[c] = state[c] * decay_s + upd[:, i * DN_HEAD_DIM:(i + 1) * DN_HEAD_DIM]
        yield
    y_b = []
    for b in range(n_seq):
        heads = []
        for h in range(DN_HEADS):
            c = b * DN_HEADS + h
            s_ref[c] = state[c]
            o = (jnp.concatenate(o_inter[c], axis=0)
                 + jnp.dot(_block_diag(attn_ref[c], bd_mask),
                           jnp.concatenate(v_new[c], axis=0).astype(BF16), **ACC))
            heads.append(_rmsnorm(o, dng_ref[...])
                         * zg_ref[b * rows:(b + 1) * rows, h * DN_HEAD_DIM:(h + 1) * DN_HEAD_DIM])
        y_b.append(jnp.concatenate(heads, axis=1).astype(BF16))
        yield
    y = jnp.concatenate([ya_ref[...], jnp.concatenate(y_b, axis=0)], axis=1)
    for c0 in range(0, D_MODEL, PROJ_COLS):
        cols = slice(c0, c0 + PROJ_COLS)
        x = x_ref[:, :, cols].reshape(all_rows, PROJ_COLS)
        o_ref[:, :, cols] = (x + jnp.dot(y, wout_ref[:, cols], **ACC)).reshape(n_seq, rows, PROJ_COLS)
        yield


def _round_robin(*stages):
    stages = list(stages)
    while stages:
        for stage in list(stages):
            try:
                next(stage)
            except StopIteration:
                stages.remove(stage)


def _mixer_body(xa_ref, xb_ref, mg_ref, win_ref, wba_ref, lng_ref, lnb_ref, wcat_ref, bias_ref,
                cw_ref, alog_ref, dtb_ref, dng_ref, wout_ref, o_ref, s_ref, cin_ref, qkv_ref, *handoff):
    t = pl.program_id(0)
    n_tiles = pl.num_programs(0) - 1
    sets = (handoff[:N_HANDOFF], handoff[N_HANDOFF:])
    stage_a = functools.partial(_mixer_stage_a, xa_ref, mg_ref, win_ref, wba_ref, lng_ref, lnb_ref, wcat_ref,
                                bias_ref, cw_ref, alog_ref, dtb_ref, cin_ref, qkv_ref)
    stage_b = functools.partial(_mixer_stage_b, xb_ref, dng_ref, wout_ref, s_ref, o_ref)
    both = jnp.logical_and(t > 0, t < n_tiles)

    @pl.when(t == 0)
    def _():
        s_ref[...] = jnp.zeros(s_ref.shape, F32)
        cin_ref[...] = jnp.zeros(cin_ref.shape, F32)
        _round_robin(stage_a(sets[0]))

    @pl.when(jnp.logical_and(both, t % 2 == 0))
    def _():
        _round_robin(stage_b(sets[1]), stage_a(sets[0]))

    @pl.when(jnp.logical_and(both, t % 2 == 1))
    def _():
        _round_robin(stage_b(sets[0]), stage_a(sets[1]))

    for parity in range(2):
        @pl.when(jnp.logical_and(t == n_tiles, t % 2 == parity))
        def _():
            _round_robin(stage_b(sets[1 - parity]))


def _mixer_call(x, mix_norm, w_in, gm_ln_g, gm_ln_b, gm_w_s, gm_b_s, dn_conv_w, dn_a_log,
                dn_dt_bias, dn_norm, w_out):
    batch, seq, _ = x.shape
    n_tiles = seq // MIX_ROWS
    n_main = 2 * GM_WIDTH + 4 * DN_WIDTH
    w_main = w_in[:, :n_main].astype(BF16)
    w_ba = jnp.pad(w_in[:, n_main:], ((0, 0), (0, LANES - 2 * DN_HEADS))).astype(BF16)
    w_cat = gm_w_s.reshape(GM_GROUPS // 2, 2, GM_BLOCK, GM_BLOCK).transpose(0, 2, 1, 3)
    w_cat = w_cat.reshape(GM_GROUPS // 2, GM_BLOCK, 2 * GM_BLOCK)
    bias = jnp.repeat(gm_b_s.T, GM_GROUP_DIM, axis=1)
    pad_a = lambda a: jnp.pad(a.reshape(1, DN_HEADS), ((0, 0), (DN_HEADS, LANES - 2 * DN_HEADS)))
    const2 = lambda t: (0, 0)
    tile_a = lambda t: (0, jnp.minimum(t, n_tiles - 1), 0)
    tile_b = lambda t: (0, jnp.maximum(t - 1, 0), 0)
    in_specs = [
        pl.BlockSpec((batch, MIX_ROWS, D_MODEL), tile_a),
        pl.BlockSpec((batch, MIX_ROWS, D_MODEL), tile_b),
        pl.BlockSpec((1, D_MODEL), const2),
        pl.BlockSpec((D_MODEL, n_main), const2),
        pl.BlockSpec((D_MODEL, LANES), const2),
        pl.BlockSpec((1, GM_WIDTH), const2),
        pl.BlockSpec((1, GM_WIDTH), const2),
        pl.BlockSpec((GM_GROUPS // 2, GM_BLOCK, 2 * GM_BLOCK), lambda t: (0, 0, 0)),
        pl.BlockSpec((GM_BLOCK, GM_WIDTH), const2),
        pl.BlockSpec((DN_CONV, 3 * DN_WIDTH), const2),
        pl.BlockSpec((1, LANES), const2),
        pl.BlockSpec((1, LANES), const2),
        pl.BlockSpec((1, DN_HEAD_DIM), const2),
        pl.BlockSpec((D_MODEL, D_MODEL), const2),
    ]
    return pl.pallas_call(
        _mixer_body,
        grid=(n_tiles + 1,),
        in_specs=in_specs,
        out_specs=pl.BlockSpec((batch, MIX_ROWS, D_MODEL), tile_b),
        out_shape=jax.ShapeDtypeStruct((batch, seq, D_MODEL), F32),
        scratch_shapes=[
            pltpu.VMEM((batch * DN_HEADS, DN_HEAD_DIM, DN_HEAD_DIM), F32),
            pltpu.VMEM((batch, 3 * DN_HEADS, CONV_TAIL + MIX_ROWS, LANES), F32),
            pltpu.VMEM((batch, 3 * DN_HEADS, MIX_ROWS, LANES), F32),
        ] + _handoff_shapes(batch) + _handoff_shapes(batch),
        compiler_params=pltpu.CompilerParams(
            dimension_semantics=("arbitrary",), vmem_limit_bytes=VMEM_LIMIT_BYTES),
        name="mixer",
    )(x, x, mix_norm.reshape(1, D_MODEL), w_main, w_ba, gm_ln_g.reshape(1, GM_WIDTH),
      gm_ln_b.reshape(1, GM_WIDTH), w_cat, bias, dn_conv_w, pad_a(dn_a_log), pad_a(dn_dt_bias),
      dn_norm.reshape(1, DN_HEAD_DIM), w_out.astype(BF16))


def kernel(x, ffn1_norm, ffn1_w_gate, ffn1_w_up, ffn1_w_down, mix_norm, w_in, gm_ln_g, gm_ln_b,
           gm_w_s, gm_b_s, dn_conv_w, dn_a_log, dn_dt_bias, dn_norm, w_out, ffn2_norm,
           ffn2_w_gate, ffn2_w_up, ffn2_w_down, final_norm):
    batch, seq, d = x.shape
    depth = ffn1_norm.shape[0]
    for l in range(depth):
        x = _ffn_call(x.reshape(batch * seq, d), ffn1_norm[l], ffn1_w_gate[l], ffn1_w_up[l],
                      ffn1_w_down[l]).reshape(batch, seq, d)
        x = _mixer_call(x, mix_norm[l], w_in[l], gm_ln_g[l], gm_ln_b[l], gm_w_s[l], gm_b_s[l],
                        dn_conv_w[l], dn_a_log[l], dn_dt_bias[l], dn_norm[l], w_out[l])
        last = l == depth - 1
        x = _ffn_call(x.reshape(batch * seq, d), ffn2_norm[l], ffn2_w_gate[l], ffn2_w_up[l],
                      ffn2_w_down[l], final_norm if last else None).reshape(batch, seq, d)
    return x
```

```python
import functools

import jax
import jax.numpy as jnp
from jax import lax
from jax.experimental import pallas as pl
from jax.experimental.pallas import tpu as pltpu

F32 = jnp.float32
BF16 = jnp.bfloat16
ACC = dict(preferred_element_type=jnp.float32)

D_MODEL = 1024
D_FF = 2816
CHUNK = 64
GM_GROUPS = 8
GM_GROUP_DIM = 64
GM_WIDTH = GM_GROUPS * GM_GROUP_DIM
GM_BLOCK = 128
DN_HEADS = 4
DN_HEAD_DIM = 128
DN_WIDTH = DN_HEADS * DN_HEAD_DIM
DN_CONV = 4
EPS = 1e-6

LANES = 128
SUBLANES = 8
VMEM_LIMIT_BYTES = 60000 * 1024

FFN_ROWS = 512
FFN_COLS = 256
MIX_ROWS = 256
MIX_CHUNKS = MIX_ROWS // CHUNK
CONV_TAIL = SUBLANES
PROJ_COLS = 256

_NT = (((1,), (1,)), ((), ()))
_TN = (((0,), (0,)), ((), ()))


def _rmsnorm(x, g):
    return (x * lax.rsqrt(jnp.mean(x * x, axis=-1, keepdims=True) + EPS)) * g


def _softplus(x):
    return jnp.maximum(x, 0.0) + jnp.log1p(jnp.exp(-jnp.abs(x)))


def _ffn_body(x_ref, g_ref, wg_hbm, wu_hbm, wd_hbm, *rest, final_norm):
    if final_norm:
        fg_ref, o_ref, wg_ref, wu_ref, wd_ref, stage_gu, stage_d, sem = rest
    else:
        o_ref, wg_ref, wu_ref, wd_ref, stage_gu, stage_d, sem = rest
    n_chunks = D_FF // FFN_COLS

    def chunk_copies(j, slot):
        cols = pl.ds(j * FFN_COLS, FFN_COLS)
        return (pltpu.make_async_copy(wg_hbm.at[:, cols], stage_gu.at[slot, 0], sem.at[slot, 0]),
                pltpu.make_async_copy(wu_hbm.at[:, cols], stage_gu.at[slot, 1], sem.at[slot, 1]),
                pltpu.make_async_copy(wd_hbm.at[cols, :], stage_d.at[slot], sem.at[slot, 2]))

    def run(load_weights):
        x = x_ref[...]
        hb = _rmsnorm(x, g_ref[...]).astype(BF16)
        acc = jnp.zeros(x.shape, F32)
        if load_weights:
            for cp in chunk_copies(0, 0):
                cp.start()
        for j in range(n_chunks):
            cols = slice(j * FFN_COLS, (j + 1) * FFN_COLS)
            if load_weights:
                slot = j % 2
                if j + 1 < n_chunks:
                    for cp in chunk_copies(j + 1, 1 - slot):
                        cp.start()
                for cp in chunk_copies(j, slot):
                    cp.wait()
                wg_ref[:, cols] = stage_gu[slot, 0].astype(BF16)
                wu_ref[:, cols] = stage_gu[slot, 1].astype(BF16)
                wd_ref[cols, :] = stage_d[slot].astype(BF16)
            gate = jnp.dot(hb, wg_ref[:, cols], **ACC)
            up = jnp.dot(hb, wu_ref[:, cols], **ACC)
            act = (jax.nn.silu(gate) * up).astype(BF16)
            acc = acc + jnp.dot(act, wd_ref[cols, :], **ACC)
        y = x + 0.5 * acc
        if final_norm:
            y = _rmsnorm(y, fg_ref[...])
        o_ref[...] = y

    first = pl.program_id(0) == 0
    pl.when(first)(functools.partial(run, True))
    pl.when(jnp.logical_not(first))(functools.partial(run, False))


def _ffn_call(x2d, gain, w_gate, w_up, w_down, final_gain=None):
    rows = x2d.shape[0]
    final_norm = final_gain is not None
    const = lambda i: (0, 0)
    hbm = pl.BlockSpec(memory_space=pl.ANY)
    in_specs = [
        pl.BlockSpec((FFN_ROWS, D_MODEL), lambda i: (i, 0)),
        pl.BlockSpec((1, D_MODEL), const),
        hbm, hbm, hbm,
    ]
    args = [x2d, gain.reshape(1, D_MODEL), w_gate, w_up, w_down]
    if final_norm:
        in_specs.append(pl.BlockSpec((1, D_MODEL), const))
        args.append(final_gain.reshape(1, D_MODEL))
    return pl.pallas_call(
        functools.partial(_ffn_body, final_norm=final_norm),
        grid=(rows // FFN_ROWS,),
        in_specs=in_specs,
        out_specs=pl.BlockSpec((FFN_ROWS, D_MODEL), lambda i: (i, 0)),
        out_shape=jax.ShapeDtypeStruct((rows, D_MODEL), F32),
        scratch_shapes=[
            pltpu.VMEM((D_MODEL, D_FF), BF16),
            pltpu.VMEM((D_MODEL, D_FF), BF16),
            pltpu.VMEM((D_FF, D_MODEL), BF16),
            pltpu.VMEM((2, 2, D_MODEL, FFN_COLS), F32),
            pltpu.VMEM((2, FFN_COLS, D_MODEL), F32),
            pltpu.SemaphoreType.DMA((2, 3)),
        ],
        compiler_params=pltpu.CompilerParams(
            dimension_semantics=("arbitrary",), vmem_limit_bytes=VMEM_LIMIT_BYTES),
        name="ffn_final" if final_norm else "ffn",
    )(*args)


def _split_hi_lo(x):
    hi = x.astype(BF16)
    lo = (x - hi.astype(F32)).astype(BF16)
    return hi, lo


def _lane_bcast(x, col):
    return jnp.broadcast_to(x[:, col:col + 1], (x.shape[0], LANES))


def _col_to_wide(col, lane_lo):
    left = jnp.where(lane_lo, col[0:CHUNK], col[CHUNK:2 * CHUNK])
    right = jnp.where(lane_lo, col[2 * CHUNK:3 * CHUNK], col[3 * CHUNK:])
    return jnp.concatenate([left, right], axis=1)


def _block_diag(wide_b, bd_mask):
    return jnp.where(bd_mask, jnp.concatenate([wide_b] * MIX_CHUNKS, axis=0), jnp.zeros((), wide_b.dtype))


def _handoff_shapes(n_seq):
    n_chain = n_seq * DN_HEADS
    return [
        pltpu.VMEM((2, n_seq * MIX_ROWS, GM_WIDTH), BF16),
        pltpu.VMEM((2, n_seq * MIX_ROWS, DN_WIDTH), F32),
        pltpu.VMEM((2, n_chain, CHUNK, MIX_ROWS), F32),
        pltpu.VMEM((2, n_chain, CHUNK, MIX_ROWS), BF16),
        pltpu.VMEM((2, n_chain, MIX_ROWS, 2 * DN_HEAD_DIM), BF16),
        pltpu.VMEM((2, n_chain, MIX_ROWS, DN_HEAD_DIM), BF16),
        pltpu.VMEM((2, n_chain, MIX_ROWS, DN_HEAD_DIM), BF16),
        pltpu.VMEM((2, n_chain, MIX_ROWS, DN_HEAD_DIM), F32),
    ]


N_HANDOFF = 8


def _mixer_stage_a(x_ref, mg_ref, win_ref, wba_ref, lng_ref, lnb_ref, wcat_ref, bias_ref, cw_ref,
                   alog_ref, dtb_ref, cin_ref, qkv_ref, out):
    ya_ref, zg_ref, m_ref, attn_ref, rhs_ref, qd_ref, kd_ref, egl_ref = out
    n_seq = x_ref.shape[0]
    rows = MIX_ROWS
    all_rows = n_seq * rows
    n_blocks = all_rows // GM_BLOCK

    x = x_ref[...].reshape(all_rows, D_MODEL)
    hb = _rmsnorm(x, mg_ref[...]).astype(BF16)
    yield

    def project(lo, hi):
        parts = []
        for c0 in range(lo, hi, PROJ_COLS):
            parts.append(jnp.dot(hb, win_ref[:, c0:min(c0 + PROJ_COLS, hi)], **ACC))
            yield
        return jnp.concatenate(parts, axis=1)

    for c0 in range(0, 3 * DN_WIDTH, PROJ_COLS):
        blk = jnp.dot(hb, win_ref[:, 2 * GM_WIDTH + c0:2 * GM_WIDTH + c0 + PROJ_COLS], **ACC)
        for b in range(n_seq):
            for j in range(PROJ_COLS // LANES):
                slab = cin_ref.at[b, c0 // LANES + j]
                slab[0:CONV_TAIL, :] = slab[rows:rows + CONV_TAIL, :]
                slab[CONV_TAIL:, :] = blk[b * rows:(b + 1) * rows, j * LANES:(j + 1) * LANES]
        yield
    pba = jnp.dot(hb, wba_ref[...], **ACC)
    yield
    half = rows // 2
    for b in range(n_seq):
        for s in range(3 * DN_WIDTH // LANES):
            src, dst = cin_ref.at[b, s], qkv_ref.at[b, s]
            w = cw_ref[:, s * LANES:(s + 1) * LANES]
            tap = lambda off: src[pl.ds(CONV_TAIL + off, half, stride=2), :]
            e0, e2, o1, om1, om3 = tap(0), tap(-2), tap(1), tap(-1), tap(-3)
            even = w[3:4] * e0 + w[2:3] * om1 + w[1:2] * e2 + w[0:1] * om3
            odd = w[3:4] * o1 + w[2:3] * e0 + w[1:2] * om1 + w[0:1] * e2
            dst[pl.ds(0, half, stride=2), :] = jax.nn.silu(even)
            dst[pl.ds(1, half, stride=2), :] = jax.nn.silu(odd)
    beta_all = jax.nn.sigmoid(pba)
    g_all = -jnp.exp(alog_ref[...]) * _softplus(pba + dtb_ref[...])

    u_a = jax.nn.gelu((yield from project(0, GM_WIDTH)))
    v_a = jax.nn.gelu((yield from project(GM_WIDTH, 2 * GM_WIDTH)))
    zg_ref[...] = jax.nn.silu((yield from project(2 * GM_WIDTH + 3 * DN_WIDTH, 2 * GM_WIDTH + 4 * DN_WIDTH)))

    ri = lax.broadcasted_iota(jnp.int32, (rows, rows), 0)
    ci = lax.broadcasted_iota(jnp.int32, (rows, rows), 1)
    bd_mask = (ri // CHUNK) == (ci // CHUNK)
    tri_b = jnp.where(bd_mask & (ci <= ri), 1.0, 0.0).astype(BF16)
    blk_b = jnp.where(bd_mask, 1.0, 0.0).astype(BF16)
    g_hi, g_lo = _split_hi_lo(jnp.concatenate([g_all[b * rows:(b + 1) * rows] for b in range(n_seq)], axis=1))
    gc_all = jnp.dot(tri_b, g_hi, **ACC) + jnp.dot(tri_b, g_lo, **ACC)
    gl_all = jnp.dot(blk_b, g_hi, **ACC) + jnp.dot(blk_b, g_lo, **ACC)
    yield
    e_gc_all = jnp.exp(gc_all)
    e_glgc_all = jnp.exp(gl_all - gc_all)
    e_gl_all = jnp.exp(gl_all)
    gc_rows = [gc_all[:, b * LANES:(b + 1) * LANES].T for b in range(n_seq)]

    mu = jnp.mean(v_a, axis=-1, keepdims=True)
    vc = v_a - mu
    var = jnp.mean(vc * vc, axis=-1, keepdims=True)
    v_n = (vc * lax.rsqrt(var + EPS)) * lng_ref[...] + lnb_ref[...]

    wi = lax.broadcasted_iota(jnp.int32, (GM_BLOCK, 2 * GM_BLOCK), 0) // CHUNK
    wj = (lax.broadcasted_iota(jnp.int32, (GM_BLOCK, 2 * GM_BLOCK), 1) % GM_BLOCK) // CHUNK
    w_mask = wj <= wi
    lane_lo128 = lax.broadcasted_iota(jnp.int32, (GM_BLOCK, LANES), 1) < GM_GROUP_DIM
    mixed_cols = []
    for p in range(GM_GROUPS // 2):
        v_p = v_n[:, p * LANES:(p + 1) * LANES]
        rhs = []
        for blk_i in range(n_blocks):
            blk = v_p[blk_i * GM_BLOCK:(blk_i + 1) * GM_BLOCK]
            rhs.append(jnp.concatenate(
                [jnp.where(lane_lo128, blk, 0.0), jnp.where(lane_lo128, 0.0, blk)], axis=0))
        rhs = jnp.concatenate(rhs, axis=1).astype(BF16)
        w_p = jnp.where(w_mask, wcat_ref[p], 0.0).astype(BF16)
        res = jnp.dot(w_p, rhs, **ACC)
        mixed_cols.append(jnp.concatenate(
            [res[:, i * LANES:(i + 1) * LANES] for i in range(n_blocks)], axis=0))
        if p % 2 == 1:
            yield
    bias = jnp.concatenate([bias_ref[...]] * n_blocks, axis=0)
    ya_ref[...] = (u_a * (jnp.concatenate(mixed_cols, axis=1) + bias)).astype(BF16)

    wr = lax.broadcasted_iota(jnp.int32, (CHUNK, rows), 0)
    wc = lax.broadcasted_iota(jnp.int32, (CHUNK, rows), 1) % CHUNK
    tri_w = wc <= wr
    strict_w = wc < wr
    lane_lo = lax.broadcasted_iota(jnp.int32, (CHUNK, LANES), 1) < CHUNK
    scale = DN_HEAD_DIM ** -0.5

    zeros_k = jnp.zeros((2 * CHUNK, DN_HEAD_DIM), BF16)
    for b in range(n_seq):
        rs = slice(b * rows, (b + 1) * rows)
        for h0 in range(0, DN_HEADS, 2):
            q, k, k_beta = {}, {}, {}
            for h in (h0, h0 + 1):
                chain = b * DN_HEADS + h
                col = b * LANES + DN_HEADS + h
                q_h, k_h, v_h = qkv_ref[b, h], qkv_ref[b, DN_HEADS + h], qkv_ref[b, 2 * DN_HEADS + h]
                q[h] = q_h * lax.rsqrt(jnp.sum(q_h * q_h, axis=-1, keepdims=True) + EPS) * scale
                k[h] = k_h * lax.rsqrt(jnp.sum(k_h * k_h, axis=-1, keepdims=True) + EPS)
                beta = _lane_bcast(beta_all[rs], h)
                e_gc = _lane_bcast(e_gc_all, col)
                k_beta[h] = k[h] * beta
                rhs_ref[chain] = jnp.concatenate([v_h * beta, k_beta[h] * e_gc], axis=1).astype(BF16)
                qd_ref[chain] = (q[h] * e_gc).astype(BF16)
                kd_ref[chain] = (k[h] * _lane_bcast(e_glgc_all, col)).astype(BF16)
                egl_ref[chain] = _lane_bcast(e_gl_all, col)
            kk_w = {h: [] for h in (h0, h0 + 1)}
            qk_w = {h: [] for h in (h0, h0 + 1)}
            for half_i in range(rows // LANES):
                hr = slice(half_i * LANES, (half_i + 1) * LANES)
                lhs = jnp.concatenate(
                    [jnp.concatenate([k_beta[h][hr], q[h][hr]], axis=0) for h in (h0, h0 + 1)], axis=1)
                kb0, kb1 = k[h0][hr].astype(BF16), k[h0 + 1][hr].astype(BF16)
                rhs_k = jnp.concatenate([jnp.concatenate([kb0, zeros_k], axis=1),
                                         jnp.concatenate([zeros_k, kb1], axis=1)], axis=0)
                out = lax.dot_general(lhs.astype(BF16), rhs_k, _NT, **ACC)
                for i, h in enumerate((h0, h0 + 1)):
                    kk = out[0:LANES, i * LANES:(i + 1) * LANES]
                    qk = out[LANES:, i * LANES:(i + 1) * LANES]
                    kk_w[h].append(jnp.where(lane_lo, kk[0:CHUNK], kk[CHUNK:]))
                    qk_w[h].append(jnp.where(lane_lo, qk[0:CHUNK], qk[CHUNK:]))
            for h in (h0, h0 + 1):
                chain = b * DN_HEADS + h
                col = b * LANES + DN_HEADS + h
                dlog = (_col_to_wide(_lane_bcast(gc_all, col), lane_lo)
                        - jnp.broadcast_to(gc_rows[b][DN_HEADS + h:DN_HEADS + h + 1, :], (CHUNK, rows)))
                decay = jnp.exp(jnp.where(tri_w, dlog, -jnp.inf))
                m_ref[chain] = -jnp.where(strict_w, jnp.concatenate(kk_w[h], axis=1) * decay, 0.0)
                attn_ref[chain] = (jnp.concatenate(qk_w[h], axis=1) * decay).astype(BF16)
            yield


def _mixer_stage_b(x_ref, dng_ref, wout_ref, s_ref, o_ref, inp):
    ya_ref, zg_ref, m_ref, attn_ref, rhs_ref, qd_ref, kd_ref, egl_ref = inp
    n_seq = x_ref.shape[0]
    rows = MIX_ROWS
    all_rows = n_seq * rows
    chains = range(n_seq * DN_HEADS)

    ri = lax.broadcasted_iota(jnp.int32, (rows, rows), 0)
    ci = lax.broadcasted_iota(jnp.int32, (rows, rows), 1)
    bd_mask = (ri // CHUNK) == (ci // CHUNK)
    wr = lax.broadcasted_iota(jnp.int32, (CHUNK, rows), 0)
    wc = lax.broadcasted_iota(jnp.int32, (CHUNK, rows), 1) % CHUNK
    eye_w = jnp.where(wc == wr, 1.0, 0.0)

    m_w = {c: m_ref[c] for c in chains}
    inv_w = {c: eye_w + m_w[c] for c in chains}
    m_bd = {c: _block_diag(m_w[c].astype(BF16), bd_mask) for c in chains}
    for _ in range(5):
        for c in chains:
            m_w[c] = jnp.dot(m_w[c].astype(BF16), m_bd[c], **ACC)
        yield
        for c in chains:
            m_bd[c] = _block_diag(m_w[c].astype(BF16), bd_mask)
        for c in chains:
            inv_w[c] = inv_w[c] + jnp.dot(inv_w[c].astype(BF16), m_bd[c], **ACC)
        yield
    uw = {c: jnp.dot(_block_diag(inv_w[c].astype(BF16), bd_mask), rhs_ref[c], **ACC) for c in chains}
    yield

    state = {c: s_ref[c] for c in chains}
    v_new = {c: [] for c in chains}
    o_inter = {c: [] for c in chains}
    pairs = [(c, c + 1) for c in range(0, n_seq * DN_HEADS, 2)]
    zeros_s = jnp.zeros((DN_HEAD_DIM, DN_HEAD_DIM), BF16)
    zeros_v = jnp.zeros((CHUNK, DN_HEAD_DIM), BF16)

    def block_diag2(a, b, zeros):
        return jnp.concatenate([jnp.concatenate([a, zeros], axis=1), jnp.concatenate([zeros, b], axis=1)], axis=0)

    for ch in range(MIX_CHUNKS):
        cs = slice(ch * CHUNK, (ch + 1) * CHUNK)
        r = {}
        for pair in pairs:
            wq = jnp.concatenate(
                [jnp.concatenate([uw[c][cs, DN_HEAD_DIM:].astype(BF16), qd_ref[c, cs, :]], axis=0) for c in pair],
                axis=1)
            s_bd = block_diag2(state[pair[0]].astype(BF16), state[pair[1]].astype(BF16), zeros_s)
            r[pair] = jnp.dot(wq, s_bd, **ACC)
        yield
        for pair in pairs:
            vn = {}
            for i, c in enumerate(pair):
                r_c = r[pair][:, i * DN_HEAD_DIM:(i + 1) * DN_HEAD_DIM]
                vn[c] = uw[c][cs, :DN_HEAD_DIM] - r_c[:CHUNK]
                v_new[c].append(vn[c])
                o_inter[c].append(r_c[CHUNK:])
            kd = jnp.concatenate([kd_ref[c, cs, :] for c in pair], axis=0)
            vn_bd = block_diag2(vn[pair[0]].astype(BF16), vn[pair[1]].astype(BF16), zeros_v)
            upd = lax.dot_general(kd, vn_bd, _TN, **ACC)
            for i, c in enumerate(pair):
                decay_s = jnp.concatenate([egl_ref[c, cs, :]] * (DN_HEAD_DIM // CHUNK), axis=0)
                state[c] = state[c] * decay_s + upd[:, i * DN_HEAD_DIM:(i + 1) * DN_HEAD_DIM]
        yield
    y_b = []
    for b in range(n_seq):
        heads = []
        for h in range(DN_HEADS):
            c = b * DN_HEADS + h
            s_ref[c] = state[c]
            o = (jnp.concatenate(o_inter[c], axis=0)
                 + jnp.dot(_block_diag(attn_ref[c], bd_mask),
                           jnp.concatenate(v_new[c], axis=0).astype(BF16), **ACC))
            heads.append(_rmsnorm(o, dng_ref[...])
                         * zg_ref[b * rows:(b + 1) * rows, h * DN_HEAD_DIM:(h + 1) * DN_HEAD_DIM])
        y_b.append(jnp.concatenate(heads, axis=1).astype(BF16))
        yield
    y = jnp.concatenate([ya_ref[...], jnp.concatenate(y_b, axis=0)], axis=1)
    for c0 in range(0, D_MODEL, PROJ_COLS):
        cols = slice(c0, c0 + PROJ_COLS)
        x = x_ref[:, :, cols].reshape(all_rows, PROJ_COLS)
        o_ref[:, :, cols] = (x + jnp.dot(y, wout_ref[:, cols], **ACC)).reshape(n_seq, rows, PROJ_COLS)
        yield


def _round_robin(*stages):
    stages = list(stages)
    while stages:
        for stage in list(stages):
            try:
                next(stage)
            except StopIteration:
                stages.remove(stage)


def _mixer_body(xa_ref, xb_ref, mg_ref, win_ref, wba_ref, lng_ref, lnb_ref, wcat_ref, bias_ref,
                cw_ref, alog_ref, dtb_ref, dng_ref, wout_ref, o_ref, s_ref, cin_ref, qkv_ref, *handoff):
    t = pl.program_id(0)
    n_tiles = pl.num_programs(0) - 1
    slot = lax.rem(t, 2)
    write_set = [ref.at[slot] for ref in handoff]
    read_set = [ref.at[1 - slot] for ref in handoff]
    stage_a = functools.partial(_mixer_stage_a, xa_ref, mg_ref, win_ref, wba_ref, lng_ref, lnb_ref, wcat_ref,
                                bias_ref, cw_ref, alog_ref, dtb_ref, cin_ref, qkv_ref)
    stage_b = functools.partial(_mixer_stage_b, xb_ref, dng_ref, wout_ref, s_ref, o_ref)

    @pl.when(t == 0)
    def _():
        s_ref[...] = jnp.zeros(s_ref.shape, F32)
        cin_ref[...] = jnp.zeros(cin_ref.shape, F32)
        _round_robin(stage_a(write_set))

    @pl.when(jnp.logical_and(t > 0, t < n_tiles))
    def _():
        _round_robin(stage_b(read_set), stage_a(write_set))

    @pl.when(t == n_tiles)
    def _():
        _round_robin(stage_b(read_set))


def _mixer_call(x, mix_norm, w_in, gm_ln_g, gm_ln_b, gm_w_s, gm_b_s, dn_conv_w, dn_a_log,
                dn_dt_bias, dn_norm, w_out):
    batch, seq, _ = x.shape
    n_tiles = seq // MIX_ROWS
    n_main = 2 * GM_WIDTH + 4 * DN_WIDTH
    w_main = w_in[:, :n_main].astype(BF16)
    w_ba = jnp.pad(w_in[:, n_main:], ((0, 0), (0, LANES - 2 * DN_HEADS))).astype(BF16)
    w_cat = gm_w_s.reshape(GM_GROUPS // 2, 2, GM_BLOCK, GM_BLOCK).transpose(0, 2, 1, 3)
    w_cat = w_cat.reshape(GM_GROUPS // 2, GM_BLOCK, 2 * GM_BLOCK)
    bias = jnp.repeat(gm_b_s.T, GM_GROUP_DIM, axis=1)
    pad_a = lambda a: jnp.pad(a.reshape(1, DN_HEADS), ((0, 0), (DN_HEADS, LANES - 2 * DN_HEADS)))
    const2 = lambda t: (0, 0)
    tile_a = lambda t: (0, jnp.minimum(t, n_tiles - 1), 0)
    tile_b = lambda t: (0, jnp.maximum(t - 1, 0), 0)
    in_specs = [
        pl.BlockSpec((batch, MIX_ROWS, D_MODEL), tile_a),
        pl.BlockSpec((batch, MIX_ROWS, D_MODEL), tile_b),
        pl.BlockSpec((1, D_MODEL), const2),
        pl.BlockSpec((D_MODEL, n_main), const2),
        pl.BlockSpec((D_MODEL, LANES), const2),
        pl.BlockSpec((1, GM_WIDTH), const2),
        pl.BlockSpec((1, GM_WIDTH), const2),
        pl.BlockSpec((GM_GROUPS // 2, GM_BLOCK, 2 * GM_BLOCK), lambda t: (0, 0, 0)),
        pl.BlockSpec((GM_BLOCK, GM_WIDTH), const2),
        pl.BlockSpec((DN_CONV, 3 * DN_WIDTH), const2),
        pl.BlockSpec((1, LANES), const2),
        pl.BlockSpec((1, LANES), const2),
        pl.BlockSpec((1, DN_HEAD_DIM), const2),
        pl.BlockSpec((D_MODEL, D_MODEL), const2),
    ]
    return pl.pallas_call(
        _mixer_body,
        grid=(n_tiles + 1,),
        in_specs=in_specs,
        out_specs=pl.BlockSpec((batch, MIX_ROWS, D_MODEL), tile_b),
        out_shape=jax.ShapeDtypeStruct((batch, seq, D_MODEL), F32),
        scratch_shapes=[
            pltpu.VMEM((batch * DN_HEADS, DN_HEAD_DIM, DN_HEAD_DIM), F32),
            pltpu.VMEM((batch, 3 * DN_HEADS, CONV_TAIL + MIX_ROWS, LANES), F32),
            pltpu.VMEM((batch, 3 * DN_HEADS, MIX_ROWS, LANES), F32),
        ] + _handoff_shapes(batch),
        compiler_params=pltpu.CompilerParams(
            dimension_semantics=("arbitrary",), vmem_limit_bytes=VMEM_LIMIT_BYTES),
        name="mixer",
    )(x, x, mix_norm.reshape(1, D_MODEL), w_main, w_ba, gm_ln_g.reshape(1, GM_WIDTH),
      gm_ln_b.reshape(1, GM_WIDTH), w_cat, bias, dn_conv_w, pad_a(dn_a_log), pad_a(dn_dt_bias),
      dn_norm.reshape(1, DN_HEAD_DIM), w_out.astype(BF16))


def kernel(x, ffn1_norm, ffn1_w_gate, ffn1_w_up, ffn1_w_down, mix_norm, w_in, gm_ln_g, gm_ln_b,
           gm_w_s, gm_b_s, dn_conv_w, dn_a_log, dn_dt_bias, dn_norm, w_out, ffn2_norm,
           ffn2_w_gate, ffn2_w_up, ffn2_w_down, final_norm):
    batch, seq, d = x.shape
    depth = ffn1_norm.shape[0]
    for l in range(depth):
        x = _ffn_call(x.reshape(batch * seq, d), ffn1_norm[l], ffn1_w_gate[l], ffn1_w_up[l],
                      ffn1_w_down[l]).reshape(batch, seq, d)
        x = _mixer_call(x, mix_norm[l], w_in[l], gm_ln_g[l], gm_ln_b[l], gm_w_s[l], gm_b_s[l],
                        dn_conv_w[l], dn_a_log[l], dn_dt_bias[l], dn_norm[l], w_out[l])
        last = l == depth - 1
        x = _ffn_call(x.reshape(batch * seq, d), ffn2_norm[l], ffn2_w_gate[l], ffn2_w_up[l],
                      ffn2_w_down[l], final_norm if last else None).reshape(batch, seq, d)
    return x
```

```python
import functools

import jax
import jax.numpy as jnp
from jax import lax
from jax.experimental import pallas as pl
from jax.experimental.pallas import tpu as pltpu

F32 = jnp.float32
BF16 = jnp.bfloat16
ACC = dict(preferred_element_type=jnp.float32)

D_MODEL = 1024
D_FF = 2816
CHUNK = 64
GM_GROUPS = 8
GM_GROUP_DIM = 64
GM_WIDTH = GM_GROUPS * GM_GROUP_DIM
GM_BLOCK = 128
DN_HEADS = 4
DN_HEAD_DIM = 128
DN_WIDTH = DN_HEADS * DN_HEAD_DIM
DN_CONV = 4
EPS = 1e-6

LANES = 128
SUBLANES = 8
VMEM_LIMIT_BYTES = 60000 * 1024

FFN_ROWS = 512
FFN_COLS = 256
MIX_ROWS = 256
MIX_CHUNKS = MIX_ROWS // CHUNK
CONV_TAIL = SUBLANES
PROJ_COLS = 256
CONV_BLOCK = 64

_NT = (((1,), (1,)), ((), ()))
_TN = (((0,), (0,)), ((), ()))


def _rmsnorm(x, g):
    return (x * lax.rsqrt(jnp.mean(x * x, axis=-1, keepdims=True) + EPS)) * g


def _softplus(x):
    return jnp.maximum(x, 0.0) + jnp.log1p(jnp.exp(-jnp.abs(x)))


def _ffn_body(x_ref, g_ref, wg_hbm, wu_hbm, wd_hbm, *rest, final_norm):
    if final_norm:
        fg_ref, o_ref, wg_ref, wu_ref, wd_ref, stage_gu, stage_d, sem = rest
    else:
        o_ref, wg_ref, wu_ref, wd_ref, stage_gu, stage_d, sem = rest
    n_chunks = D_FF // FFN_COLS

    def chunk_copies(j, slot):
        cols = pl.ds(j * FFN_COLS, FFN_COLS)
        return (pltpu.make_async_copy(wg_hbm.at[:, cols], stage_gu.at[slot, 0], sem.at[slot, 0]),
                pltpu.make_async_copy(wu_hbm.at[:, cols], stage_gu.at[slot, 1], sem.at[slot, 1]),
                pltpu.make_async_copy(wd_hbm.at[cols, :], stage_d.at[slot], sem.at[slot, 2]))

    def run(load_weights):
        x = x_ref[...]
        hb = _rmsnorm(x, g_ref[...]).astype(BF16)
        acc = jnp.zeros(x.shape, F32)
        if load_weights:
            for cp in chunk_copies(0, 0):
                cp.start()
        for j in range(n_chunks):
            cols = slice(j * FFN_COLS, (j + 1) * FFN_COLS)
            if load_weights:
                slot = j % 2
                if j + 1 < n_chunks:
                    for cp in chunk_copies(j + 1, 1 - slot):
                        cp.start()
                for cp in chunk_copies(j, slot):
                    cp.wait()
                wg_ref[:, cols] = stage_gu[slot, 0].astype(BF16)
                wu_ref[:, cols] = stage_gu[slot, 1].astype(BF16)
                wd_ref[cols, :] = stage_d[slot].astype(BF16)
            gate = jnp.dot(hb, wg_ref[:, cols], **ACC)
            up = jnp.dot(hb, wu_ref[:, cols], **ACC)
            act = (jax.nn.silu(gate) * up).astype(BF16)
            acc = acc + jnp.dot(act, wd_ref[cols, :], **ACC)
        y = x + 0.5 * acc
        if final_norm:
            y = _rmsnorm(y, fg_ref[...])
        o_ref[...] = y

    first = pl.program_id(0) == 0
    pl.when(first)(functools.partial(run, True))
    pl.when(jnp.logical_not(first))(functools.partial(run, False))


def _ffn_call(x2d, gain, w_gate, w_up, w_down, final_gain=None):
    rows = x2d.shape[0]
    final_norm = final_gain is not None
    const = lambda i: (0, 0)
    hbm = pl.BlockSpec(memory_space=pl.ANY)
    in_specs = [
        pl.BlockSpec((FFN_ROWS, D_MODEL), lambda i: (i, 0)),
        pl.BlockSpec((1, D_MODEL), const),
        hbm, hbm, hbm,
    ]
    args = [x2d, gain.reshape(1, D_MODEL), w_gate, w_up, w_down]
    if final_norm:
        in_specs.append(pl.BlockSpec((1, D_MODEL), const))
        args.append(final_gain.reshape(1, D_MODEL))
    return pl.pallas_call(
        functools.partial(_ffn_body, final_norm=final_norm),
        grid=(rows // FFN_ROWS,),
        in_specs=in_specs,
        out_specs=pl.BlockSpec((FFN_ROWS, D_MODEL), lambda i: (i, 0)),
        out_shape=jax.ShapeDtypeStruct((rows, D_MODEL), F32),
        scratch_shapes=[
            pltpu.VMEM((D_MODEL, D_FF), BF16),
            pltpu.VMEM((D_MODEL, D_FF), BF16),
            pltpu.VMEM((D_FF, D_MODEL), BF16),
            pltpu.VMEM((2, 2, D_MODEL, FFN_COLS), F32),
            pltpu.VMEM((2, FFN_COLS, D_MODEL), F32),
            pltpu.SemaphoreType.DMA((2, 3)),
        ],
        compiler_params=pltpu.CompilerParams(
            dimension_semantics=("arbitrary",), vmem_limit_bytes=VMEM_LIMIT_BYTES),
        name="ffn_final" if final_norm else "ffn",
    )(*args)


def _split_hi_lo(x):
    hi = x.astype(BF16)
    lo = (x - hi.astype(F32)).astype(BF16)
    return hi, lo


def _lane_bcast(x, col):
    return jnp.broadcast_to(x[:, col:col + 1], (x.shape[0], LANES))


def _col_to_wide(col, lane_lo):
    left = jnp.where(lane_lo, col[0:CHUNK], col[CHUNK:2 * CHUNK])
    right = jnp.where(lane_lo, col[2 * CHUNK:3 * CHUNK], col[3 * CHUNK:])
    return jnp.concatenate([left, right], axis=1)


def _block_diag(wide_b, bd_mask):
    return jnp.where(bd_mask, jnp.concatenate([wide_b] * MIX_CHUNKS, axis=0), jnp.zeros((), wide_b.dtype))


def _handoff_shapes(n_seq):
    n_chain = n_seq * DN_HEADS
    return [
        pltpu.VMEM((2, n_seq * MIX_ROWS, GM_WIDTH), BF16),
        pltpu.VMEM((2, n_seq * MIX_ROWS, DN_WIDTH), F32),
        pltpu.VMEM((2, n_chain, CHUNK, MIX_ROWS), F32),
        pltpu.VMEM((2, n_chain, CHUNK, MIX_ROWS), BF16),
        pltpu.VMEM((2, n_chain, MIX_ROWS, 2 * DN_HEAD_DIM), BF16),
        pltpu.VMEM((2, n_chain, MIX_ROWS, DN_HEAD_DIM), BF16),
        pltpu.VMEM((2, n_chain, MIX_ROWS, DN_HEAD_DIM), BF16),
        pltpu.VMEM((2, n_chain, MIX_ROWS, DN_HEAD_DIM), F32),
    ]


N_HANDOFF = 8


def _mixer_stage_a(x_ref, mg_ref, win_ref, wba_ref, lng_ref, lnb_ref, wcat_ref, bias_ref, cw_ref,
                   alog_ref, dtb_ref, cin_ref, qkv_ref, out):
    ya_ref, zg_ref, m_ref, attn_ref, rhs_ref, qd_ref, kd_ref, egl_ref = out
    n_seq = x_ref.shape[0]
    rows = MIX_ROWS
    all_rows = n_seq * rows
    n_blocks = all_rows // GM_BLOCK

    x = x_ref[...].reshape(all_rows, D_MODEL)
    hb = _rmsnorm(x, mg_ref[...]).astype(BF16)
    yield

    def project(lo, hi):
        parts = []
        for c0 in range(lo, hi, PROJ_COLS):
            parts.append(jnp.dot(hb, win_ref[:, c0:min(c0 + PROJ_COLS, hi)], **ACC))
            yield
        return jnp.concatenate(parts, axis=1)

    for b in range(n_seq):
        for s in range(3 * DN_WIDTH // LANES):
            cin_ref[b, s, 0:CONV_TAIL, :] = cin_ref[b, s, rows:rows + CONV_TAIL, :]
    for c0 in range(0, 3 * DN_WIDTH, PROJ_COLS):
        blk = jnp.dot(hb, win_ref[:, 2 * GM_WIDTH + c0:2 * GM_WIDTH + c0 + PROJ_COLS], **ACC)
        for b in range(n_seq):
            for j in range(PROJ_COLS // LANES):
                cin_ref[b, c0 // LANES + j, CONV_TAIL:, :] = blk[b * rows:(b + 1) * rows, j * LANES:(j + 1) * LANES]
        yield
    pba = jnp.dot(hb, wba_ref[...], **ACC)
    yield
    half = CONV_BLOCK // 2
    for b in range(n_seq):
        for s in range(3 * DN_WIDTH // LANES):
            src, dst = cin_ref.at[b, s], qkv_ref.at[b, s]
            w = cw_ref[:, s * LANES:(s + 1) * LANES]
            for r0 in range(0, rows, CONV_BLOCK):
                tap = lambda off: src[pl.ds(CONV_TAIL + r0 + off, half, stride=2), :]
                e0, e2, o1, om1, om3 = tap(0), tap(-2), tap(1), tap(-1), tap(-3)
                even = w[3:4] * e0 + w[2:3] * om1 + w[1:2] * e2 + w[0:1] * om3
                odd = w[3:4] * o1 + w[2:3] * e0 + w[1:2] * om1 + w[0:1] * e2
                dst[pl.ds(r0, half, stride=2), :] = jax.nn.silu(even)
                dst[pl.ds(r0 + 1, half, stride=2), :] = jax.nn.silu(odd)
    beta_all = jax.nn.sigmoid(pba)
    g_all = -jnp.exp(alog_ref[...]) * _softplus(pba + dtb_ref[...])

    v_a = jax.nn.gelu((yield from project(GM_WIDTH, 2 * GM_WIDTH)))
    zg_ref[...] = jax.nn.silu((yield from project(2 * GM_WIDTH + 3 * DN_WIDTH, 2 * GM_WIDTH + 4 * DN_WIDTH)))

    ri = lax.broadcasted_iota(jnp.int32, (rows, rows), 0)
    ci = lax.broadcasted_iota(jnp.int32, (rows, rows), 1)
    bd_mask = (ri // CHUNK) == (ci // CHUNK)
    tri_b = jnp.where(bd_mask & (ci <= ri), 1.0, 0.0).astype(BF16)
    blk_b = jnp.where(bd_mask, 1.0, 0.0).astype(BF16)
    g_hi, g_lo = _split_hi_lo(jnp.concatenate([g_all[b * rows:(b + 1) * rows] for b in range(n_seq)], axis=1))
    gc_all = jnp.dot(tri_b, g_hi, **ACC) + jnp.dot(tri_b, g_lo, **ACC)
    gl_all = jnp.dot(blk_b, g_hi, **ACC) + jnp.dot(blk_b, g_lo, **ACC)
    yield
    e_gc_all = jnp.exp(gc_all)
    e_glgc_all = jnp.exp(gl_all - gc_all)
    e_gl_all = jnp.exp(gl_all)
    gc_rows = [gc_all[:, b * LANES:(b + 1) * LANES].T for b in range(n_seq)]

    mu = jnp.mean(v_a, axis=-1, keepdims=True)
    vc = v_a - mu
    var = jnp.mean(vc * vc, axis=-1, keepdims=True)
    v_n = (vc * lax.rsqrt(var + EPS)) * lng_ref[...] + lnb_ref[...]

    wi = lax.broadcasted_iota(jnp.int32, (GM_BLOCK, 2 * GM_BLOCK), 0) // CHUNK
    wj = (lax.broadcasted_iota(jnp.int32, (GM_BLOCK, 2 * GM_BLOCK), 1) % GM_BLOCK) // CHUNK
    w_mask = wj <= wi
    lane_lo128 = lax.broadcasted_iota(jnp.int32, (GM_BLOCK, LANES), 1) < GM_GROUP_DIM
    mixed_cols = []
    for p in range(GM_GROUPS // 2):
        v_p = v_n[:, p * LANES:(p + 1) * LANES]
        rhs = []
        for blk_i in range(n_blocks):
            blk = v_p[blk_i * GM_BLOCK:(blk_i + 1) * GM_BLOCK]
            rhs.append(jnp.concatenate(
                [jnp.where(lane_lo128, blk, 0.0), jnp.where(lane_lo128, 0.0, blk)], axis=0))
        rhs = jnp.concatenate(rhs, axis=1).astype(BF16)
        w_p = jnp.where(w_mask, wcat_ref[p], 0.0).astype(BF16)
        res = jnp.dot(w_p, rhs, **ACC)
        mixed_cols.append(jnp.concatenate(
            [res[:, i * LANES:(i + 1) * LANES] for i in range(n_blocks)], axis=0))
        if p % 2 == 1:
            yield
    bias = jnp.concatenate([bias_ref[...]] * n_blocks, axis=0)
    u_a = jax.nn.gelu((yield from project(0, GM_WIDTH)))
    ya_ref[...] =(u_a * (jnp.concatenate(mixed_cols, axis=1) + bias)).astype(BF16)

    wr = lax.broadcasted_iota(jnp.int32, (CHUNK, rows), 0)
    wc = lax.broadcasted_iota(jnp.int32, (CHUNK, rows), 1) % CHUNK
    tri_w = wc <= wr
    strict_w = wc < wr
    lane_lo = lax.broadcasted_iota(jnp.int32, (CHUNK, LANES), 1) < CHUNK
    scale = DN_HEAD_DIM ** -0.5

    zeros_k = jnp.zeros((2 * CHUNK, DN_HEAD_DIM), BF16)
    for b in range(n_seq):
        rs = slice(b * rows, (b + 1) * rows)
        for h0 in range(0, DN_HEADS, 2):
            q, k, k_beta = {}, {}, {}
            for h in (h0, h0 + 1):
                chain = b * DN_HEADS + h
                col = b * LANES + DN_HEADS + h
                q_h, k_h, v_h = qkv_ref[b, h], qkv_ref[b, DN_HEADS + h], qkv_ref[b, 2 * DN_HEADS + h]
                q[h] = q_h * lax.rsqrt(jnp.sum(q_h * q_h, axis=-1, keepdims=True) + EPS) * scale
                k[h] = k_h * lax.rsqrt(jnp.sum(k_h * k_h, axis=-1, keepdims=True) + EPS)
                beta = _lane_bcast(beta_all[rs], h)
                e_gc = _lane_bcast(e_gc_all, col)
                k_beta[h] = k[h] * beta
                rhs_ref[chain] = jnp.concatenate([v_h * beta, k_beta[h] * e_gc], axis=1).astype(BF16)
                qd_ref[chain] = (q[h] * e_gc).astype(BF16)
                kd_ref[chain] = (k[h] * _lane_bcast(e_glgc_all, col)).astype(BF16)
                egl_ref[chain] = _lane_bcast(e_gl_all, col)
            kk_w = {h: [] for h in (h0, h0 + 1)}
            qk_w = {h: [] for h in (h0, h0 + 1)}
            for half_i in range(rows // LANES):
                hr = slice(half_i * LANES, (half_i + 1) * LANES)
                lhs = jnp.concatenate(
                    [jnp.concatenate([k_beta[h][hr], q[h][hr]], axis=0) for h in (h0, h0 + 1)], axis=1)
                kb0, kb1 = k[h0][hr].astype(BF16), k[h0 + 1][hr].astype(BF16)
                rhs_k = jnp.concatenate([jnp.concatenate([kb0, zeros_k], axis=1),
                                         jnp.concatenate([zeros_k, kb1], axis=1)], axis=0)
                out = lax.dot_general(lhs.astype(BF16), rhs_k, _NT, **ACC)
                for i, h in enumerate((h0, h0 + 1)):
                    kk = out[0:LANES, i * LANES:(i + 1) * LANES]
                    qk = out[LANES:, i * LANES:(i + 1) * LANES]
                    kk_w[h].append(jnp.where(lane_lo, kk[0:CHUNK], kk[CHUNK:]))
                    qk_w[h].append(jnp.where(lane_lo, qk[0:CHUNK], qk[CHUNK:]))
            for h in (h0, h0 + 1):
                chain = b * DN_HEADS + h
                col = b * LANES + DN_HEADS + h
                dlog = (_col_to_wide(_lane_bcast(gc_all, col), lane_lo)
                        - jnp.broadcast_to(gc_rows[b][DN_HEADS + h:DN_HEADS + h + 1, :], (CHUNK, rows)))
                decay = jnp.exp(jnp.where(tri_w, dlog, -jnp.inf))
                m_ref[chain] = -jnp.where(strict_w, jnp.concatenate(kk_w[h], axis=1) * decay, 0.0)
                attn_ref[chain] = (jnp.concatenate(qk_w[h], axis=1) * decay).astype(BF16)
            yield


def _mixer_stage_b(x_ref, dng_ref, wout_ref, s_ref, o_ref, inp):
    ya_ref, zg_ref, m_ref, attn_ref, rhs_ref, qd_ref, kd_ref, egl_ref = inp
    n_seq = x_ref.shape[0]
    rows = MIX_ROWS
    all_rows = n_seq * rows
    chains = range(n_seq * DN_HEADS)

    ri = lax.broadcasted_iota(jnp.int32, (rows, rows), 0)
    ci = lax.broadcasted_iota(jnp.int32, (rows, rows), 1)
    bd_mask = (ri // CHUNK) == (ci // CHUNK)
    wr = lax.broadcasted_iota(jnp.int32, (CHUNK, rows), 0)
    wc = lax.broadcasted_iota(jnp.int32, (CHUNK, rows), 1) % CHUNK
    eye_w = jnp.where(wc == wr, 1.0, 0.0)

    m_w = {c: m_ref[c] for c in chains}
    inv_w = {c: eye_w + m_w[c] for c in chains}
    m_bd = {c: _block_diag(m_w[c].astype(BF16), bd_mask) for c in chains}
    for _ in range(5):
        for c in chains:
            m_w[c] = jnp.dot(m_w[c].astype(BF16), m_bd[c], **ACC)
        yield
        for c in chains:
            m_bd[c] = _block_diag(m_w[c].astype(BF16), bd_mask)
        for c in chains:
            inv_w[c] = inv_w[c] + jnp.dot(inv_w[c].astype(BF16), m_bd[c], **ACC)
        yield
    uw = {c: jnp.dot(_block_diag(inv_w[c].astype(BF16), bd_mask), rhs_ref[c], **ACC) for c in chains}
    yield

    state = {c: s_ref[c] for c in chains}
    v_new = {c: [] for c in chains}
    o_inter = {c: [] for c in chains}
    pairs = [(c, c + 1) for c in range(0, n_seq * DN_HEADS, 2)]
    zeros_s = jnp.zeros((DN_HEAD_DIM, DN_HEAD_DIM), BF16)
    zeros_v = jnp.zeros((CHUNK, DN_HEAD_DIM), BF16)

    def block_diag2(a, b, zeros):
        return jnp.concatenate([jnp.concatenate([a, zeros], axis=1), jnp.concatenate([zeros, b], axis=1)], axis=0)

    for ch in range(MIX_CHUNKS):
        cs = slice(ch * CHUNK, (ch + 1) * CHUNK)
        r = {}
        for pair in pairs:
            wq = jnp.concatenate(
                [jnp.concatenate([uw[c][cs, DN_HEAD_DIM:].astype(BF16), qd_ref[c, cs, :]], axis=0) for c in pair],
                axis=1)
            s_bd = block_diag2(state[pair[0]].astype(BF16), state[pair[1]].astype(BF16), zeros_s)
            r[pair] = jnp.dot(wq, s_bd, **ACC)
        yield
        for pair in pairs:
            vn = {}
            for i, c in enumerate(pair):
                r_c = r[pair][:, i * DN_HEAD_DIM:(i + 1) * DN_HEAD_DIM]
                vn[c] = uw[c][cs, :DN_HEAD_DIM] - r_c[:CHUNK]
                v_new[c].append(vn[c])
                o_inter[c].append(r_c[CHUNK:])
            kd = jnp.concatenate([kd_ref[c, cs, :] for c in pair], axis=0)
            vn_bd = block_diag2(vn[pair[0]].astype(BF16), vn[pair[1]].astype(BF16), zeros_v)
            upd = lax.dot_general(kd, vn_bd, _TN, **ACC)
            for i, c in enumerate(pair):
                decay_s = jnp.concatenate([egl_ref[c, cs, :]] * (DN_HEAD_DIM // CHUNK), axis=0)
                state[c] = state[c] * decay_s + upd[:, i * DN_HEAD_DIM:(i + 1) * DN_HEAD_DIM]
        yield
    y_b = []
    for b in range(n_seq):
        heads = []
        for h in range(DN_HEADS):
            c = b * DN_HEADS + h
            s_ref[c] = state[c]
            o = (jnp.concatenate(o_inter[c], axis=0)
                 + jnp.dot(_block_diag(attn_ref[c], bd_mask),
                           jnp.concatenate(v_new[c], axis=0).astype(BF16), **ACC))
            heads.append(_rmsnorm(o, dng_ref[...])
                         * zg_ref[b * rows:(b + 1) * rows, h * DN_HEAD_DIM:(h + 1) * DN_HEAD_DIM])
        y_b.append(jnp.concatenate(heads, axis=1).astype(BF16))
        yield
    y = jnp.concatenate([ya_ref[...], jnp.concatenate(y_b, axis=0)], axis=1)
    for c0 in range(0, D_MODEL, PROJ_COLS):
        cols = slice(c0, c0 + PROJ_COLS)
        x = x_ref[:, :, cols].reshape(all_rows, PROJ_COLS)
        o_ref[:, :, cols] = (x + jnp.dot(y, wout_ref[:, cols], **ACC)).reshape(n_seq, rows, PROJ_COLS)
        yield


def _round_robin(*stages):
    stages = list(stages)
    while stages:
        for stage in list(stages):
            try:
                next(stage)
            except StopIteration:
                stages.remove(stage)


def _mixer_body(xa_ref, xb_ref, mg_ref, win_ref, wba_ref, lng_ref, lnb_ref, wcat_ref, bias_ref,
                cw_ref, alog_ref, dtb_ref, dng_ref, wout_ref, o_ref, s_ref, cin_ref, qkv_ref, *handoff):
    t = pl.program_id(0)
    n_tiles = pl.num_programs(0) - 1
    slot = lax.rem(t, 2)
    write_set = [ref.at[slot] for ref in handoff]
    read_set = [ref.at[1 - slot] for ref in handoff]
    stage_a = functools.partial(_mixer_stage_a, xa_ref, mg_ref, win_ref, wba_ref, lng_ref, lnb_ref, wcat_ref,
                                bias_ref, cw_ref, alog_ref, dtb_ref, cin_ref, qkv_ref)
    stage_b = functools.partial(_mixer_stage_b, xb_ref, dng_ref, wout_ref, s_ref, o_ref)

    @pl.when(t == 0)
    def _():
        s_ref[...] = jnp.zeros(s_ref.shape, F32)
        cin_ref[...] = jnp.zeros(cin_ref.shape, F32)
        _round_robin(stage_a(write_set))

    @pl.when(jnp.logical_and(t > 0, t < n_tiles))
    def _():
        _round_robin(stage_b(read_set), stage_a(write_set))

    @pl.when(t == n_tiles)
    def _():
        _round_robin(stage_b(read_set))


def _mixer_call(x, mix_norm, w_in, gm_ln_g, gm_ln_b, gm_w_s, gm_b_s, dn_conv_w, dn_a_log,
                dn_dt_bias, dn_norm, w_out):
    batch, seq, _ = x.shape
    n_tiles = seq // MIX_ROWS
    n_main = 2 * GM_WIDTH + 4 * DN_WIDTH
    w_main = w_in[:, :n_main].astype(BF16)
    w_ba = jnp.pad(w_in[:, n_main:], ((0, 0), (0, LANES - 2 * DN_HEADS))).astype(BF16)
    w_cat = gm_w_s.reshape(GM_GROUPS // 2, 2, GM_BLOCK, GM_BLOCK).transpose(0, 2, 1, 3)
    w_cat = w_cat.reshape(GM_GROUPS // 2, GM_BLOCK, 2 * GM_BLOCK)
    bias = jnp.repeat(gm_b_s.T, GM_GROUP_DIM, axis=1)
    pad_a = lambda a: jnp.pad(a.reshape(1, DN_HEADS), ((0, 0), (DN_HEADS, LANES - 2 * DN_HEADS)))
    const2 = lambda t: (0, 0)
    tile_a = lambda t: (0, jnp.minimum(t, n_tiles - 1), 0)
    tile_b = lambda t: (0, jnp.maximum(t - 1, 0), 0)
    in_specs = [
        pl.BlockSpec((batch, MIX_ROWS, D_MODEL), tile_a),
        pl.BlockSpec((batch, MIX_ROWS, D_MODEL), tile_b),
        pl.BlockSpec((1, D_MODEL), const2),
        pl.BlockSpec((D_MODEL, n_main), const2),
        pl.BlockSpec((D_MODEL, LANES), const2),
        pl.BlockSpec((1, GM_WIDTH), const2),
        pl.BlockSpec((1, GM_WIDTH), const2),
        pl.BlockSpec((GM_GROUPS // 2, GM_BLOCK, 2 * GM_BLOCK), lambda t: (0, 0, 0)),
        pl.BlockSpec((GM_BLOCK, GM_WIDTH), const2),
        pl.BlockSpec((DN_CONV, 3 * DN_WIDTH), const2),
        pl.BlockSpec((1, LANES), const2),
        pl.BlockSpec((1, LANES), const2),
        pl.BlockSpec((1, DN_HEAD_DIM), const2),
        pl.BlockSpec((D_MODEL, D_MODEL), const2),
    ]
    return pl.pallas_call(
        _mixer_body,
        grid=(n_tiles + 1,),
        in_specs=in_specs,
        out_specs=pl.BlockSpec((batch, MIX_ROWS, D_MODEL), tile_b),
        out_shape=jax.ShapeDtypeStruct((batch, seq, D_MODEL), F32),
        scratch_shapes=[
            pltpu.VMEM((batch * DN_HEADS, DN_HEAD_DIM, DN_HEAD_DIM), F32),
            pltpu.VMEM((batch, 3 * DN_HEADS, CONV_TAIL + MIX_ROWS, LANES), F32),
            pltpu.VMEM((batch, 3 * DN_HEADS, MIX_ROWS, LANES), F32),
        ] + _handoff_shapes(batch),
        compiler_params=pltpu.CompilerParams(
            dimension_semantics=("arbitrary",), vmem_limit_bytes=VMEM_LIMIT_BYTES),
        name="mixer",
    )(x, x, mix_norm.reshape(1, D_MODEL), w_main, w_ba, gm_ln_g.reshape(1, GM_WIDTH),
      gm_ln_b.reshape(1, GM_WIDTH), w_cat, bias, dn_conv_w, pad_a(dn_a_log), pad_a(dn_dt_bias),
      dn_norm.reshape(1, DN_HEAD_DIM), w_out.astype(BF16))


def kernel(x, ffn1_norm, ffn1_w_gate, ffn1_w_up, ffn1_w_down, mix_norm, w_in, gm_ln_g, gm_ln_b,
           gm_w_s, gm_b_s, dn_conv_w, dn_a_log, dn_dt_bias, dn_norm, w_out, ffn2_norm,
           ffn2_w_gate, ffn2_w_up, ffn2_w_down, final_norm):
    batch, seq, d = x.shape
    depth = ffn1_norm.shape[0]
    for l in range(depth):
        x = _ffn_call(x.reshape(batch * seq, d), ffn1_norm[l], ffn1_w_gate[l], ffn1_w_up[l],
                      ffn1_w_down[l]).reshape(batch, seq, d)
        x = _mixer_call(x, mix_norm[l], w_in[l], gm_ln_g[l], gm_ln_b[l], gm_w_s[l], gm_b_s[l],
                        dn_conv_w[l], dn_a_log[l], dn_dt_bias[l], dn_norm[l], w_out[l])
        last = l == depth - 1
        x = _ffn_call(x.reshape(batch * seq, d), ffn2_norm[l], ffn2_w_gate[l], ffn2_w_up[l],
                      ffn2_w_down[l], final_norm if last else None).reshape(batch, seq, d)
    return x
```

```python
import functools

import jax
import jax.numpy as jnp
from jax import lax
from jax.experimental import pallas as pl
from jax.experimental.pallas import tpu as pltpu

F32 = jnp.float32
BF16 = jnp.bfloat16
ACC = dict(preferred_element_type=jnp.float32)

D_MODEL = 1024
D_FF = 2816
CHUNK = 64
GM_GROUPS = 8
GM_GROUP_DIM = 64
GM_WIDTH = GM_GROUPS * GM_GROUP_DIM
GM_BLOCK = 128
DN_HEADS = 4
DN_HEAD_DIM = 128
DN_WIDTH = DN_HEADS * DN_HEAD_DIM
DN_CONV = 4
EPS = 1e-6

LANES = 128
SUBLANES = 8
VMEM_LIMIT_BYTES = 60000 * 1024

FFN_ROWS = 512
FFN_COLS = 256
MIX_ROWS = 256
MIX_CHUNKS = MIX_ROWS // CHUNK
CONV_TAIL = SUBLANES
PROJ_COLS = 256
CONV_BLOCK = 64

_NT = (((1,), (1,)), ((), ()))
_TN = (((0,), (0,)), ((), ()))


def _rmsnorm(x, g):
    return (x * lax.rsqrt(jnp.mean(x * x, axis=-1, keepdims=True) + EPS)) * g


def _softplus(x):
    return jnp.maximum(x, 0.0) + jnp.log1p(jnp.exp(-jnp.abs(x)))


def _ffn_body(x_ref, g_ref, wg_hbm, wu_hbm, wd_hbm, *rest, final_norm):
    if final_norm:
        fg_ref, o_ref, wg_ref, wu_ref, wd_ref, stage_gu, stage_d, sem = rest
    else:
        o_ref, wg_ref, wu_ref, wd_ref, stage_gu, stage_d, sem = rest
    n_chunks = D_FF // FFN_COLS

    def chunk_copies(j, slot):
        cols = pl.ds(j * FFN_COLS, FFN_COLS)
        return (pltpu.make_async_copy(wg_hbm.at[:, cols], stage_gu.at[slot, 0], sem.at[slot, 0]),
                pltpu.make_async_copy(wu_hbm.at[:, cols], stage_gu.at[slot, 1], sem.at[slot, 1]),
                pltpu.make_async_copy(wd_hbm.at[cols, :], stage_d.at[slot], sem.at[slot, 2]))

    def run(load_weights):
        x = x_ref[...]
        hb = _rmsnorm(x, g_ref[...]).astype(BF16)
        acc = jnp.zeros(x.shape, F32)
        if load_weights:
            for cp in chunk_copies(0, 0):
                cp.start()
        for j in range(n_chunks):
            cols = slice(j * FFN_COLS, (j + 1) * FFN_COLS)
            if load_weights:
                slot = j % 2
                if j + 1 < n_chunks:
                    for cp in chunk_copies(j + 1, 1 - slot):
                        cp.start()
                for cp in chunk_copies(j, slot):
                    cp.wait()
                wg_ref[:, cols] = stage_gu[slot, 0].astype(BF16)
                wu_ref[:, cols] = stage_gu[slot, 1].astype(BF16)
                wd_ref[cols, :] = stage_d[slot].astype(BF16)
            gate = jnp.dot(hb, wg_ref[:, cols], **ACC)
            up = jnp.dot(hb, wu_ref[:, cols], **ACC)
            act = (jax.nn.silu(gate) * up).astype(BF16)
            acc = acc + jnp.dot(act, wd_ref[cols, :], **ACC)
        y = x + 0.5 * acc
        if final_norm:
            y = _rmsnorm(y, fg_ref[...])
        o_ref[...] = y

    first = pl.program_id(0) == 0
    pl.when(first)(functools.partial(run, True))
    pl.when(jnp.logical_not(first))(functools.partial(run, False))


def _ffn_call(x2d, gain, w_gate, w_up, w_down, final_gain=None):
    rows = x2d.shape[0]
    final_norm = final_gain is not None
    const = lambda i: (0, 0)
    hbm = pl.BlockSpec(memory_space=pl.ANY)
    in_specs = [
        pl.BlockSpec((FFN_ROWS, D_MODEL), lambda i: (i, 0)),
        pl.BlockSpec((1, D_MODEL), const),
        hbm, hbm, hbm,
    ]
    args = [x2d, gain.reshape(1, D_MODEL), w_gate, w_up, w_down]
    if final_norm:
        in_specs.append(pl.BlockSpec((1, D_MODEL), const))
        args.append(final_gain.reshape(1, D_MODEL))
    return pl.pallas_call(
        functools.partial(_ffn_body, final_norm=final_norm),
        grid=(rows // FFN_ROWS,),
        in_specs=in_specs,
        out_specs=pl.BlockSpec((FFN_ROWS, D_MODEL), lambda i: (i, 0)),
        out_shape=jax.ShapeDtypeStruct((rows, D_MODEL), F32),
        scratch_shapes=[
            pltpu.VMEM((D_MODEL, D_FF), BF16),
            pltpu.VMEM((D_MODEL, D_FF), BF16),
            pltpu.VMEM((D_FF, D_MODEL), BF16),
            pltpu.VMEM((2, 2, D_MODEL, FFN_COLS), F32),
            pltpu.VMEM((2, FFN_COLS, D_MODEL), F32),
            pltpu.SemaphoreType.DMA((2, 3)),
        ],
        compiler_params=pltpu.CompilerParams(
            dimension_semantics=("arbitrary",), vmem_limit_bytes=VMEM_LIMIT_BYTES),
        name="ffn_final" if final_norm else "ffn",
    )(*args)


def _split_hi_lo(x):
    hi = x.astype(BF16)
    lo = (x - hi.astype(F32)).astype(BF16)
    return hi, lo


def _lane_bcast(x, col):
    return jnp.broadcast_to(x[:, col:col + 1], (x.shape[0], LANES))


def _col_to_wide(col, lane_lo):
    left = jnp.where(lane_lo, col[0:CHUNK], col[CHUNK:2 * CHUNK])
    right = jnp.where(lane_lo, col[2 * CHUNK:3 * CHUNK], col[3 * CHUNK:])
    return jnp.concatenate([left, right], axis=1)


def _block_diag(wide_b, bd_mask):
    return jnp.where(bd_mask, jnp.concatenate([wide_b] * MIX_CHUNKS, axis=0), jnp.zeros((), wide_b.dtype))


def _handoff_shapes(n_seq):
    n_chain = n_seq * DN_HEADS
    return [
        pltpu.VMEM((2, n_seq * MIX_ROWS, GM_WIDTH), BF16),
        pltpu.VMEM((2, n_seq * MIX_ROWS, DN_WIDTH), F32),
        pltpu.VMEM((2, n_chain, CHUNK, MIX_ROWS), F32),
        pltpu.VMEM((2, n_chain, CHUNK, MIX_ROWS), BF16),
        pltpu.VMEM((2, n_chain, MIX_ROWS, 2 * DN_HEAD_DIM), BF16),
        pltpu.VMEM((2, n_chain, MIX_ROWS, DN_HEAD_DIM), BF16),
        pltpu.VMEM((2, n_chain, MIX_ROWS, DN_HEAD_DIM), BF16),
        pltpu.VMEM((2, n_chain, MIX_CHUNKS * SUBLANES, DN_HEAD_DIM), F32),
    ]


N_HANDOFF = 8


def _mixer_stage_a(x_ref, mg_ref, win_ref, wba_ref, lng_ref, lnb_ref, wcat_ref, bias_ref, cw_ref,
                   alog_ref, dtb_ref, cin_ref, qkv_ref, out):
    ya_ref, zg_ref, m_ref, attn_ref, rhs_ref, qd_ref, kd_ref, egl_ref = out
    n_seq = x_ref.shape[0]
    rows = MIX_ROWS
    all_rows = n_seq * rows
    n_blocks = all_rows // GM_BLOCK

    x = x_ref[...].reshape(all_rows, D_MODEL)
    hb = _rmsnorm(x, mg_ref[...]).astype(BF16)
    yield

    def project(lo, hi):
        parts = []
        for c0 in range(lo, hi, PROJ_COLS):
            parts.append(jnp.dot(hb, win_ref[:, c0:min(c0 + PROJ_COLS, hi)], **ACC))
            yield
        return jnp.concatenate(parts, axis=1)

    for b in range(n_seq):
        for s in range(3 * DN_WIDTH // LANES):
            cin_ref[b, s, 0:CONV_TAIL, :] = cin_ref[b, s, rows:rows + CONV_TAIL, :]
    for c0 in range(0, 3 * DN_WIDTH, PROJ_COLS):
        blk = jnp.dot(hb, win_ref[:, 2 * GM_WIDTH + c0:2 * GM_WIDTH + c0 + PROJ_COLS], **ACC)
        for b in range(n_seq):
            for j in range(PROJ_COLS // LANES):
                cin_ref[b, c0 // LANES + j, CONV_TAIL:, :] = blk[b * rows:(b + 1) * rows, j * LANES:(j + 1) * LANES]
        yield
    pba = jnp.dot(hb, wba_ref[...], **ACC)
    yield
    half = CONV_BLOCK // 2
    for b in range(n_seq):
        for s in range(3 * DN_WIDTH // LANES):
            src, dst = cin_ref.at[b, s], qkv_ref.at[b, s]
            w = cw_ref[:, s * LANES:(s + 1) * LANES]
            for r0 in range(0, rows, CONV_BLOCK):
                tap = lambda off: src[pl.ds(CONV_TAIL + r0 + off, half, stride=2), :]
                e0, e2, o1, om1, om3 = tap(0), tap(-2), tap(1), tap(-1), tap(-3)
                even = w[3:4] * e0 + w[2:3] * om1 + w[1:2] * e2 + w[0:1] * om3
                odd = w[3:4] * o1 + w[2:3] * e0 + w[1:2] * om1 + w[0:1] * e2
                dst[pl.ds(r0, half, stride=2), :] = jax.nn.silu(even)
                dst[pl.ds(r0 + 1, half, stride=2), :] = jax.nn.silu(odd)
    beta_all = jax.nn.sigmoid(pba)
    g_all = -jnp.exp(alog_ref[...]) * _softplus(pba + dtb_ref[...])

    v_a = jax.nn.gelu((yield from project(GM_WIDTH, 2 * GM_WIDTH)))
    zg_ref[...] = jax.nn.silu((yield from project(2 * GM_WIDTH + 3 * DN_WIDTH, 2 * GM_WIDTH + 4 * DN_WIDTH)))

    ri = lax.broadcasted_iota(jnp.int32, (rows, rows), 0)
    ci = lax.broadcasted_iota(jnp.int32, (rows, rows), 1)
    bd_mask = (ri // CHUNK) == (ci // CHUNK)
    tri_b = jnp.where(bd_mask & (ci <= ri), 1.0, 0.0).astype(BF16)
    blk_b = jnp.where(bd_mask, 1.0, 0.0).astype(BF16)
    g_hi, g_lo = _split_hi_lo(jnp.concatenate([g_all[b * rows:(b + 1) * rows] for b in range(n_seq)], axis=1))
    gc_all = jnp.dot(tri_b, g_hi, **ACC) + jnp.dot(tri_b, g_lo, **ACC)
    gl_all = jnp.dot(blk_b, g_hi, **ACC) + jnp.dot(blk_b, g_lo, **ACC)
    yield
    e_gc_all = jnp.exp(gc_all)
    e_glgc_all = jnp.exp(gl_all - gc_all)
    e_gl_all = jnp.exp(gl_all)
    gc_rows = [gc_all[:, b * LANES:(b + 1) * LANES].T for b in range(n_seq)]

    mu = jnp.mean(v_a, axis=-1, keepdims=True)
    vc = v_a - mu
    var = jnp.mean(vc * vc, axis=-1, keepdims=True)
    v_n = (vc * lax.rsqrt(var + EPS)) * lng_ref[...] + lnb_ref[...]

    wi = lax.broadcasted_iota(jnp.int32, (GM_BLOCK, 2 * GM_BLOCK), 0) // CHUNK
    wj = (lax.broadcasted_iota(jnp.int32, (GM_BLOCK, 2 * GM_BLOCK), 1) % GM_BLOCK) // CHUNK
    w_mask = wj <= wi
    lane_lo128 = lax.broadcasted_iota(jnp.int32, (GM_BLOCK, LANES), 1) < GM_GROUP_DIM
    mixed_cols = []
    for p in range(GM_GROUPS // 2):
        v_p = v_n[:, p * LANES:(p + 1) * LANES]
        rhs = []
        for blk_i in range(n_blocks):
            blk = v_p[blk_i * GM_BLOCK:(blk_i + 1) * GM_BLOCK]
            rhs.append(jnp.concatenate(
                [jnp.where(lane_lo128, blk, 0.0), jnp.where(lane_lo128, 0.0, blk)], axis=0))
        rhs = jnp.concatenate(rhs, axis=1).astype(BF16)
        w_p = jnp.where(w_mask, wcat_ref[p], 0.0).astype(BF16)
        res = jnp.dot(w_p, rhs, **ACC)
        mixed_cols.append(jnp.concatenate(
            [res[:, i * LANES:(i + 1) * LANES] for i in range(n_blocks)], axis=0))
        if p % 2 == 1:
            yield
    bias = jnp.concatenate([bias_ref[...]] * n_blocks, axis=0)
    u_a = jax.nn.gelu((yield from project(0, GM_WIDTH)))
    ya_ref[...] = (u_a *(jnp.concatenate(mixed_cols, axis=1) + bias)).astype(BF16)

    wr = lax.broadcasted_iota(jnp.int32, (CHUNK, rows), 0)
    wc = lax.broadcasted_iota(jnp.int32, (CHUNK, rows), 1) % CHUNK
    tri_w = wc <= wr
    strict_w = wc < wr
    lane_lo = lax.broadcasted_iota(jnp.int32, (CHUNK, LANES), 1) < CHUNK
    scale = DN_HEAD_DIM ** -0.5

    zeros_k = jnp.zeros((2 * CHUNK, DN_HEAD_DIM), BF16)
    for b in range(n_seq):
        rs = slice(b * rows, (b + 1) * rows)
        for h0 in range(0, DN_HEADS, 2):
            q, k, k_beta = {}, {}, {}
            for h in (h0, h0 + 1):
                chain = b * DN_HEADS + h
                col = b * LANES + DN_HEADS + h
                q_h, k_h, v_h = qkv_ref[b, h], qkv_ref[b, DN_HEADS + h], qkv_ref[b, 2 * DN_HEADS + h]
                q[h] = q_h * lax.rsqrt(jnp.sum(q_h * q_h, axis=-1, keepdims=True) + EPS) * scale
                k[h] = k_h * lax.rsqrt(jnp.sum(k_h * k_h, axis=-1, keepdims=True) + EPS)
                beta = _lane_bcast(beta_all[rs], h)
                e_gc = _lane_bcast(e_gc_all, col)
                k_beta[h] = k[h] * beta
                rhs_ref[chain] = jnp.concatenate([v_h * beta, k_beta[h] * e_gc], axis=1).astype(BF16)
                qd_ref[chain] = (q[h] * e_gc).astype(BF16)
                kd_ref[chain] = (k[h] * _lane_bcast(e_glgc_all, col)).astype(BF16)
                for ch in range(MIX_CHUNKS):
                    egl_ref[chain, ch * SUBLANES:(ch + 1) * SUBLANES, :] = _lane_bcast(
                        e_gl_all[ch * CHUNK:ch * CHUNK + SUBLANES], col)
            kk_w = {h: [] for h in (h0, h0 + 1)}
            qk_w = {h: [] for h in (h0, h0 + 1)}
            for half_i in range(rows // LANES):
                hr = slice(half_i * LANES, (half_i + 1) * LANES)
                lhs = jnp.concatenate(
                    [jnp.concatenate([k_beta[h][hr], q[h][hr]], axis=0) for h in (h0, h0 + 1)], axis=1)
                kb0, kb1 = k[h0][hr].astype(BF16), k[h0 + 1][hr].astype(BF16)
                rhs_k = jnp.concatenate([jnp.concatenate([kb0, zeros_k], axis=1),
                                         jnp.concatenate([zeros_k, kb1], axis=1)], axis=0)
                out = lax.dot_general(lhs.astype(BF16), rhs_k, _NT, **ACC)
                for i, h in enumerate((h0, h0 + 1)):
                    kk = out[0:LANES, i * LANES:(i + 1) * LANES]
                    qk = out[LANES:, i * LANES:(i + 1) * LANES]
                    kk_w[h].append(jnp.where(lane_lo, kk[0:CHUNK], kk[CHUNK:]))
                    qk_w[h].append(jnp.where(lane_lo, qk[0:CHUNK], qk[CHUNK:]))
            for h in (h0, h0 + 1):
                chain = b * DN_HEADS + h
                col = b * LANES + DN_HEADS + h
                dlog = (_col_to_wide(_lane_bcast(gc_all, col), lane_lo)
                        - jnp.broadcast_to(gc_rows[b][DN_HEADS + h:DN_HEADS + h + 1, :], (CHUNK, rows)))
                decay = jnp.exp(jnp.where(tri_w, dlog, -jnp.inf))
                m_ref[chain] = -jnp.where(strict_w, jnp.concatenate(kk_w[h], axis=1) * decay, 0.0)
                attn_ref[chain] = (jnp.concatenate(qk_w[h], axis=1) * decay).astype(BF16)
            yield


def _mixer_stage_b(x_ref, dng_ref, wout_ref, s_ref, o_ref, inp):
    ya_ref, zg_ref, m_ref, attn_ref, rhs_ref, qd_ref, kd_ref, egl_ref = inp
    n_seq = x_ref.shape[0]
    rows = MIX_ROWS
    all_rows = n_seq * rows
    chains = range(n_seq * DN_HEADS)

    ri = lax.broadcasted_iota(jnp.int32, (rows, rows), 0)
    ci = lax.broadcasted_iota(jnp.int32, (rows, rows), 1)
    bd_mask = (ri // CHUNK) == (ci // CHUNK)
    wr = lax.broadcasted_iota(jnp.int32, (CHUNK, rows), 0)
    wc = lax.broadcasted_iota(jnp.int32, (CHUNK, rows), 1) % CHUNK
    eye_w = jnp.where(wc == wr, 1.0, 0.0)

    m_w = {c: m_ref[c] for c in chains}
    inv_w = {c: eye_w + m_w[c] for c in chains}
    m_bd = {c: _block_diag(m_w[c].astype(BF16), bd_mask) for c in chains}
    for _ in range(5):
        for c in chains:
            m_w[c] = jnp.dot(m_w[c].astype(BF16), m_bd[c], **ACC)
        yield
        for c in chains:
            m_bd[c] = _block_diag(m_w[c].astype(BF16), bd_mask)
        for c in chains:
            inv_w[c] = inv_w[c] + jnp.dot(inv_w[c].astype(BF16), m_bd[c], **ACC)
        yield
    uw = {c: jnp.dot(_block_diag(inv_w[c].astype(BF16), bd_mask), rhs_ref[c], **ACC) for c in chains}
    yield

    state = {c: s_ref[c] for c in chains}
    v_new = {c: [] for c in chains}
    o_inter = {c: [] for c in chains}
    pairs = [(c, c + 1) for c in range(0, n_seq * DN_HEADS, 2)]
    zeros_s = jnp.zeros((DN_HEAD_DIM, DN_HEAD_DIM), BF16)
    zeros_v = jnp.zeros((CHUNK, DN_HEAD_DIM), BF16)

    def block_diag2(a, b, zeros):
        return jnp.concatenate([jnp.concatenate([a, zeros], axis=1), jnp.concatenate([zeros, b], axis=1)], axis=0)

    for ch in range(MIX_CHUNKS):
        cs = slice(ch * CHUNK, (ch + 1) * CHUNK)
        r = {}
        for pair in pairs:
            wq = jnp.concatenate(
                [jnp.concatenate([uw[c][cs, DN_HEAD_DIM:].astype(BF16), qd_ref[c, cs, :]], axis=0) for c in pair],
                axis=1)
            s_bd = block_diag2(state[pair[0]].astype(BF16), state[pair[1]].astype(BF16), zeros_s)
            r[pair] = jnp.dot(wq, s_bd, **ACC)
        yield
        for pair in pairs:
            vn = {}
            for i, c in enumerate(pair):
                r_c = r[pair][:, i * DN_HEAD_DIM:(i + 1) * DN_HEAD_DIM]
                vn[c] = uw[c][cs, :DN_HEAD_DIM] - r_c[:CHUNK]
                v_new[c].append(vn[c])
                o_inter[c].append(r_c[CHUNK:])
            kd = jnp.concatenate([kd_ref[c, cs, :] for c in pair], axis=0)
            vn_bd = block_diag2(vn[pair[0]].astype(BF16), vn[pair[1]].astype(BF16), zeros_v)
            upd = lax.dot_general(kd, vn_bd, _TN, **ACC)
            for i, c in enumerate(pair):
                decay_s = jnp.concatenate([egl_ref[c, ch * SUBLANES:(ch + 1) * SUBLANES, :]] * (DN_HEAD_DIM // SUBLANES), axis=0)
                state[c] = state[c] * decay_s + upd[:, i * DN_HEAD_DIM:(i + 1) * DN_HEAD_DIM]
        yield
    y_b = []
    for b in range(n_seq):
        heads = []
        for h in range(DN_HEADS):
            c = b * DN_HEADS + h
            s_ref[c] = state[c]
            o = (jnp.concatenate(o_inter[c], axis=0)
                 + jnp.dot(_block_diag(attn_ref[c], bd_mask),
                           jnp.concatenate(v_new[c], axis=0).astype(BF16), **ACC))
            heads.append(_rmsnorm(o, dng_ref[...])
                         * zg_ref[b * rows:(b + 1) * rows, h * DN_HEAD_DIM:(h + 1) * DN_HEAD_DIM])
        y_b.append(jnp.concatenate(heads, axis=1).astype(BF16))
        yield
    y = jnp.concatenate([ya_ref[...], jnp.concatenate(y_b, axis=0)], axis=1)
    for c0 in range(0, D_MODEL, PROJ_COLS):
        cols = slice(c0, c0 + PROJ_COLS)
        x = x_ref[:, :, cols].reshape(all_rows, PROJ_COLS)
        o_ref[:, :, cols] = (x + jnp.dot(y, wout_ref[:, cols], **ACC)).reshape(n_seq, rows, PROJ_COLS)
        yield


def _round_robin(*stages):
    stages = list(stages)
    while stages:
        for stage in list(stages):
            try:
                next(stage)
            except StopIteration:
                stages.remove(stage)


def _mixer_body(xa_ref, xb_ref, mg_ref, win_ref, wba_ref, lng_ref, lnb_ref, wcat_ref, bias_ref,
                cw_ref, alog_ref, dtb_ref, dng_ref, wout_ref, o_ref, s_ref, cin_ref, qkv_ref, *handoff):
    t = pl.program_id(0)
    n_tiles = pl.num_programs(0) - 1
    slot = lax.rem(t, 2)
    write_set = [ref.at[slot] for ref in handoff]
    read_set = [ref.at[1 - slot] for ref in handoff]
    stage_a = functools.partial(_mixer_stage_a, xa_ref, mg_ref, win_ref, wba_ref, lng_ref, lnb_ref, wcat_ref,
                                bias_ref, cw_ref, alog_ref, dtb_ref, cin_ref, qkv_ref)
    stage_b = functools.partial(_mixer_stage_b, xb_ref, dng_ref, wout_ref, s_ref, o_ref)

    @pl.when(t == 0)
    def _():
        s_ref[...] = jnp.zeros(s_ref.shape, F32)
        cin_ref[...] = jnp.zeros(cin_ref.shape, F32)
        _round_robin(stage_a(write_set))

    @pl.when(jnp.logical_and(t > 0, t < n_tiles))
    def _():
        _round_robin(stage_b(read_set), stage_a(write_set))

    @pl.when(t == n_tiles)
    def _():
        _round_robin(stage_b(read_set))


def _mixer_call(x, mix_norm, w_in, gm_ln_g, gm_ln_b, gm_w_s, gm_b_s, dn_conv_w, dn_a_log,
                dn_dt_bias, dn_norm, w_out):
    batch, seq, _ = x.shape
    n_tiles = seq // MIX_ROWS
    n_main = 2 * GM_WIDTH + 4 * DN_WIDTH
    w_main = w_in[:, :n_main].astype(BF16)
    w_ba = jnp.pad(w_in[:, n_main:], ((0, 0), (0, LANES - 2 * DN_HEADS))).astype(BF16)
    w_cat = gm_w_s.reshape(GM_GROUPS // 2, 2, GM_BLOCK, GM_BLOCK).transpose(0, 2, 1, 3)
    w_cat = w_cat.reshape(GM_GROUPS // 2, GM_BLOCK, 2 * GM_BLOCK)
    bias = jnp.repeat(gm_b_s.T, GM_GROUP_DIM, axis=1)
    pad_a = lambda a: jnp.pad(a.reshape(1, DN_HEADS), ((0, 0), (DN_HEADS, LANES - 2 * DN_HEADS)))
    const2 = lambda t: (0, 0)
    tile_a = lambda t: (0, jnp.minimum(t, n_tiles - 1), 0)
    tile_b = lambda t: (0, jnp.maximum(t - 1, 0), 0)
    in_specs = [
        pl.BlockSpec((batch, MIX_ROWS, D_MODEL), tile_a),
        pl.BlockSpec((batch, MIX_ROWS, D_MODEL), tile_b),
        pl.BlockSpec((1, D_MODEL), const2),
        pl.BlockSpec((D_MODEL, n_main), const2),
        pl.BlockSpec((D_MODEL, LANES), const2),
        pl.BlockSpec((1, GM_WIDTH), const2),
        pl.BlockSpec((1, GM_WIDTH), const2),
        pl.BlockSpec((GM_GROUPS // 2, GM_BLOCK, 2 * GM_BLOCK), lambda t: (0, 0, 0)),
        pl.BlockSpec((GM_BLOCK, GM_WIDTH), const2),
        pl.BlockSpec((DN_CONV, 3 * DN_WIDTH), const2),
        pl.BlockSpec((1, LANES), const2),
        pl.BlockSpec((1, LANES), const2),
        pl.BlockSpec((1, DN_HEAD_DIM), const2),
        pl.BlockSpec((D_MODEL, D_MODEL), const2),
    ]
    return pl.pallas_call(
        _mixer_body,
        grid=(n_tiles + 1,),
        in_specs=in_specs,
        out_specs=pl.BlockSpec((batch, MIX_ROWS, D_MODEL), tile_b),
        out_shape=jax.ShapeDtypeStruct((batch, seq, D_MODEL), F32),
        scratch_shapes=[
            pltpu.VMEM((batch * DN_HEADS, DN_HEAD_DIM, DN_HEAD_DIM), F32),
            pltpu.VMEM((batch, 3 * DN_HEADS, CONV_TAIL + MIX_ROWS, LANES), F32),
            pltpu.VMEM((batch, 3 * DN_HEADS, MIX_ROWS, LANES), F32),
        ] + _handoff_shapes(batch),
        compiler_params=pltpu.CompilerParams(
            dimension_semantics=("arbitrary",), vmem_limit_bytes=VMEM_LIMIT_BYTES),
        name="mixer",
    )(x, x, mix_norm.reshape(1, D_MODEL), w_main, w_ba, gm_ln_g.reshape(1, GM_WIDTH),
      gm_ln_b.reshape(1, GM_WIDTH), w_cat, bias, dn_conv_w, pad_a(dn_a_log), pad_a(dn_dt_bias),
      dn_norm.reshape(1, DN_HEAD_DIM), w_out.astype(BF16))


def kernel(x, ffn1_norm, ffn1_w_gate, ffn1_w_up, ffn1_w_down, mix_norm, w_in, gm_ln_g, gm_ln_b,
           gm_w_s, gm_b_s, dn_conv_w, dn_a_log, dn_dt_bias, dn_norm, w_out, ffn2_norm,
           ffn2_w_gate, ffn2_w_up, ffn2_w_down, final_norm):
    batch, seq, d = x.shape
    depth = ffn1_norm.shape[0]
    for l in range(depth):
        x = _ffn_call(x.reshape(batch * seq, d), ffn1_norm[l], ffn1_w_gate[l], ffn1_w_up[l],
                      ffn1_w_down[l]).reshape(batch, seq, d)
        x = _mixer_call(x, mix_norm[l], w_in[l], gm_ln_g[l], gm_ln_b[l], gm_w_s[l], gm_b_s[l],
                        dn_conv_w[l], dn_a_log[l], dn_dt_bias[l], dn_norm[l], w_out[l])
        last = l == depth - 1
        x = _ffn_call(x.reshape(batch * seq, d), ffn2_norm[l], ffn2_w_gate[l], ffn2_w_up[l],
                      ffn2_w_down[l], final_norm if last else None).reshape(batch, seq, d)
    return x
```

```python
import functools

import jax
import jax.numpy as jnp
from jax import lax
from jax.experimental import pallas as pl
from jax.experimental.pallas import tpu as pltpu

F32 = jnp.float32
BF16 = jnp.bfloat16
ACC = dict(preferred_element_type=jnp.float32)

D_MODEL = 1024
D_FF = 2816
CHUNK = 64
GM_GROUPS = 8
GM_GROUP_DIM = 64
GM_WIDTH = GM_GROUPS * GM_GROUP_DIM
GM_BLOCK = 128
DN_HEADS = 4
DN_HEAD_DIM = 128
DN_WIDTH = DN_HEADS * DN_HEAD_DIM
DN_CONV = 4
EPS = 1e-6

LANES = 128
SUBLANES = 8
VMEM_LIMIT_BYTES = 60000 * 1024

FFN_ROWS = 512
FFN_COLS = 256
MIX_ROWS = 256
MIX_CHUNKS = MIX_ROWS // CHUNK
CONV_TAIL = SUBLANES
PROJ_COLS = 256
CONV_BLOCK = 64

_NT = (((1,), (1,)), ((), ()))
_TN = (((0,), (0,)), ((), ()))


def _rmsnorm(x, g):
    return (x * lax.rsqrt(jnp.mean(x * x, axis=-1, keepdims=True) + EPS)) * g


def _softplus(x):
    return jnp.maximum(x, 0.0) + jnp.log1p(jnp.exp(-jnp.abs(x)))


def _ffn_body(x_ref, g_ref, wg_hbm, wu_hbm, wd_hbm, *rest, final_norm):
    if final_norm:
        fg_ref, o_ref, wg_ref, wu_ref, wd_ref, stage_gu, stage_d, sem = rest
    else:
        o_ref, wg_ref, wu_ref, wd_ref, stage_gu, stage_d, sem = rest
    n_chunks = D_FF // FFN_COLS

    def chunk_copies(j, slot):
        cols = pl.ds(j * FFN_COLS, FFN_COLS)
        return (pltpu.make_async_copy(wg_hbm.at[:, cols], stage_gu.at[slot, 0], sem.at[slot, 0]),
                pltpu.make_async_copy(wu_hbm.at[:, cols], stage_gu.at[slot, 1], sem.at[slot, 1]),
                pltpu.make_async_copy(wd_hbm.at[cols, :], stage_d.at[slot], sem.at[slot, 2]))

    def run(load_weights):
        x = x_ref[...]
        hb = _rmsnorm(x, g_ref[...]).astype(BF16)
        acc = jnp.zeros(x.shape, F32)
        if load_weights:
            for cp in chunk_copies(0, 0):
                cp.start()
        for j in range(n_chunks):
            cols = slice(j * FFN_COLS, (j + 1) * FFN_COLS)
            if load_weights:
                slot = j % 2
                if j + 1 < n_chunks:
                    for cp in chunk_copies(j + 1, 1 - slot):
                        cp.start()
                for cp in chunk_copies(j, slot):
                    cp.wait()
                wg_ref[:, cols] = stage_gu[slot, 0].astype(BF16)
                wu_ref[:, cols] = stage_gu[slot, 1].astype(BF16)
                wd_ref[cols, :] = stage_d[slot].astype(BF16)
            gate = jnp.dot(hb, wg_ref[:, cols], **ACC)
            up = jnp.dot(hb, wu_ref[:, cols], **ACC)
            act = (jax.nn.silu(gate) * up).astype(BF16)
            acc = acc + jnp.dot(act, wd_ref[cols, :], **ACC)
        y = x + 0.5 * acc
        if final_norm:
            y = _rmsnorm(y, fg_ref[...])
        o_ref[...] = y

    first = pl.program_id(0) == 0
    pl.when(first)(functools.partial(run, True))
    pl.when(jnp.logical_not(first))(functools.partial(run, False))


def _ffn_call(x2d, gain, w_gate, w_up, w_down, final_gain=None):
    rows = x2d.shape[0]
    final_norm = final_gain is not None
    const = lambda i: (0, 0)
    hbm = pl.BlockSpec(memory_space=pl.ANY)
    in_specs = [
        pl.BlockSpec((FFN_ROWS, D_MODEL), lambda i: (i, 0)),
        pl.BlockSpec((1, D_MODEL), const),
        hbm, hbm, hbm,
    ]
    args = [x2d, gain.reshape(1, D_MODEL), w_gate, w_up, w_down]
    if final_norm:
        in_specs.append(pl.BlockSpec((1, D_MODEL), const))
        args.append(final_gain.reshape(1, D_MODEL))
    return pl.pallas_call(
        functools.partial(_ffn_body, final_norm=final_norm),
        grid=(rows // FFN_ROWS,),
        in_specs=in_specs,
        out_specs=pl.BlockSpec((FFN_ROWS, D_MODEL), lambda i: (i, 0)),
        out_shape=jax.ShapeDtypeStruct((rows, D_MODEL), F32),
        scratch_shapes=[
            pltpu.VMEM((D_MODEL, D_FF), BF16),
            pltpu.VMEM((D_MODEL, D_FF), BF16),
            pltpu.VMEM((D_FF, D_MODEL), BF16),
            pltpu.VMEM((2, 2, D_MODEL, FFN_COLS), F32),
            pltpu.VMEM((2, FFN_COLS, D_MODEL), F32),
            pltpu.SemaphoreType.DMA((2, 3)),
        ],
        compiler_params=pltpu.CompilerParams(
            dimension_semantics=("arbitrary",), vmem_limit_bytes=VMEM_LIMIT_BYTES),
        name="ffn_final" if final_norm else "ffn",
    )(*args)


def _split_hi_lo(x):
    hi = x.astype(BF16)
    lo = (x - hi.astype(F32)).astype(BF16)
    return hi, lo


def _lane_bcast(x, col):
    return jnp.broadcast_to(x[:, col:col + 1], (x.shape[0], LANES))


def _col_to_wide(col, lane_lo):
    left = jnp.where(lane_lo, col[0:CHUNK], col[CHUNK:2 * CHUNK])
    right = jnp.where(lane_lo, col[2 * CHUNK:3 * CHUNK], col[3 * CHUNK:])
    return jnp.concatenate([left, right], axis=1)


def _block_diag(wide, bd_mask):
    return jnp.where(bd_mask, jnp.concatenate([wide.astype(F32)] * MIX_CHUNKS, axis=0), 0.0).astype(BF16)


def _handoff_shapes(n_seq):
    n_chain = n_seq * DN_HEADS
    return [
        pltpu.VMEM((2, n_seq * MIX_ROWS, GM_WIDTH), BF16),
        pltpu.VMEM((2, n_seq * MIX_ROWS, DN_WIDTH), F32),
        pltpu.VMEM((2, n_chain, CHUNK, MIX_ROWS), F32),
        pltpu.VMEM((2, n_chain, CHUNK, MIX_ROWS), BF16),
        pltpu.VMEM((2, n_chain, MIX_ROWS, 2 * DN_HEAD_DIM), BF16),
        pltpu.VMEM((2, n_chain, MIX_ROWS, DN_HEAD_DIM), BF16),
        pltpu.VMEM((2, n_chain, MIX_ROWS, DN_HEAD_DIM), BF16),
        pltpu.VMEM((2, n_chain, MIX_CHUNKS * SUBLANES, DN_HEAD_DIM), F32),
    ]


N_HANDOFF = 8


def _mixer_stage_a(x_ref, mg_ref, win_ref, wba_ref, lng_ref, lnb_ref, wcat_ref, bias_ref, cw_ref,
                   alog_ref, dtb_ref, cin_ref, qkv_ref, out):
    ya_ref, zg_ref, m_ref, attn_ref, rhs_ref, qd_ref, kd_ref, egl_ref = out
    n_seq = x_ref.shape[0]
    rows = MIX_ROWS
    all_rows = n_seq * rows
    n_blocks = all_rows // GM_BLOCK

    x = x_ref[...].reshape(all_rows, D_MODEL)
    hb = _rmsnorm(x, mg_ref[...]).astype(BF16)
    yield

    def project(lo, hi):
        parts = []
        for c0 in range(lo, hi, PROJ_COLS):
            parts.append(jnp.dot(hb, win_ref[:, c0:min(c0 + PROJ_COLS, hi)], **ACC))
            yield
        return jnp.concatenate(parts, axis=1)

    for b in range(n_seq):
        for s in range(3 * DN_WIDTH // LANES):
            cin_ref[b, s, 0:CONV_TAIL, :] = cin_ref[b, s, rows:rows + CONV_TAIL, :]
    for c0 in range(0, 3 * DN_WIDTH, PROJ_COLS):
        blk = jnp.dot(hb, win_ref[:, 2 * GM_WIDTH + c0:2 * GM_WIDTH + c0 + PROJ_COLS], **ACC)
        for b in range(n_seq):
            for j in range(PROJ_COLS // LANES):
                cin_ref[b, c0 // LANES + j, CONV_TAIL:, :] = blk[b * rows:(b + 1) * rows, j * LANES:(j + 1) * LANES]
        yield
    pba = jnp.dot(hb, wba_ref[...], **ACC)
    yield
    half = CONV_BLOCK // 2
    for b in range(n_seq):
        for s in range(3 * DN_WIDTH // LANES):
            src, dst = cin_ref.at[b, s], qkv_ref.at[b, s]
            w = cw_ref[:, s * LANES:(s + 1) * LANES]
            for r0 in range(0, rows, CONV_BLOCK):
                tap = lambda off: src[pl.ds(CONV_TAIL + r0 + off, half, stride=2), :]
                e0, e2, o1, om1, om3 = tap(0), tap(-2), tap(1), tap(-1), tap(-3)
                even = w[3:4] * e0 + w[2:3] * om1 + w[1:2] * e2 + w[0:1] * om3
                odd = w[3:4] * o1 + w[2:3] * e0 + w[1:2] * om1 + w[0:1] * e2
                dst[pl.ds(r0, half, stride=2), :] = jax.nn.silu(even)
                dst[pl.ds(r0 + 1, half, stride=2), :] = jax.nn.silu(odd)
    beta_all = jax.nn.sigmoid(pba)
    g_all = -jnp.exp(alog_ref[...]) * _softplus(pba + dtb_ref[...])

    v_a = jax.nn.gelu((yield from project(GM_WIDTH, 2 * GM_WIDTH)))
    zg_ref[...] = jax.nn.silu((yield from project(2 * GM_WIDTH + 3 * DN_WIDTH, 2 * GM_WIDTH + 4 * DN_WIDTH)))

    ri = lax.broadcasted_iota(jnp.int32, (rows, rows), 0)
    ci = lax.broadcasted_iota(jnp.int32, (rows, rows), 1)
    bd_mask = (ri // CHUNK) == (ci // CHUNK)
    tri_b = jnp.where(bd_mask & (ci <= ri), 1.0, 0.0).astype(BF16)
    blk_b = jnp.where(bd_mask, 1.0, 0.0).astype(BF16)
    g_hi, g_lo = _split_hi_lo(jnp.concatenate([g_all[b * rows:(b + 1) * rows] for b in range(n_seq)], axis=1))
    gc_all = jnp.dot(tri_b, g_hi, **ACC) + jnp.dot(tri_b, g_lo, **ACC)
    gl_all = jnp.dot(blk_b, g_hi, **ACC) + jnp.dot(blk_b, g_lo, **ACC)
    yield
    e_gc_all = jnp.exp(gc_all)
    e_glgc_all = jnp.exp(gl_all - gc_all)
    e_gl_all = jnp.exp(gl_all)
    gc_rows = [gc_all[:, b * LANES:(b + 1) * LANES].T for b in range(n_seq)]

    mu = jnp.mean(v_a, axis=-1, keepdims=True)
    vc = v_a - mu
    var = jnp.mean(vc * vc, axis=-1, keepdims=True)
    v_n = (vc * lax.rsqrt(var + EPS)) * lng_ref[...] + lnb_ref[...]

    wi = lax.broadcasted_iota(jnp.int32, (GM_BLOCK, 2 * GM_BLOCK), 0) // CHUNK
    wj = (lax.broadcasted_iota(jnp.int32, (GM_BLOCK, 2 * GM_BLOCK), 1) % GM_BLOCK) // CHUNK
    w_mask = wj <= wi
    lane_lo128 = lax.broadcasted_iota(jnp.int32, (GM_BLOCK, LANES), 1) < GM_GROUP_DIM
    mixed_cols = []
    for p in range(GM_GROUPS // 2):
        v_p = v_n[:, p * LANES:(p + 1) * LANES]
        rhs = []
        for blk_i in range(n_blocks):
            blk = v_p[blk_i * GM_BLOCK:(blk_i + 1) * GM_BLOCK]
            rhs.append(jnp.concatenate(
                [jnp.where(lane_lo128, blk, 0.0), jnp.where(lane_lo128, 0.0, blk)], axis=0))
        rhs = jnp.concatenate(rhs, axis=1).astype(BF16)
        w_p = jnp.where(w_mask, wcat_ref[p], 0.0).astype(BF16)
        res = jnp.dot(w_p, rhs, **ACC)
        mixed_cols.append(jnp.concatenate(
            [res[:, i * LANES:(i + 1) * LANES] for i in range(n_blocks)], axis=0))
        if p % 2 == 1:
            yield
    bias = jnp.concatenate([bias_ref[...]] * n_blocks, axis=0)
    u_a = jax.nn.gelu((yield from project(0, GM_WIDTH)))
    ya_ref[...] = (u_a * (jnp.concatenate(mixed_cols, axis=1) + bias)).astype(BF16)

    wr = lax.broadcasted_iota(jnp.int32, (CHUNK, rows), 0)
    wc = lax.broadcasted_iota(jnp.int32, (CHUNK, rows), 1) % CHUNK
    tri_w = wc <= wr
    strict_w = wc < wr
    lane_lo = lax.broadcasted_iota(jnp.int32, (CHUNK, LANES), 1) < CHUNK
    scale = DN_HEAD_DIM ** -0.5

    zeros_k = jnp.zeros((2 * CHUNK, DN_HEAD_DIM), BF16)
    for b in range(n_seq):
        rs = slice(b * rows, (b + 1) * rows)
        for h0 in range(0, DN_HEADS, 2):
            q, k, k_beta = {}, {}, {}
            for h in (h0, h0 + 1):
                chain = b * DN_HEADS + h
                col = b * LANES + DN_HEADS + h
                q_h, k_h, v_h = qkv_ref[b, h], qkv_ref[b, DN_HEADS + h], qkv_ref[b, 2 * DN_HEADS + h]
                q[h] = q_h * lax.rsqrt(jnp.sum(q_h * q_h, axis=-1, keepdims=True) + EPS) * scale
                k[h] = k_h * lax.rsqrt(jnp.sum(k_h * k_h, axis=-1, keepdims=True) + EPS)
                beta = _lane_bcast(beta_all[rs], h)
                e_gc = _lane_bcast(e_gc_all, col)
                k_beta[h] = k[h] * beta
                rhs_ref[chain] = jnp.concatenate([v_h * beta, k_beta[h] * e_gc], axis=1).astype(BF16)
                qd_ref[chain] = (q[h] * e_gc).astype(BF16)
                kd_ref[chain] = (k[h] * _lane_bcast(e_glgc_all, col)).astype(BF16)
                for ch in range(MIX_CHUNKS):
                    egl_ref[chain, ch * SUBLANES:(ch + 1) * SUBLANES, :] = _lane_bcast(
                        e_gl_all[ch * CHUNK:ch * CHUNK + SUBLANES], col)
            kk_w = {h: [] for h in (h0, h0 + 1)}
            qk_w = {h: [] for h in (h0, h0 + 1)}
            for half_i in range(rows // LANES):
                hr = slice(half_i * LANES, (half_i + 1) * LANES)
                lhs = jnp.concatenate(
                    [jnp.concatenate([k_beta[h][hr], q[h][hr]], axis=0) for h in (h0, h0 + 1)], axis=1)
                kb0, kb1 = k[h0][hr].astype(BF16), k[h0 + 1][hr].astype(BF16)
                rhs_k = jnp.concatenate([jnp.concatenate([kb0, zeros_k], axis=1),
                                         jnp.concatenate([zeros_k, kb1], axis=1)], axis=0)
                out = lax.dot_general(lhs.astype(BF16), rhs_k, _NT, **ACC)
                for i, h in enumerate((h0, h0 + 1)):
                    kk = out[0:LANES, i * LANES:(i + 1) * LANES]
                    qk = out[LANES:, i * LANES:(i + 1) * LANES]
                    kk_w[h].append(jnp.where(lane_lo, kk[0:CHUNK], kk[CHUNK:]))
                    qk_w[h].append(jnp.where(lane_lo, qk[0:CHUNK], qk[CHUNK:]))
            for h in (h0, h0 + 1):
                chain = b * DN_HEADS + h
                col = b * LANES + DN_HEADS + h
                dlog = (_col_to_wide(_lane_bcast(gc_all, col), lane_lo)
                        - jnp.broadcast_to(gc_rows[b][DN_HEADS + h:DN_HEADS + h + 1, :], (CHUNK, rows)))
                decay = jnp.exp(jnp.where(tri_w, dlog, -jnp.inf))
                m_ref[chain] = -jnp.where(strict_w, jnp.concatenate(kk_w[h], axis=1) * decay, 0.0)
                attn_ref[chain] = (jnp.concatenate(qk_w[h], axis=1) * decay).astype(BF16)
            yield


def _mixer_stage_b(x_ref, dng_ref, wout_ref, s_ref, o_ref, inp):
    ya_ref, zg_ref, m_ref, attn_ref, rhs_ref, qd_ref, kd_ref, egl_ref = inp
    n_seq = x_ref.shape[0]
    rows = MIX_ROWS
    all_rows = n_seq * rows
    chains = range(n_seq * DN_HEADS)

    ri = lax.broadcasted_iota(jnp.int32, (rows, rows), 0)
    ci = lax.broadcasted_iota(jnp.int32, (rows, rows), 1)
    bd_mask = (ri // CHUNK) == (ci // CHUNK)
    wr = lax.broadcasted_iota(jnp.int32, (CHUNK, rows), 0)
    wc = lax.broadcasted_iota(jnp.int32, (CHUNK, rows), 1) % CHUNK
    eye_w = jnp.where(wc == wr, 1.0, 0.0)

    m_w = {c: m_ref[c] for c in chains}
    inv_w = {c: eye_w + m_w[c] for c in chains}
    m_bd = {c: _block_diag(m_w[c], bd_mask) for c in chains}
    for _ in range(5):
        for c in chains:
            m_w[c] = jnp.dot(m_w[c].astype(BF16), m_bd[c], **ACC)
        yield
        for c in chains:
            m_bd[c] = _block_diag(m_w[c], bd_mask)
        for c in chains:
            inv_w[c] = inv_w[c] + jnp.dot(inv_w[c].astype(BF16), m_bd[c], **ACC)
        yield
    uw = {c: jnp.dot(_block_diag(inv_w[c], bd_mask), rhs_ref[c], **ACC) for c in chains}
    yield

    state = {c: s_ref[c] for c in chains}
    v_new = {c: [] for c in chains}
    o_inter = {c: [] for c in chains}
    pairs = [(c, c + 1) for c in range(0, n_seq * DN_HEADS, 2)]
    zeros_s = jnp.zeros((DN_HEAD_DIM, DN_HEAD_DIM), BF16)
    zeros_v = jnp.zeros((CHUNK, DN_HEAD_DIM), BF16)

    def block_diag2(a, b, zeros):
        return jnp.concatenate([jnp.concatenate([a, zeros], axis=1), jnp.concatenate([zeros, b], axis=1)], axis=0)

    for ch in range(MIX_CHUNKS):
        cs = slice(ch * CHUNK, (ch + 1) * CHUNK)
        r = {}
        for pair in pairs:
            wq = jnp.concatenate(
                [jnp.concatenate([uw[c][cs, DN_HEAD_DIM:].astype(BF16), qd_ref[c, cs, :]], axis=0) for c in pair],
                axis=1)
            s_bd = block_diag2(state[pair[0]].astype(BF16), state[pair[1]].astype(BF16), zeros_s)
            r[pair] = jnp.dot(wq, s_bd, **ACC)
        yield
        for pair in pairs:
            vn = {}
            for i, c in enumerate(pair):
                r_c = r[pair][:, i * DN_HEAD_DIM:(i + 1) * DN_HEAD_DIM]
                vn[c] = uw[c][cs, :DN_HEAD_DIM] - r_c[:CHUNK]
                v_new[c].append(vn[c])
                o_inter[c].append(r_c[CHUNK:])
            kd = jnp.concatenate([kd_ref[c, cs, :] for c in pair], axis=0)
            vn_bd = block_diag2(vn[pair[0]].astype(BF16), vn[pair[1]].astype(BF16), zeros_v)
            upd = lax.dot_general(kd, vn_bd, _TN, **ACC)
            for i, c in enumerate(pair):
                decay_s = jnp.concatenate([egl_ref[c, ch * SUBLANES:(ch + 1) * SUBLANES, :]] * (DN_HEAD_DIM // SUBLANES), axis=0)
                state[c] = state[c] * decay_s + upd[:, i * DN_HEAD_DIM:(i + 1) * DN_HEAD_DIM]
        yield
    y_b = []
    for b in range(n_seq):
        heads = []
        for h in range(DN_HEADS):
            c = b * DN_HEADS + h
            s_ref[c] = state[c]
            o = (jnp.concatenate(o_inter[c], axis=0)
                 + jnp.dot(_block_diag(attn_ref[c], bd_mask),
                           jnp.concatenate(v_new[c], axis=0).astype(BF16), **ACC))
            heads.append(_rmsnorm(o, dng_ref[...])
                         * zg_ref[b * rows:(b + 1) * rows, h * DN_HEAD_DIM:(h + 1) * DN_HEAD_DIM])
        y_b.append(jnp.concatenate(heads, axis=1).astype(BF16))
        yield
    y = jnp.concatenate([ya_ref[...], jnp.concatenate(y_b, axis=0)], axis=1)
    for c0 in range(0, D_MODEL, PROJ_COLS):
        cols = slice(c0, c0 + PROJ_COLS)
        x = x_ref[:, :, cols].reshape(all_rows, PROJ_COLS)
        o_ref[:, :, cols] = (x + jnp.dot(y, wout_ref[:, cols], **ACC)).reshape(n_seq, rows, PROJ_COLS)
        yield


def _round_robin(*stages):
    stages = list(stages)
    while stages:
        for stage in list(stages):
            try:
                next(stage)
            except StopIteration:
                stages.remove(stage)


def _mixer_body(xa_ref, xb_ref, mg_ref, win_ref, wba_ref, lng_ref, lnb_ref, wcat_ref, bias_ref,
                cw_ref, alog_ref, dtb_ref, dng_ref, wout_ref, o_ref, s_ref, cin_ref, qkv_ref, *handoff):
    t = pl.program_id(0)
    n_tiles = pl.num_programs(0) - 1
    slot = lax.rem(t, 2)
    write_set = [ref.at[slot] for ref in handoff]
    read_set = [ref.at[1 - slot] for ref in handoff]
    stage_a = functools.partial(_mixer_stage_a, xa_ref, mg_ref, win_ref, wba_ref, lng_ref, lnb_ref, wcat_ref,
                                bias_ref, cw_ref, alog_ref, dtb_ref, cin_ref, qkv_ref)
    stage_b = functools.partial(_mixer_stage_b, xb_ref, dng_ref, wout_ref, s_ref, o_ref)

    @pl.when(t == 0)
    def _():
        s_ref[...] = jnp.zeros(s_ref.shape, F32)
        cin_ref[...] = jnp.zeros(cin_ref.shape, F32)
        _round_robin(stage_a(write_set))

    @pl.when(jnp.logical_and(t > 0, t < n_tiles))
    def _():
        _round_robin(stage_b(read_set), stage_a(write_set))

    @pl.when(t == n_tiles)
    def _():
        _round_robin(stage_b(read_set))


def _mixer_call(x, mix_norm, w_in, gm_ln_g, gm_ln_b, gm_w_s, gm_b_s, dn_conv_w, dn_a_log,
                dn_dt_bias, dn_norm, w_out):
    batch, seq, _ = x.shape
    n_tiles = seq // MIX_ROWS
    n_main = 2 * GM_WIDTH + 4 * DN_WIDTH
    w_main = w_in[:, :n_main].astype(BF16)
    w_ba = jnp.pad(w_in[:, n_main:], ((0, 0), (0, LANES - 2 * DN_HEADS))).astype(BF16)
    w_cat = gm_w_s.reshape(GM_GROUPS // 2, 2, GM_BLOCK, GM_BLOCK).transpose(0, 2, 1, 3)
    w_cat = w_cat.reshape(GM_GROUPS // 2, GM_BLOCK, 2 * GM_BLOCK)
    bias = jnp.repeat(gm_b_s.T, GM_GROUP_DIM, axis=1)
    pad_a = lambda a: jnp.pad(a.reshape(1, DN_HEADS), ((0, 0), (DN_HEADS, LANES - 2 * DN_HEADS)))
    const2 = lambda t: (0, 0)
    tile_a = lambda t: (0, jnp.minimum(t, n_tiles - 1), 0)
    tile_b = lambda t: (0, jnp.maximum(t - 1, 0), 0)
    in_specs = [
        pl.BlockSpec((batch, MIX_ROWS, D_MODEL), tile_a),
        pl.BlockSpec((batch, MIX_ROWS, D_MODEL), tile_b),
        pl.BlockSpec((1, D_MODEL), const2),
        pl.BlockSpec((D_MODEL, n_main), const2),
        pl.BlockSpec((D_MODEL, LANES), const2),
        pl.BlockSpec((1, GM_WIDTH), const2),
        pl.BlockSpec((1, GM_WIDTH), const2),
        pl.BlockSpec((GM_GROUPS // 2, GM_BLOCK, 2 * GM_BLOCK), lambda t: (0, 0, 0)),
        pl.BlockSpec((GM_BLOCK, GM_WIDTH), const2),
        pl.BlockSpec((DN_CONV, 3 * DN_WIDTH), const2),
        pl.BlockSpec((1, LANES), const2),
        pl.BlockSpec((1, LANES), const2),
        pl.BlockSpec((1, DN_HEAD_DIM), const2),
        pl.BlockSpec((D_MODEL, D_MODEL), const2),
    ]
    return pl.pallas_call(
        _mixer_body,
        grid=(n_tiles + 1,),
        in_specs=in_specs,
        out_specs=pl.BlockSpec((batch, MIX_ROWS, D_MODEL), tile_b),
        out_shape=jax.ShapeDtypeStruct((batch, seq, D_MODEL), F32),
        scratch_shapes=[
            pltpu.VMEM((batch * DN_HEADS, DN_HEAD_DIM, DN_HEAD_DIM), F32),
            pltpu.VMEM((batch, 3 * DN_HEADS, CONV_TAIL + MIX_ROWS, LANES), F32),
            pltpu.VMEM((batch, 3 * DN_HEADS, MIX_ROWS, LANES), F32),
        ] + _handoff_shapes(batch),
        compiler_params=pltpu.CompilerParams(
            dimension_semantics=("arbitrary",), vmem_limit_bytes=VMEM_LIMIT_BYTES),
        name="mixer",
    )(x, x, mix_norm.reshape(1, D_MODEL), w_main, w_ba, gm_ln_g.reshape(1, GM_WIDTH),
      gm_ln_b.reshape(1, GM_WIDTH), w_cat, bias, dn_conv_w, pad_a(dn_a_log), pad_a(dn_dt_bias),
      dn_norm.reshape(1, DN_HEAD_DIM), w_out.astype(BF16))


def kernel(x, ffn1_norm, ffn1_w_gate, ffn1_w_up, ffn1_w_down, mix_norm, w_in, gm_ln_g, gm_ln_b,
           gm_w_s, gm_b_s, dn_conv_w, dn_a_log, dn_dt_bias, dn_norm, w_out, ffn2_norm,
           ffn2_w_gate, ffn2_w_up, ffn2_w_down, final_norm):
    batch, seq, d = x.shape
    depth = ffn1_norm.shape[0]
    for l in range(depth):
        x = _ffn_call(x.reshape(batch * seq, d), ffn1_norm[l], ffn1_w_gate[l], ffn1_w_up[l],
                      ffn1_w_down[l]).reshape(batch, seq, d)
        x = _mixer_call(x, mix_norm[l], w_in[l], gm_ln_g[l], gm_ln_b[l], gm_w_s[l], gm_b_s[l],
                        dn_conv_w[l], dn_a_log[l], dn_dt_bias[l], dn_norm[l], w_out[l])
        last = l == depth - 1
        x = _ffn_call(x.reshape(batch * seq, d), ffn2_norm[l], ffn2_w_gate[l], ffn2_w_up[l],
                      ffn2_w_down[l], final_norm if last else None).reshape(batch, seq, d)
    return x
```

```python
import functools

import jax
import jax.numpy as jnp
from jax import lax
from jax.experimental import pallas as pl
from jax.experimental.pallas import tpu as pltpu

F32 = jnp.float32
BF16 = jnp.bfloat16
ACC = dict(preferred_element_type=jnp.float32)

D_MODEL = 1024
D_FF = 2816
CHUNK = 64
GM_GROUPS = 8
GM_GROUP_DIM = 64
GM_WIDTH = GM_GROUPS * GM_GROUP_DIM
GM_BLOCK = 128
DN_HEADS = 4
DN_HEAD_DIM = 128
DN_WIDTH = DN_HEADS * DN_HEAD_DIM
DN_CONV = 4
EPS = 1e-6

LANES = 128
SUBLANES = 8
VMEM_LIMIT_BYTES = 60000 * 1024

FFN_ROWS = 512
FFN_COLS = 256
MIX_ROWS = 256
MIX_CHUNKS = MIX_ROWS // CHUNK
CONV_TAIL = SUBLANES
PROJ_COLS = 256
CONV_BLOCK = 64

_NT = (((1,), (1,)), ((), ()))
_TN = (((0,), (0,)), ((), ()))


def _rmsnorm(x, g):
    return (x * lax.rsqrt(jnp.mean(x * x, axis=-1, keepdims=True) + EPS)) * g


def _softplus(x):
    return jnp.maximum(x, 0.0) + jnp.log1p(jnp.exp(-jnp.abs(x)))


def _ffn_body(x_ref, g_ref, wg_hbm, wu_hbm, wd_hbm, *rest, final_norm):
    if final_norm:
        fg_ref, o_ref, wg_ref, wu_ref, wd_ref, stage_gu, stage_d, sem = rest
    else:
        o_ref, wg_ref, wu_ref, wd_ref, stage_gu, stage_d, sem = rest
    n_chunks = D_FF // FFN_COLS

    def chunk_copies(j, slot):
        cols = pl.ds(j * FFN_COLS, FFN_COLS)
        return (pltpu.make_async_copy(wg_hbm.at[:, cols], stage_gu.at[slot, 0], sem.at[slot, 0]),
                pltpu.make_async_copy(wu_hbm.at[:, cols], stage_gu.at[slot, 1], sem.at[slot, 1]),
                pltpu.make_async_copy(wd_hbm.at[cols, :], stage_d.at[slot], sem.at[slot, 2]))

    def run(load_weights):
        x = x_ref[...]
        hb = _rmsnorm(x, g_ref[...]).astype(BF16)
        half_d = D_MODEL // 2
        acc = jnp.zeros((x.shape[0], half_d), F32)
        acts = []
        if load_weights:
            for cp in chunk_copies(0, 0):
                cp.start()
        for j in range(n_chunks):
            cols = slice(j * FFN_COLS, (j + 1) * FFN_COLS)
            if load_weights:
                slot = j % 2
                if j + 1 < n_chunks:
                    for cp in chunk_copies(j + 1, 1 - slot):
                        cp.start()
                for cp in chunk_copies(j, slot):
                    cp.wait()
                wg_ref[:, cols] = stage_gu[slot, 0].astype(BF16)
                wu_ref[:, cols] = stage_gu[slot, 1].astype(BF16)
                wd_ref[cols, :] = stage_d[slot].astype(BF16)
            gate = jnp.dot(hb, wg_ref[:, cols], **ACC)
            up = jnp.dot(hb, wu_ref[:, cols], **ACC)
            act = (jax.nn.silu(gate) * up).astype(BF16)
            acts.append(act)
            acc = acc + jnp.dot(act, wd_ref[cols, :half_d], **ACC)
        acc2 = jnp.zeros((x.shape[0], half_d), F32)
        for j in range(n_chunks):
            acc2 = acc2 + jnp.dot(acts[j], wd_ref[j * FFN_COLS:(j + 1) * FFN_COLS, half_d:], **ACC)
        y = x + 0.5 * jnp.concatenate([acc, acc2], axis=1)
        if final_norm:
            y = _rmsnorm(y, fg_ref[...])
        o_ref[...] = y

    first = pl.program_id(0) == 0
    pl.when(first)(functools.partial(run, True))
    pl.when(jnp.logical_not(first))(functools.partial(run, False))


def _ffn_call(x2d, gain, w_gate, w_up, w_down, final_gain=None):
    rows = x2d.shape[0]
    final_norm = final_gain is not None
    const = lambda i: (0, 0)
    hbm = pl.BlockSpec(memory_space=pl.ANY)
    in_specs = [
        pl.BlockSpec((FFN_ROWS, D_MODEL), lambda i: (i, 0)),
        pl.BlockSpec((1, D_MODEL), const),
        hbm, hbm, hbm,
    ]
    args = [x2d, gain.reshape(1, D_MODEL), w_gate, w_up, w_down]
    if final_norm:
        in_specs.append(pl.BlockSpec((1, D_MODEL), const))
        args.append(final_gain.reshape(1, D_MODEL))
    return pl.pallas_call(
        functools.partial(_ffn_body, final_norm=final_norm),
        grid=(rows // FFN_ROWS,),
        in_specs=in_specs,
        out_specs=pl.BlockSpec((FFN_ROWS, D_MODEL), lambda i: (i, 0)),
        out_shape=jax.ShapeDtypeStruct((rows, D_MODEL), F32),
        scratch_shapes=[
            pltpu.VMEM((D_MODEL, D_FF), BF16),
            pltpu.VMEM((D_MODEL, D_FF), BF16),
            pltpu.VMEM((D_FF, D_MODEL), BF16),
            pltpu.VMEM((2, 2, D_MODEL, FFN_COLS), F32),
            pltpu.VMEM((2, FFN_COLS, D_MODEL), F32),
            pltpu.SemaphoreType.DMA((2, 3)),
        ],
        compiler_params=pltpu.CompilerParams(
            dimension_semantics=("arbitrary",), vmem_limit_bytes=VMEM_LIMIT_BYTES),
        name="ffn_final" if final_norm else "ffn",
    )(*args)


def _split_hi_lo(x):
    hi = x.astype(BF16)
    lo = (x - hi.astype(F32)).astype(BF16)
    return hi, lo


def _lane_bcast(x, col):
    return jnp.broadcast_to(x[:, col:col + 1], (x.shape[0], LANES))


def _col_to_wide(col, lane_lo):
    left = jnp.where(lane_lo, col[0:CHUNK], col[CHUNK:2 * CHUNK])
    right = jnp.where(lane_lo, col[2 * CHUNK:3 * CHUNK], col[3 * CHUNK:])
    return jnp.concatenate([left, right], axis=1)


def _block_diag(wide, bd_mask):
    return jnp.where(bd_mask, jnp.concatenate([wide.astype(F32)] * MIX_CHUNKS, axis=0), 0.0).astype(BF16)


def _handoff_shapes(n_seq):
    n_chain = n_seq * DN_HEADS
    return [
        pltpu.VMEM((2, n_seq * MIX_ROWS, GM_WIDTH), BF16),
        pltpu.VMEM((2, n_seq * MIX_ROWS, DN_WIDTH), F32),
        pltpu.VMEM((2, n_chain, CHUNK, MIX_ROWS), F32),
        pltpu.VMEM((2, n_chain, CHUNK, MIX_ROWS), BF16),
        pltpu.VMEM((2, n_chain, MIX_ROWS, 2 * DN_HEAD_DIM), BF16),
        pltpu.VMEM((2, n_chain, MIX_ROWS, DN_HEAD_DIM), BF16),
        pltpu.VMEM((2, n_chain, MIX_ROWS, DN_HEAD_DIM), BF16),
        pltpu.VMEM((2, n_chain, MIX_CHUNKS * SUBLANES, DN_HEAD_DIM), F32),
    ]


N_HANDOFF = 8


def _mixer_stage_a(x_ref, mg_ref, win_ref, wba_ref, lng_ref, lnb_ref, wcat_ref, bias_ref, cw_ref,
                   alog_ref, dtb_ref, cin_ref, qkv_ref, out):
    ya_ref, zg_ref, m_ref, attn_ref, rhs_ref, qd_ref, kd_ref, egl_ref = out
    n_seq = x_ref.shape[0]
    rows = MIX_ROWS
    all_rows = n_seq * rows
    n_blocks = all_rows // GM_BLOCK

    x = x_ref[...].reshape(all_rows, D_MODEL)
    hb = _rmsnorm(x, mg_ref[...]).astype(BF16)
    yield

    def project(lo, hi):
        parts = []
        for c0 in range(lo, hi, PROJ_COLS):
            parts.append(jnp.dot(hb, win_ref[:, c0:min(c0 + PROJ_COLS, hi)], **ACC))
            yield
        return jnp.concatenate(parts, axis=1)

    for b in range(n_seq):
        for s in range(3 * DN_WIDTH // LANES):
            cin_ref[b, s, 0:CONV_TAIL, :] = cin_ref[b, s, rows:rows + CONV_TAIL, :]
    for c0 in range(0, 3 * DN_WIDTH, PROJ_COLS):
        blk = jnp.dot(hb, win_ref[:, 2 * GM_WIDTH + c0:2 * GM_WIDTH + c0 + PROJ_COLS], **ACC)
        for b in range(n_seq):
            for j in range(PROJ_COLS // LANES):
                cin_ref[b, c0 // LANES + j, CONV_TAIL:, :] = blk[b * rows:(b + 1) * rows, j * LANES:(j + 1) * LANES]
        yield
    pba = jnp.dot(hb, wba_ref[...], **ACC)
    yield
    half = CONV_BLOCK // 2
    for b in range(n_seq):
        for s in range(3 * DN_WIDTH // LANES):
            src, dst = cin_ref.at[b, s], qkv_ref.at[b, s]
            w = cw_ref[:, s * LANES:(s + 1) * LANES]
            for r0 in range(0, rows, CONV_BLOCK):
                tap = lambda off: src[pl.ds(CONV_TAIL + r0 + off, half, stride=2), :]
                e0, e2, o1, om1, om3 = tap(0), tap(-2), tap(1), tap(-1), tap(-3)
                even = w[3:4] * e0 + w[2:3] * om1 + w[1:2] * e2 + w[0:1] * om3
                odd = w[3:4] * o1 + w[2:3] * e0 + w[1:2] * om1 + w[0:1] * e2
                dst[pl.ds(r0, half, stride=2), :] = jax.nn.silu(even)
                dst[pl.ds(r0 + 1, half, stride=2), :] = jax.nn.silu(odd)
    beta_all = jax.nn.sigmoid(pba)
    g_all = -jnp.exp(alog_ref[...]) * _softplus(pba + dtb_ref[...])

    v_a = jax.nn.gelu((yield from project(GM_WIDTH, 2 * GM_WIDTH)))
    zg_ref[...] = jax.nn.silu((yield from project(2 * GM_WIDTH + 3 * DN_WIDTH, 2 * GM_WIDTH + 4 * DN_WIDTH)))

    ri = lax.broadcasted_iota(jnp.int32, (rows, rows), 0)
    ci = lax.broadcasted_iota(jnp.int32, (rows, rows), 1)
    bd_mask = (ri // CHUNK) == (ci // CHUNK)
    tri_b = jnp.where(bd_mask & (ci <= ri), 1.0, 0.0).astype(BF16)
    blk_b = jnp.where(bd_mask, 1.0, 0.0).astype(BF16)
    g_hi, g_lo = _split_hi_lo(jnp.concatenate([g_all[b * rows:(b + 1) * rows] for b in range(n_seq)], axis=1))
    gc_all = jnp.dot(tri_b, g_hi, **ACC) + jnp.dot(tri_b, g_lo, **ACC)
    gl_all = jnp.dot(blk_b, g_hi, **ACC) + jnp.dot(blk_b, g_lo, **ACC)
    yield
    e_gc_all = jnp.exp(gc_all)
    e_glgc_all = jnp.exp(gl_all - gc_all)
    e_gl_all = jnp.exp(gl_all)
    gc_rows = [gc_all[:, b * LANES:(b + 1) * LANES].T for b in range(n_seq)]

    mu = jnp.mean(v_a, axis=-1, keepdims=True)
    vc = v_a - mu
    var = jnp.mean(vc * vc, axis=-1, keepdims=True)
    v_n = (vc * lax.rsqrt(var + EPS)) * lng_ref[...] + lnb_ref[...]

    wi = lax.broadcasted_iota(jnp.int32, (GM_BLOCK, 2 * GM_BLOCK), 0) // CHUNK
    wj = (lax.broadcasted_iota(jnp.int32, (GM_BLOCK, 2 * GM_BLOCK), 1) % GM_BLOCK) // CHUNK
    w_mask = wj <= wi
    lane_lo128 = lax.broadcasted_iota(jnp.int32, (GM_BLOCK, LANES), 1) < GM_GROUP_DIM
    mixed_cols = []
    for p in range(GM_GROUPS // 2):
        v_p = v_n[:, p * LANES:(p + 1) * LANES]
        rhs = []
        for blk_i in range(n_blocks):
            blk = v_p[blk_i * GM_BLOCK:(blk_i + 1) * GM_BLOCK]
            rhs.append(jnp.concatenate(
                [jnp.where(lane_lo128, blk, 0.0), jnp.where(lane_lo128, 0.0, blk)], axis=0))
        rhs = jnp.concatenate(rhs, axis=1).astype(BF16)
        w_p = jnp.where(w_mask, wcat_ref[p], 0.0).astype(BF16)
        res = jnp.dot(w_p, rhs, **ACC)
        mixed_cols.append(jnp.concatenate(
            [res[:, i * LANES:(i + 1) * LANES] for i in range(n_blocks)], axis=0))
        if p % 2 == 1:
            yield
    bias = jnp.concatenate([bias_ref[...]] * n_blocks, axis=0)
    u_a = jax.nn.gelu((yield from project(0, GM_WIDTH)))
    ya_ref[...] = (u_a * (jnp.concatenate(mixed_cols, axis=1) + bias)).astype(BF16)

    wr = lax.broadcasted_iota(jnp.int32, (CHUNK, rows), 0)
    wc = lax.broadcasted_iota(jnp.int32, (CHUNK, rows), 1) % CHUNK
    tri_w = wc <= wr
    strict_w = wc < wr
    lane_lo = lax.broadcasted_iota(jnp.int32, (CHUNK, LANES), 1) < CHUNK
    scale = DN_HEAD_DIM ** -0.5

    zeros_k = jnp.zeros((2 * CHUNK, DN_HEAD_DIM), BF16)
    for b in range(n_seq):
        rs = slice(b * rows, (b + 1) * rows)
        for h0 in range(0, DN_HEADS, 2):
            q, k, k_beta = {}, {}, {}
            for h in (h0, h0 + 1):
                chain = b * DN_HEADS + h
                col = b * LANES + DN_HEADS + h
                q_h, k_h, v_h = qkv_ref[b, h], qkv_ref[b, DN_HEADS + h], qkv_ref[b, 2 * DN_HEADS + h]
                q[h] = q_h * lax.rsqrt(jnp.sum(q_h * q_h, axis=-1, keepdims=True) + EPS) * scale
                k[h] = k_h * lax.rsqrt(jnp.sum(k_h * k_h, axis=-1, keepdims=True) + EPS)
                beta = _lane_bcast(beta_all[rs], h)
                e_gc = _lane_bcast(e_gc_all, col)
                k_beta[h] = k[h] * beta
                rhs_ref[chain] = jnp.concatenate([v_h * beta, k_beta[h] * e_gc], axis=1).astype(BF16)
                qd_ref[chain] = (q[h] * e_gc).astype(BF16)
                kd_ref[chain] = (k[h] * _lane_bcast(e_glgc_all, col)).astype(BF16)
                for ch in range(MIX_CHUNKS):
                    egl_ref[chain, ch * SUBLANES:(ch + 1) * SUBLANES, :] = _lane_bcast(
                        e_gl_all[ch * CHUNK:ch * CHUNK + SUBLANES], col)
            kk_w = {h: [] for h in (h0, h0 + 1)}
            qk_w = {h: [] for h in (h0, h0 + 1)}
            for half_i in range(rows // LANES):
                hr = slice(half_i * LANES, (half_i + 1) * LANES)
                lhs = jnp.concatenate(
                    [jnp.concatenate([k_beta[h][hr], q[h][hr]], axis=0) for h in (h0, h0 + 1)], axis=1)
                kb0, kb1 = k[h0][hr].astype(BF16), k[h0 + 1][hr].astype(BF16)
                rhs_k = jnp.concatenate([jnp.concatenate([kb0, zeros_k], axis=1),
                                         jnp.concatenate([zeros_k, kb1], axis=1)], axis=0)
                out = lax.dot_general(lhs.astype(BF16), rhs_k, _NT, **ACC)
                for i, h in enumerate((h0, h0 + 1)):
                    kk = out[0:LANES, i * LANES:(i + 1) * LANES]
                    qk = out[LANES:, i * LANES:(i + 1) * LANES]
                    kk_w[h].append(jnp.where(lane_lo, kk[0:CHUNK], kk[CHUNK:]))
                    qk_w[h].append(jnp.where(lane_lo, qk[0:CHUNK], qk[CHUNK:]))
            for h in (h0, h0 + 1):
                chain = b * DN_HEADS + h
                col = b * LANES + DN_HEADS + h
                dlog = (_col_to_wide(_lane_bcast(gc_all, col), lane_lo)
                        - jnp.broadcast_to(gc_rows[b][DN_HEADS + h:DN_HEADS + h + 1, :], (CHUNK, rows)))
                decay = jnp.exp(jnp.where(tri_w, dlog, -jnp.inf))
                m_ref[chain] = -jnp.where(strict_w, jnp.concatenate(kk_w[h], axis=1) * decay, 0.0)
                attn_ref[chain] = (jnp.concatenate(qk_w[h], axis=1) * decay).astype(BF16)
            yield


def _mixer_stage_b(x_ref, dng_ref, wout_ref, s_ref, o_ref, inp):
    ya_ref, zg_ref, m_ref, attn_ref, rhs_ref, qd_ref, kd_ref, egl_ref = inp
    n_seq = x_ref.shape[0]
    rows = MIX_ROWS
    all_rows = n_seq * rows
    chains = range(n_seq * DN_HEADS)

    ri = lax.broadcasted_iota(jnp.int32, (rows, rows), 0)
    ci = lax.broadcasted_iota(jnp.int32, (rows, rows), 1)
    bd_mask = (ri // CHUNK) == (ci // CHUNK)
    wr = lax.broadcasted_iota(jnp.int32, (CHUNK, rows), 0)
    wc = lax.broadcasted_iota(jnp.int32, (CHUNK, rows), 1) % CHUNK
    eye_w = jnp.where(wc == wr, 1.0, 0.0)

    m_w = {c: m_ref[c] for c in chains}
    inv_w = {c: eye_w + m_w[c] for c in chains}
    m_bd = {c: _block_diag(m_w[c], bd_mask) for c in chains}
    for _ in range(5):
        for c in chains:
            m_w[c] = jnp.dot(m_w[c].astype(BF16), m_bd[c], **ACC)
        yield
        for c in chains:
            m_bd[c] = _block_diag(m_w[c], bd_mask)
        for c in chains:
            inv_w[c] = inv_w[c] + jnp.dot(inv_w[c].astype(BF16), m_bd[c], **ACC)
        yield
    uw = {c: jnp.dot(_block_diag(inv_w[c], bd_mask), rhs_ref[c], **ACC) for c in chains}
    yield

    state = {c: s_ref[c] for c in chains}
    v_new = {c: [] for c in chains}
    o_inter = {c: [] for c in chains}
    pairs = [(c, c + 1) for c in range(0, n_seq * DN_HEADS, 2)]
    zeros_s = jnp.zeros((DN_HEAD_DIM, DN_HEAD_DIM), BF16)
    zeros_v = jnp.zeros((CHUNK, DN_HEAD_DIM), BF16)

    def block_diag2(a, b, zeros):
        return jnp.concatenate([jnp.concatenate([a, zeros], axis=1), jnp.concatenate([zeros, b], axis=1)], axis=0)

    for ch in range(MIX_CHUNKS):
        cs = slice(ch * CHUNK, (ch + 1) * CHUNK)
        r = {}
        for pair in pairs:
            wq = jnp.concatenate(
                [jnp.concatenate([uw[c][cs, DN_HEAD_DIM:].astype(BF16), qd_ref[c, cs, :]], axis=0) for c in pair],
                axis=1)
            s_bd = block_diag2(state[pair[0]].astype(BF16), state[pair[1]].astype(BF16), zeros_s)
            r[pair] = jnp.dot(wq, s_bd, **ACC)
        yield
        for pair in pairs:
            vn = {}
            for i, c in enumerate(pair):
                r_c = r[pair][:, i * DN_HEAD_DIM:(i + 1) * DN_HEAD_DIM]
                vn[c] = uw[c][cs, :DN_HEAD_DIM] - r_c[:CHUNK]
                v_new[c].append(vn[c])
                o_inter[c].append(r_c[CHUNK:])
            kd = jnp.concatenate([kd_ref[c, cs, :] for c in pair], axis=0)
            vn_bd = block_diag2(vn[pair[0]].astype(BF16), vn[pair[1]].astype(BF16), zeros_v)
            upd = lax.dot_general(kd, vn_bd, _TN, **ACC)
            for i, c in enumerate(pair):
                decay_s = jnp.concatenate([egl_ref[c, ch * SUBLANES:(ch + 1) * SUBLANES, :]] * (DN_HEAD_DIM // SUBLANES), axis=0)
                state[c] = state[c] * decay_s + upd[:, i * DN_HEAD_DIM:(i + 1) * DN_HEAD_DIM]
        yield
    y_b = []
    for b in range(n_seq):
        heads = []
        for h in range(DN_HEADS):
            c = b * DN_HEADS + h
            s_ref[c] = state[c]
            o = (jnp.concatenate(o_inter[c], axis=0)
                 + jnp.dot(_block_diag(attn_ref[c], bd_mask),
                           jnp.concatenate(v_new[c], axis=0).astype(BF16), **ACC))
            heads.append(_rmsnorm(o, dng_ref[...])
                         * zg_ref[b * rows:(b + 1) * rows, h * DN_HEAD_DIM:(h + 1) * DN_HEAD_DIM])
        y_b.append(jnp.concatenate(heads, axis=1).astype(BF16))
        yield
    y = jnp.concatenate([ya_ref[...], jnp.concatenate(y_b, axis=0)], axis=1)
    for c0 in range(0, D_MODEL, PROJ_COLS):
        cols = slice(c0, c0 + PROJ_COLS)
        x = x_ref[:, :, cols].reshape(all_rows, PROJ_COLS)
        o_ref[:, :, cols] = (x + jnp.dot(y, wout_ref[:, cols], **ACC)).reshape(n_seq, rows, PROJ_COLS)
        yield


def _round_robin(*stages):
    stages = list(stages)
    while stages:
        for stage in list(stages):
            try:
                next(stage)
            except StopIteration:
                stages.remove(stage)


def _mixer_body(xa_ref, xb_ref, mg_ref, win_ref, wba_ref, lng_ref, lnb_ref, wcat_ref, bias_ref,
                cw_ref, alog_ref, dtb_ref, dng_ref, wout_ref, o_ref, s_ref, cin_ref, qkv_ref, *handoff):
    t = pl.program_id(0)
    n_tiles = pl.num_programs(0) - 1
    slot = lax.rem(t, 2)
    write_set = [ref.at[slot] for ref in handoff]
    read_set = [ref.at[1 - slot] for ref in handoff]
    stage_a = functools.partial(_mixer_stage_a, xa_ref, mg_ref, win_ref, wba_ref, lng_ref, lnb_ref, wcat_ref,
                                bias_ref, cw_ref, alog_ref, dtb_ref, cin_ref, qkv_ref)
    stage_b = functools.partial(_mixer_stage_b, xb_ref, dng_ref, wout_ref, s_ref, o_ref)

    @pl.when(t == 0)
    def _():
        s_ref[...] = jnp.zeros(s_ref.shape, F32)
        cin_ref[...] = jnp.zeros(cin_ref.shape, F32)
        _round_robin(stage_a(write_set))

    @pl.when(jnp.logical_and(t > 0, t < n_tiles))
    def _():
        _round_robin(stage_b(read_set), stage_a(write_set))

    @pl.when(t == n_tiles)
    def _():
        _round_robin(stage_b(read_set))


def _mixer_call(x, mix_norm, w_in, gm_ln_g, gm_ln_b, gm_w_s, gm_b_s, dn_conv_w, dn_a_log,
                dn_dt_bias, dn_norm, w_out):
    batch, seq, _ = x.shape
    n_tiles = seq // MIX_ROWS
    n_main = 2 * GM_WIDTH + 4 * DN_WIDTH
    w_main = w_in[:, :n_main].astype(BF16)
    w_ba = jnp.pad(w_in[:, n_main:], ((0, 0), (0, LANES - 2 * DN_HEADS))).astype(BF16)
    w_cat = gm_w_s.reshape(GM_GROUPS // 2, 2, GM_BLOCK, GM_BLOCK).transpose(0, 2, 1, 3)
    w_cat = w_cat.reshape(GM_GROUPS // 2, GM_BLOCK, 2 * GM_BLOCK)
    bias = jnp.repeat(gm_b_s.T, GM_GROUP_DIM, axis=1)
    pad_a = lambda a: jnp.pad(a.reshape(1, DN_HEADS), ((0, 0), (DN_HEADS, LANES - 2 * DN_HEADS)))
    const2 = lambda t: (0, 0)
    tile_a = lambda t: (0, jnp.minimum(t, n_tiles - 1), 0)
    tile_b = lambda t: (0, jnp.maximum(t - 1, 0), 0)
    in_specs = [
        pl.BlockSpec((batch, MIX_ROWS, D_MODEL), tile_a),
        pl.BlockSpec((batch, MIX_ROWS, D_MODEL), tile_b),
        pl.BlockSpec((1, D_MODEL), const2),
        pl.BlockSpec((D_MODEL, n_main), const2),
        pl.BlockSpec((D_MODEL, LANES), const2),
        pl.BlockSpec((1, GM_WIDTH), const2),
        pl.BlockSpec((1, GM_WIDTH), const2),
        pl.BlockSpec((GM_GROUPS // 2, GM_BLOCK, 2 * GM_BLOCK), lambda t: (0, 0, 0)),
        pl.BlockSpec((GM_BLOCK, GM_WIDTH), const2),
        pl.BlockSpec((DN_CONV, 3 * DN_WIDTH), const2),
        pl.BlockSpec((1, LANES), const2),
        pl.BlockSpec((1, LANES), const2),
        pl.BlockSpec((1, DN_HEAD_DIM), const2),
        pl.BlockSpec((D_MODEL, D_MODEL), const2),
    ]
    return pl.pallas_call(
        _mixer_body,
        grid=(n_tiles + 1,),
        in_specs=in_specs,
        out_specs=pl.BlockSpec((batch, MIX_ROWS, D_MODEL), tile_b),
        out_shape=jax.ShapeDtypeStruct((batch, seq, D_MODEL), F32),
        scratch_shapes=[
            pltpu.VMEM((batch * DN_HEADS, DN_HEAD_DIM, DN_HEAD_DIM), F32),
            pltpu.VMEM((batch, 3 * DN_HEADS, CONV_TAIL + MIX_ROWS, LANES), F32),
            pltpu.VMEM((batch, 3 * DN_HEADS, MIX_ROWS, LANES), F32),
        ] + _handoff_shapes(batch),
        compiler_params=pltpu.CompilerParams(
            dimension_semantics=("arbitrary",), vmem_limit_bytes=VMEM_LIMIT_BYTES),
        name="mixer",
    )(x, x, mix_norm.reshape(1, D_MODEL), w_main, w_ba, gm_ln_g.reshape(1, GM_WIDTH),
      gm_ln_b.reshape(1, GM_WIDTH), w_cat, bias, dn_conv_w, pad_a(dn_a_log), pad_a(dn_dt_bias),
      dn_norm.reshape(1, DN_HEAD_DIM), w_out.astype(BF16))


def kernel(x, ffn1_norm, ffn1_w_gate, ffn1_w_up, ffn1_w_down, mix_norm, w_in, gm_ln_g, gm_ln_b,
           gm_w_s, gm_b_s, dn_conv_w, dn_a_log, dn_dt_bias, dn_norm, w_out, ffn2_norm,
           ffn2_w_gate, ffn2_w_up, ffn2_w_down, final_norm):
    batch, seq, d = x.shape
    depth = ffn1_norm.shape[0]
    for l in range(depth):
        x = _ffn_call(x.reshape(batch * seq, d), ffn1_norm[l], ffn1_w_gate[l], ffn1_w_up[l],
                      ffn1_w_down[l]).reshape(batch, seq, d)
        x = _mixer_call(x, mix_norm[l], w_in[l], gm_ln_g[l], gm_ln_b[l], gm_w_s[l], gm_b_s[l],
                        dn_conv_w[l], dn_a_log[l], dn_dt_bias[l], dn_norm[l], w_out[l])
        last = l == depth - 1
        x = _ffn_call(x.reshape(batch * seq, d), ffn2_norm[l], ffn2_w_gate[l], ffn2_w_up[l],
                      ffn2_w_down[l], final_norm if last else None).reshape(batch, seq, d)
    return x
```

```python
import functools

import jax
import jax.numpy as jnp
from jax import lax
from jax.experimental import pallas as pl
from jax.experimental.pallas import tpu as pltpu

F32 = jnp.float32
BF16 = jnp.bfloat16
ACC = dict(preferred_element_type=jnp.float32)

D_MODEL = 1024
D_FF = 2816
CHUNK = 64
GM_GROUPS = 8
GM_GROUP_DIM = 64
GM_WIDTH = GM_GROUPS * GM_GROUP_DIM
GM_BLOCK = 128
DN_HEADS = 4
DN_HEAD_DIM = 128
DN_WIDTH = DN_HEADS * DN_HEAD_DIM
DN_CONV = 4
EPS = 1e-6

LANES = 128
SUBLANES = 8
VMEM_LIMIT_BYTES = 60000 * 1024

FFN_ROWS = 512
FFN_COLS = 256
MIX_ROWS = 256
MIX_CHUNKS = MIX_ROWS // CHUNK
CONV_TAIL = SUBLANES
PROJ_COLS = 256
CONV_BLOCK = 64

_NT = (((1,), (1,)), ((), ()))
_TN = (((0,), (0,)), ((), ()))


def _rmsnorm(x, g):
    return (x * lax.rsqrt(jnp.mean(x * x, axis=-1, keepdims=True) + EPS)) * g


def _softplus(x):
    return jnp.maximum(x, 0.0) + jnp.log1p(jnp.exp(-jnp.abs(x)))


def _ffn_body(x_ref, g_ref, wg_hbm, wu_hbm, wd_hbm, *rest, final_norm):
    if final_norm:
        fg_ref, o_ref, wg_ref, wu_ref, wd_ref, stage_gu, stage_d, sem = rest
    else:
        o_ref, wg_ref, wu_ref, wd_ref, stage_gu, stage_d, sem = rest
    n_chunks = D_FF // FFN_COLS

    def chunk_copies(j, slot):
        cols = pl.ds(j * FFN_COLS, FFN_COLS)
        return (pltpu.make_async_copy(wg_hbm.at[:, cols], stage_gu.at[slot, 0], sem.at[slot, 0]),
                pltpu.make_async_copy(wu_hbm.at[:, cols], stage_gu.at[slot, 1], sem.at[slot, 1]),
                pltpu.make_async_copy(wd_hbm.at[cols, :], stage_d.at[slot], sem.at[slot, 2]))

    def run(load_weights):
        x = x_ref[...]
        hb = _rmsnorm(x, g_ref[...]).astype(BF16)
        acc = jnp.zeros(x.shape, F32)
        if load_weights:
            for cp in chunk_copies(0, 0):
                cp.start()
        for j in range(n_chunks):
            cols = slice(j * FFN_COLS, (j + 1) * FFN_COLS)
            if load_weights:
                slot = j % 2
                if j + 1 < n_chunks:
                    for cp in chunk_copies(j + 1, 1 - slot):
                        cp.start()
                for cp in chunk_copies(j, slot):
                    cp.wait()
                wg_ref[:, cols] = stage_gu[slot, 0].astype(BF16)
                wu_ref[:, cols] = stage_gu[slot, 1].astype(BF16)
                wd_ref[cols, :] = stage_d[slot].astype(BF16)
            gate = jnp.dot(hb, wg_ref[:, cols], **ACC)
            up = jnp.dot(hb, wu_ref[:, cols], **ACC)
            act = (jax.nn.silu(gate) * up).astype(BF16)
            acc = acc + jnp.dot(act, wd_ref[cols, :], **ACC)
        y = x + 0.5 * acc
        if final_norm:
            y = _rmsnorm(y, fg_ref[...])
        o_ref[...] = y

    first = pl.program_id(0) == 0
    pl.when(first)(functools.partial(run, True))
    pl.when(jnp.logical_not(first))(functools.partial(run, False))


def _ffn_call(x2d, gain, w_gate, w_up, w_down, final_gain=None):
    rows = x2d.shape[0]
    final_norm = final_gain is not None
    const = lambda i: (0, 0)
    hbm = pl.BlockSpec(memory_space=pl.ANY)
    in_specs = [
        pl.BlockSpec((FFN_ROWS, D_MODEL), lambda i: (i, 0)),
        pl.BlockSpec((1, D_MODEL), const),
        hbm, hbm, hbm,
    ]
    args = [x2d, gain.reshape(1, D_MODEL), w_gate, w_up, w_down]
    if final_norm:
        in_specs.append(pl.BlockSpec((1, D_MODEL), const))
        args.append(final_gain.reshape(1, D_MODEL))
    return pl.pallas_call(
        functools.partial(_ffn_body, final_norm=final_norm),
        grid=(rows // FFN_ROWS,),
        in_specs=in_specs,
        out_specs=pl.BlockSpec((FFN_ROWS, D_MODEL), lambda i: (i, 0)),
        out_shape=jax.ShapeDtypeStruct((rows, D_MODEL), F32),
        scratch_shapes=[
            pltpu.VMEM((D_MODEL, D_FF), BF16),
            pltpu.VMEM((D_MODEL, D_FF), BF16),
            pltpu.VMEM((D_FF, D_MODEL), BF16),
            pltpu.VMEM((2, 2, D_MODEL, FFN_COLS), F32),
            pltpu.VMEM((2, FFN_COLS, D_MODEL), F32),
            pltpu.SemaphoreType.DMA((2, 3)),
        ],
        compiler_params=pltpu.CompilerParams(
            dimension_semantics=("arbitrary",), vmem_limit_bytes=VMEM_LIMIT_BYTES),
        name="ffn_final" if final_norm else "ffn",
    )(*args)


def _split_hi_lo(x):
    hi = x.astype(BF16)
    lo = (x - hi.astype(F32)).astype(BF16)
    return hi, lo


def _lane_bcast(x, col):
    return jnp.broadcast_to(x[:, col:col + 1], (x.shape[0], LANES))


def _col_to_wide(col, lane_lo):
    left = jnp.where(lane_lo, col[0:CHUNK], col[CHUNK:2 * CHUNK])
    right = jnp.where(lane_lo, col[2 * CHUNK:3 * CHUNK], col[3 * CHUNK:])
    return jnp.concatenate([left, right], axis=1)


def _block_diag(wide, bd_mask):
    return jnp.where(bd_mask, jnp.concatenate([wide.astype(F32)] * MIX_CHUNKS, axis=0), 0.0).astype(BF16)


def _handoff_shapes(n_seq):
    n_chain = n_seq * DN_HEADS
    return [
        pltpu.VMEM((2, n_seq * MIX_ROWS, GM_WIDTH), BF16),
        pltpu.VMEM((2, n_seq * MIX_ROWS, DN_WIDTH), F32),
        pltpu.VMEM((2, n_chain, CHUNK, MIX_ROWS), F32),
        pltpu.VMEM((2, n_chain, CHUNK, MIX_ROWS), BF16),
        pltpu.VMEM((2, n_chain, MIX_ROWS, 2 * DN_HEAD_DIM), BF16),
        pltpu.VMEM((2, n_chain, MIX_ROWS, DN_HEAD_DIM), BF16),
        pltpu.VMEM((2, n_chain, MIX_ROWS, DN_HEAD_DIM), BF16),
        pltpu.VMEM((2, n_chain, MIX_CHUNKS * SUBLANES, DN_HEAD_DIM), F32),
    ]


N_HANDOFF = 8


def _mixer_stage_a(x_ref, mg_ref, win_ref, wba_ref, lng_ref, lnb_ref, wcat_ref, bias_ref, cw_ref,
                   alog_ref, dtb_ref, cin_ref, qkv_ref, out):
    ya_ref, zg_ref, m_ref, attn_ref, rhs_ref, qd_ref, kd_ref, egl_ref = out
    n_seq = x_ref.shape[0]
    rows = MIX_ROWS
    all_rows = n_seq * rows
    n_blocks = all_rows // GM_BLOCK

    x = x_ref[...].reshape(all_rows, D_MODEL)
    hb = _rmsnorm(x, mg_ref[...]).astype(BF16)
    yield

    def project(lo, hi):
        parts = []
        for c0 in range(lo, hi, PROJ_COLS):
            parts.append(jnp.dot(hb, win_ref[:, c0:min(c0 + PROJ_COLS, hi)], **ACC))
            yield
        return jnp.concatenate(parts, axis=1)

    for b in range(n_seq):
        for s in range(3 * DN_WIDTH // LANES):
            cin_ref[b, s, 0:CONV_TAIL, :] = cin_ref[b, s, rows:rows + CONV_TAIL, :]
    for c0 in range(0, 3 * DN_WIDTH, PROJ_COLS):
        blk = jnp.dot(hb, win_ref[:, 2 * GM_WIDTH + c0:2 * GM_WIDTH + c0 + PROJ_COLS], **ACC)
        for b in range(n_seq):
            for j in range(PROJ_COLS // LANES):
                cin_ref[b, c0 // LANES + j, CONV_TAIL:, :] = blk[b * rows:(b + 1) * rows, j * LANES:(j + 1) * LANES]
        yield
    pba = jnp.dot(hb, wba_ref[...], **ACC)
    yield
    half = CONV_BLOCK // 2
    for b in range(n_seq):
        for s in range(3 * DN_WIDTH // LANES):
            src, dst = cin_ref.at[b, s], qkv_ref.at[b, s]
            w = cw_ref[:, s * LANES:(s + 1) * LANES]
            for r0 in range(0, rows, CONV_BLOCK):
                tap = lambda off: src[pl.ds(CONV_TAIL + r0 + off, half, stride=2), :]
                e0, e2, o1, om1, om3 = tap(0), tap(-2), tap(1), tap(-1), tap(-3)
                even = w[3:4] * e0 + w[2:3] * om1 + w[1:2] * e2 + w[0:1] * om3
                odd = w[3:4] * o1 + w[2:3] * e0 + w[1:2] * om1 + w[0:1] * e2
                dst[pl.ds(r0, half, stride=2), :] = jax.nn.silu(even)
                dst[pl.ds(r0 + 1, half, stride=2), :] = jax.nn.silu(odd)
    beta_all = jax.nn.sigmoid(pba)
    g_all = -jnp.exp(alog_ref[...]) * _softplus(pba + dtb_ref[...])

    v_a = jax.nn.gelu((yield from project(GM_WIDTH, 2 * GM_WIDTH)))
    zg_ref[...] = jax.nn.silu((yield from project(2 * GM_WIDTH + 3 * DN_WIDTH, 2 * GM_WIDTH + 4 * DN_WIDTH)))

    ri = lax.broadcasted_iota(jnp.int32, (rows, rows), 0)
    ci = lax.broadcasted_iota(jnp.int32, (rows, rows), 1)
    bd_mask = (ri // CHUNK) == (ci // CHUNK)
    tri_b = jnp.where(bd_mask & (ci <= ri), 1.0, 0.0).astype(BF16)
    blk_b = jnp.where(bd_mask, 1.0, 0.0).astype(BF16)
    g_hi, g_lo = _split_hi_lo(jnp.concatenate([g_all[b * rows:(b + 1) * rows] for b in range(n_seq)], axis=1))
    gc_all = jnp.dot(tri_b, g_hi, **ACC) + jnp.dot(tri_b, g_lo, **ACC)
    gl_all = jnp.dot(blk_b, g_hi, **ACC) + jnp.dot(blk_b, g_lo, **ACC)
    yield
    e_gc_all = jnp.exp(gc_all)
    e_glgc_all = jnp.exp(gl_all - gc_all)
    e_gl_all = jnp.exp(gl_all)
    gc_rows = [gc_all[:, b * LANES:(b + 1) * LANES].T for b in range(n_seq)]

    mu = jnp.mean(v_a, axis=-1, keepdims=True)
    vc = v_a - mu
    var = jnp.mean(vc * vc, axis=-1, keepdims=True)
    v_n = (vc * lax.rsqrt(var + EPS)) * lng_ref[...] + lnb_ref[...]

    wi = lax.broadcasted_iota(jnp.int32, (GM_BLOCK, 2 * GM_BLOCK), 0) // CHUNK
    wj = (lax.broadcasted_iota(jnp.int32, (GM_BLOCK, 2 * GM_BLOCK), 1) % GM_BLOCK) // CHUNK
    w_mask = wj <= wi
    lane_lo128 = lax.broadcasted_iota(jnp.int32, (GM_BLOCK, LANES), 1) < GM_GROUP_DIM
    mixed_cols = []
    for p in range(GM_GROUPS // 2):
        v_p = v_n[:, p * LANES:(p + 1) * LANES]
        rhs = []
        for blk_i in range(n_blocks):
            blk = v_p[blk_i * GM_BLOCK:(blk_i + 1) * GM_BLOCK]
            rhs.append(jnp.concatenate(
                [jnp.where(lane_lo128, blk, 0.0), jnp.where(lane_lo128, 0.0, blk)], axis=0))
        rhs = jnp.concatenate(rhs, axis=1).astype(BF16)
        w_p = jnp.where(w_mask, wcat_ref[p], 0.0).astype(BF16)
        res = jnp.dot(w_p, rhs, **ACC)
        mixed_cols.append(jnp.concatenate(
            [res[:, i * LANES:(i + 1) * LANES] for i in range(n_blocks)], axis=0))
        if p % 2 == 1:
            yield
    bias = jnp.concatenate([bias_ref[...]] * n_blocks, axis=0)
    u_a = jax.nn.gelu((yield from project(0, GM_WIDTH)))
    ya_ref[...] = (u_a * (jnp.concatenate(mixed_cols, axis=1) + bias)).astype(BF16)

    wr = lax.broadcasted_iota(jnp.int32, (CHUNK, rows), 0)
    wc = lax.broadcasted_iota(jnp.int32, (CHUNK, rows), 1) % CHUNK
    tri_w = wc <= wr
    strict_w = wc < wr
    lane_lo = lax.broadcasted_iota(jnp.int32, (CHUNK, LANES), 1) < CHUNK
    scale = DN_HEAD_DIM ** -0.5

    zeros_k = jnp.zeros((2 * CHUNK, DN_HEAD_DIM), BF16)
    for b in range(n_seq):
        rs = slice(b * rows, (b + 1) * rows)
        for h0 in range(0, DN_HEADS, 2):
            q, k, k_beta = {}, {}, {}
            for h in (h0, h0 + 1):
                chain = b * DN_HEADS + h
                col = b * LANES + DN_HEADS + h
                q_h, k_h, v_h = qkv_ref[b, h], qkv_ref[b, DN_HEADS + h], qkv_ref[b, 2 * DN_HEADS + h]
                q[h] = q_h * lax.rsqrt(jnp.sum(q_h * q_h, axis=-1, keepdims=True) + EPS) * scale
                k[h] = k_h * lax.rsqrt(jnp.sum(k_h * k_h, axis=-1, keepdims=True) + EPS)
                beta = _lane_bcast(beta_all[rs], h)
                e_gc = _lane_bcast(e_gc_all, col)
                k_beta[h] = k[h] * beta
                rhs_ref[chain] = jnp.concatenate([v_h * beta, k_beta[h] * e_gc], axis=1).astype(BF16)
                qd_ref[chain] = (q[h] * e_gc).astype(BF16)
                kd_ref[chain] = (k[h] * _lane_bcast(e_glgc_all, col)).astype(BF16)
                for ch in range(MIX_CHUNKS):
                    egl_ref[chain, ch * SUBLANES:(ch + 1) * SUBLANES, :] = _lane_bcast(
                        e_gl_all[ch * CHUNK:ch * CHUNK + SUBLANES], col)
            kk_w = {h: [] for h in (h0, h0 + 1)}
            qk_w = {h: [] for h in (h0, h0 + 1)}
            for half_i in range(rows // LANES):
                hr = slice(half_i * LANES, (half_i + 1) * LANES)
                lhs = jnp.concatenate(
                    [jnp.concatenate([k_beta[h][hr], q[h][hr]], axis=0) for h in (h0, h0 + 1)], axis=1)
                kb0, kb1 = k[h0][hr].astype(BF16), k[h0 + 1][hr].astype(BF16)
                rhs_k = jnp.concatenate([jnp.concatenate([kb0, zeros_k], axis=1),
                                         jnp.concatenate([zeros_k, kb1], axis=1)], axis=0)
                out = lax.dot_general(lhs.astype(BF16), rhs_k, _NT, **ACC)
                for i, h in enumerate((h0, h0 + 1)):
                    kk = out[0:LANES, i * LANES:(i + 1) * LANES]
                    qk = out[LANES:, i * LANES:(i + 1) * LANES]
                    kk_w[h].append(jnp.where(lane_lo, kk[0:CHUNK], kk[CHUNK:]))
                    qk_w[h].append(jnp.where(lane_lo, qk[0:CHUNK], qk[CHUNK:]))
            for h in (h0, h0 + 1):
                chain = b * DN_HEADS + h
                col = b * LANES + DN_HEADS + h
                dlog = (_col_to_wide(_lane_bcast(gc_all, col), lane_lo)
                        - jnp.broadcast_to(gc_rows[b][DN_HEADS + h:DN_HEADS + h + 1, :], (CHUNK, rows)))
                decay = jnp.exp(jnp.where(tri_w, dlog, -jnp.inf))
                m_ref[chain] = -jnp.where(strict_w, jnp.concatenate(kk_w[h], axis=1) * decay, 0.0)
                attn_ref[chain] = (jnp.concatenate(qk_w[h], axis=1) * decay).astype(BF16)
            yield


def _mixer_stage_b(x_ref, dng_ref, wout_ref, s_ref, o_ref, inp):
    ya_ref, zg_ref, m_ref, attn_ref, rhs_ref, qd_ref, kd_ref, egl_ref = inp
    n_seq = x_ref.shape[0]
    rows = MIX_ROWS
    all_rows = n_seq * rows
    chains = range(n_seq * DN_HEADS)

    ri = lax.broadcasted_iota(jnp.int32, (rows, rows), 0)
    ci = lax.broadcasted_iota(jnp.int32, (rows, rows), 1)
    bd_mask = (ri // CHUNK) == (ci // CHUNK)
    wr = lax.broadcasted_iota(jnp.int32, (CHUNK, rows), 0)
    wc = lax.broadcasted_iota(jnp.int32, (CHUNK, rows), 1) % CHUNK
    eye_w = jnp.where(wc == wr, 1.0, 0.0)

    m_w = {c: m_ref[c] for c in chains}
    inv_w = {c: eye_w + m_w[c] for c in chains}
    m_bd = {c: _block_diag(m_w[c], bd_mask) for c in chains}
    for _ in range(5):
        for c in chains:
            m_w[c] = jnp.dot(m_w[c].astype(BF16), m_bd[c], **ACC)
        yield
        for c in chains:
            m_bd[c] = _block_diag(m_w[c], bd_mask)
        for c in chains:
            inv_w[c] = inv_w[c] + jnp.dot(inv_w[c].astype(BF16), m_bd[c], **ACC)
        yield
    uw = {c: jnp.dot(_block_diag(inv_w[c], bd_mask), rhs_ref[c], **ACC) for c in chains}
    yield

    state = {c: s_ref[c] for c in chains}
    v_new = {c: [] for c in chains}
    o_inter = {c: [] for c in chains}
    pairs = [(c, c + 1) for c in range(0, n_seq * DN_HEADS, 2)]
    zeros_s = jnp.zeros((DN_HEAD_DIM, DN_HEAD_DIM), BF16)
    zeros_v = jnp.zeros((CHUNK, DN_HEAD_DIM), BF16)

    def block_diag2(a, b, zeros):
        return jnp.concatenate([jnp.concatenate([a, zeros], axis=1), jnp.concatenate([zeros, b], axis=1)], axis=0)

    for ch in range(MIX_CHUNKS):
        cs = slice(ch * CHUNK, (ch + 1) * CHUNK)
        r = {}
        for pair in pairs:
            wq = jnp.concatenate(
                [jnp.concatenate([uw[c][cs, DN_HEAD_DIM:].astype(BF16), qd_ref[c, cs, :]], axis=0) for c in pair],
                axis=1)
            s_bd = block_diag2(state[pair[0]].astype(BF16), state[pair[1]].astype(BF16), zeros_s)
            r[pair] = jnp.dot(wq, s_bd, **ACC)
        yield
        for pair in pairs:
            vn = {}
            for i, c in enumerate(pair):
                r_c = r[pair][:, i * DN_HEAD_DIM:(i + 1) * DN_HEAD_DIM]
                vn[c] = uw[c][cs, :DN_HEAD_DIM] - r_c[:CHUNK]
                v_new[c].append(vn[c])
                o_inter[c].append(r_c[CHUNK:])
            kd = jnp.concatenate([kd_ref[c, cs, :] for c in pair], axis=0)
            vn_bd = block_diag2(vn[pair[0]].astype(BF16), vn[pair[1]].astype(BF16), zeros_v)
            upd = lax.dot_general(kd, vn_bd, _TN, **ACC)
            for i, c in enumerate(pair):
                decay_s = jnp.concatenate([egl_ref[c, ch * SUBLANES:(ch + 1) * SUBLANES, :]] * (DN_HEAD_DIM // SUBLANES), axis=0)
                state[c] = state[c] * decay_s + upd[:, i * DN_HEAD_DIM:(i + 1) * DN_HEAD_DIM]
        yield
    y_b = []
    for b in range(n_seq):
        heads = []
        for h in range(DN_HEADS):
            c = b * DN_HEADS + h
            s_ref[c] = state[c]
            o = (jnp.concatenate(o_inter[c], axis=0)
                 + jnp.dot(_block_diag(attn_ref[c], bd_mask),
                           jnp.concatenate(v_new[c], axis=0).astype(BF16), **ACC))
            heads.append(_rmsnorm(o, dng_ref[...])
                         * zg_ref[b * rows:(b + 1) * rows, h * DN_HEAD_DIM:(h + 1) * DN_HEAD_DIM])
        y_b.append(jnp.concatenate(heads, axis=1).astype(BF16))
        yield
    y = jnp.concatenate([ya_ref[...], jnp.concatenate(y_b, axis=0)], axis=1)
    for c0 in range(0, D_MODEL, PROJ_COLS):
        cols = slice(c0, c0 + PROJ_COLS)
        x = x_ref[:, :, cols].reshape(all_rows, PROJ_COLS)
        o_ref[:, :, cols] = (x + jnp.dot(y, wout_ref[:, cols], **ACC)).reshape(n_seq, rows, PROJ_COLS)
        yield


def _round_robin(*stages):
    stages = list(stages)
    while stages:
        for stage in list(stages):
            try:
                next(stage)
            except StopIteration:
                stages.remove(stage)


def _mixer_body(xa_ref, xb_ref, mg_ref, win_ref, wba_ref, lng_ref, lnb_ref, wcat_ref, bias_ref,
                cw_ref, alog_ref, dtb_ref, dng_ref, wout_ref, o_ref, s_ref, cin_ref, qkv_ref, *handoff):
    t = pl.program_id(0)
    n_tiles = pl.num_programs(0) - 1
    slot = lax.rem(t, 2)
    write_set = [ref.at[slot] for ref in handoff]
    read_set = [ref.at[1 - slot] for ref in handoff]
    stage_a = functools.partial(_mixer_stage_a, xa_ref, mg_ref, win_ref, wba_ref, lng_ref, lnb_ref, wcat_ref,
                                bias_ref, cw_ref, alog_ref, dtb_ref, cin_ref, qkv_ref)
    stage_b = functools.partial(_mixer_stage_b, xb_ref, dng_ref, wout_ref, s_ref, o_ref)

    @pl.when(t == 0)
    def _():
        s_ref[...] = jnp.zeros(s_ref.shape, F32)
        cin_ref[...] = jnp.zeros(cin_ref.shape, F32)
        _round_robin(stage_a(write_set))

    @pl.when(jnp.logical_and(t > 0, t < n_tiles))
    def _():
        _round_robin(stage_b(read_set), stage_a(write_set))

    @pl.when(t == n_tiles)
    def _():
        _round_robin(stage_b(read_set))


def _mixer_call(x, mix_norm, w_in, gm_ln_g, gm_ln_b, gm_w_s, gm_b_s, dn_conv_w, dn_a_log,
                dn_dt_bias, dn_norm, w_out):
    batch, seq, _ = x.shape
    n_tiles = seq // MIX_ROWS
    n_main = 2 * GM_WIDTH + 4 * DN_WIDTH
    w_all = w_in.astype(BF16)
    w_ba = jnp.pad(w_in[:, n_main:], ((0, 0), (0, LANES - 2 * DN_HEADS))).astype(BF16)
    w_cat = gm_w_s.reshape(GM_GROUPS // 2, 2, GM_BLOCK, GM_BLOCK).transpose(0, 2, 1, 3)
    w_cat = w_cat.reshape(GM_GROUPS // 2, GM_BLOCK, 2 * GM_BLOCK)
    bias = jnp.repeat(gm_b_s.T, GM_GROUP_DIM, axis=1)
    pad_a = lambda a: jnp.pad(a.reshape(1, DN_HEADS), ((0, 0), (DN_HEADS, LANES - 2 * DN_HEADS)))
    const2 = lambda t: (0, 0)
    tile_a = lambda t: (0, jnp.minimum(t, n_tiles - 1), 0)
    tile_b = lambda t: (0, jnp.maximum(t - 1, 0), 0)
    in_specs = [
        pl.BlockSpec((batch, MIX_ROWS, D_MODEL), tile_a),
        pl.BlockSpec((batch, MIX_ROWS, D_MODEL), tile_b),
        pl.BlockSpec((1, D_MODEL), const2),
        pl.BlockSpec(w_in.shape, const2),
        pl.BlockSpec((D_MODEL, LANES), const2),
        pl.BlockSpec((1, GM_WIDTH), const2),
        pl.BlockSpec((1, GM_WIDTH), const2),
        pl.BlockSpec((GM_GROUPS // 2, GM_BLOCK, 2 * GM_BLOCK), lambda t: (0, 0, 0)),
        pl.BlockSpec((GM_BLOCK, GM_WIDTH), const2),
        pl.BlockSpec((DN_CONV, 3 * DN_WIDTH), const2),
        pl.BlockSpec((1, LANES), const2),
        pl.BlockSpec((1, LANES), const2),
        pl.BlockSpec((1, DN_HEAD_DIM), const2),
        pl.BlockSpec((D_MODEL, D_MODEL), const2),
    ]
    return pl.pallas_call(
        _mixer_body,
        grid=(n_tiles + 1,),
        in_specs=in_specs,
        out_specs=pl.BlockSpec((batch, MIX_ROWS, D_MODEL), tile_b),
        out_shape=jax.ShapeDtypeStruct((batch, seq, D_MODEL), F32),
        scratch_shapes=[
            pltpu.VMEM((batch * DN_HEADS, DN_HEAD_DIM, DN_HEAD_DIM), F32),
            pltpu.VMEM((batch, 3 * DN_HEADS, CONV_TAIL + MIX_ROWS, LANES), F32),
            pltpu.VMEM((batch, 3 * DN_HEADS, MIX_ROWS, LANES), F32),
        ] + _handoff_shapes(batch),
        compiler_params=pltpu.CompilerParams(
            dimension_semantics=("arbitrary",), vmem_limit_bytes=VMEM_LIMIT_BYTES),
        name="mixer",
    )(x, x, mix_norm.reshape(1, D_MODEL), w_all, w_ba, gm_ln_g.reshape(1, GM_WIDTH),
      gm_ln_b.reshape(1, GM_WIDTH), w_cat, bias, dn_conv_w, pad_a(dn_a_log), pad_a(dn_dt_bias),
      dn_norm.reshape(1, DN_HEAD_DIM), w_out.astype(BF16))


def kernel(x, ffn1_norm, ffn1_w_gate, ffn1_w_up, ffn1_w_down, mix_norm, w_in, gm_ln_g, gm_ln_b,
           gm_w_s, gm_b_s, dn_conv_w, dn_a_log, dn_dt_bias, dn_norm, w_out, ffn2_norm,
           ffn2_w_gate, ffn2_w_up, ffn2_w_down, final_norm):
    batch, seq, d = x.shape
    depth = ffn1_norm.shape[0]
    for l in range(depth):
        x = _ffn_call(x.reshape(batch * seq, d), ffn1_norm[l], ffn1_w_gate[l], ffn1_w_up[l],
                      ffn1_w_down[l]).reshape(batch, seq, d)
        x = _mixer_call(x, mix_norm[l], w_in[l], gm_ln_g[l], gm_ln_b[l], gm_w_s[l], gm_b_s[l],
                        dn_conv_w[l], dn_a_log[l], dn_dt_bias[l], dn_norm[l], w_out[l])
        last = l == depth - 1
        x = _ffn_call(x.reshape(batch * seq, d), ffn2_norm[l], ffn2_w_gate[l], ffn2_w_up[l],
                      ffn2_w_down[l], final_norm if last else None).reshape(batch, seq, d)
    return x
```

```python
import functools

import jax
import jax.numpy as jnp
from jax import lax
from jax.experimental import pallas as pl
from jax.experimental.pallas import tpu as pltpu

F32 = jnp.float32
BF16 = jnp.bfloat16
ACC = dict(preferred_element_type=jnp.float32)

D_MODEL = 1024
D_FF = 2816
CHUNK = 64
GM_GROUPS = 8
GM_GROUP_DIM = 64
GM_WIDTH = GM_GROUPS * GM_GROUP_DIM
GM_BLOCK = 128
DN_HEADS = 4
DN_HEAD_DIM = 128
DN_WIDTH = DN_HEADS * DN_HEAD_DIM
DN_CONV = 4
EPS = 1e-6

LANES = 128
SUBLANES = 8
VMEM_LIMIT_BYTES = 60000 * 1024

FFN_ROWS = 512
FFN_COLS = 256
MIX_ROWS = 256
MIX_CHUNKS = MIX_ROWS // CHUNK
CONV_TAIL = SUBLANES
PROJ_COLS = 256
CONV_BLOCK = 64

_NT = (((1,), (1,)), ((), ()))
_TN = (((0,), (0,)), ((), ()))


def _rmsnorm(x, g):
    return (x * lax.rsqrt(jnp.mean(x * x, axis=-1, keepdims=True) + EPS)) * g


def _softplus(x):
    return jnp.maximum(x, 0.0) + jnp.log1p(jnp.exp(-jnp.abs(x)))


def _ffn_body(x_ref, g_ref, wg_hbm, wu_hbm, wd_hbm, *rest, final_norm):
    if final_norm:
        fg_ref, o_ref, wg_ref, wu_ref, wd_ref, stage_gu, stage_d, sem = rest
    else:
        o_ref, wg_ref, wu_ref, wd_ref, stage_gu, stage_d, sem = rest
    n_chunks = D_FF // FFN_COLS

    def chunk_copies(j, slot):
        cols = pl.ds(j * FFN_COLS, FFN_COLS)
        return (pltpu.make_async_copy(wg_hbm.at[:, cols], stage_gu.at[slot, 0], sem.at[slot, 0]),
                pltpu.make_async_copy(wu_hbm.at[:, cols], stage_gu.at[slot, 1], sem.at[slot, 1]),
                pltpu.make_async_copy(wd_hbm.at[cols, :], stage_d.at[slot], sem.at[slot, 2]))

    def run(load_weights):
        x = x_ref[...]
        hb = _rmsnorm(x, g_ref[...]).astype(BF16)
        acc = jnp.zeros(x.shape, F32)
        if load_weights:
            for cp in chunk_copies(0, 0):
                cp.start()
        for j in range(n_chunks):
            cols = slice(j * FFN_COLS, (j + 1) * FFN_COLS)
            if load_weights:
                slot = j % 2
                if j + 1 < n_chunks:
                    for cp in chunk_copies(j + 1, 1 - slot):
                        cp.start()
                for cp in chunk_copies(j, slot):
                    cp.wait()
                wg_ref[:, cols] = stage_gu[slot, 0].astype(BF16)
                wu_ref[:, cols] = stage_gu[slot, 1].astype(BF16)
                wd_ref[cols, :] = stage_d[slot].astype(BF16)
            gate = jnp.dot(hb, wg_ref[:, cols], **ACC)
            up = jnp.dot(hb, wu_ref[:, cols], **ACC)
            act = (jax.nn.silu(gate) * up).astype(BF16)
            acc = acc + jnp.dot(act, wd_ref[cols, :], **ACC)
        y = x + 0.5 * acc
        if final_norm:
            y = _rmsnorm(y, fg_ref[...])
        o_ref[...] = y

    first = pl.program_id(0) == 0
    pl.when(first)(functools.partial(run, True))
    pl.when(jnp.logical_not(first))(functools.partial(run, False))


def _ffn_call(x2d, gain, w_gate, w_up, w_down, final_gain=None):
    rows = x2d.shape[0]
    final_norm = final_gain is not None
    const = lambda i: (0, 0)
    hbm = pl.BlockSpec(memory_space=pl.ANY)
    in_specs = [
        pl.BlockSpec((FFN_ROWS, D_MODEL), lambda i: (i, 0)),
        pl.BlockSpec((1, D_MODEL), const),
        hbm, hbm, hbm,
    ]
    args = [x2d, gain.reshape(1, D_MODEL), w_gate, w_up, w_down]
    if final_norm:
        in_specs.append(pl.BlockSpec((1, D_MODEL), const))
        args.append(final_gain.reshape(1, D_MODEL))
    return pl.pallas_call(
        functools.partial(_ffn_body, final_norm=final_norm),
        grid=(rows // FFN_ROWS,),
        in_specs=in_specs,
        out_specs=pl.BlockSpec((FFN_ROWS, D_MODEL), lambda i: (i, 0)),
        out_shape=jax.ShapeDtypeStruct((rows, D_MODEL), F32),
        scratch_shapes=[
            pltpu.VMEM((D_MODEL, D_FF), BF16),
            pltpu.VMEM((D_MODEL, D_FF), BF16),
            pltpu.VMEM((D_FF, D_MODEL), BF16),
            pltpu.VMEM((2, 2, D_MODEL, FFN_COLS), F32),
            pltpu.VMEM((2, FFN_COLS, D_MODEL), F32),
            pltpu.SemaphoreType.DMA((2, 3)),
        ],
        compiler_params=pltpu.CompilerParams(
            dimension_semantics=("arbitrary",), vmem_limit_bytes=VMEM_LIMIT_BYTES),
        name="ffn_final" if final_norm else "ffn",
    )(*args)


def _split_hi_lo(x):
    hi = x.astype(BF16)
    lo = (x - hi.astype(F32)).astype(BF16)
    return hi, lo


def _lane_bcast(x, col):
    return jnp.broadcast_to(x[:, col:col + 1], (x.shape[0], LANES))


def _col_to_wide(col, lane_lo):
    left = jnp.where(lane_lo, col[0:CHUNK], col[CHUNK:2 * CHUNK])
    right = jnp.where(lane_lo, col[2 * CHUNK:3 * CHUNK], col[3 * CHUNK:])
    return jnp.concatenate([left, right], axis=1)


def _block_diag(wide, bd_mask):
    return jnp.where(bd_mask, jnp.concatenate([wide.astype(F32)] * MIX_CHUNKS, axis=0), 0.0).astype(BF16)


def _handoff_shapes(n_seq):
    n_chain = n_seq * DN_HEADS
    return [
        pltpu.VMEM((2, n_seq * MIX_ROWS, GM_WIDTH), BF16),
        pltpu.VMEM((2, n_seq * MIX_ROWS, DN_WIDTH), F32),
        pltpu.VMEM((2, n_chain, CHUNK, MIX_ROWS), F32),
        pltpu.VMEM((2, n_chain, CHUNK, MIX_ROWS), BF16),
        pltpu.VMEM((2, n_chain, MIX_ROWS, 2 * DN_HEAD_DIM), BF16),
        pltpu.VMEM((2, n_chain, MIX_ROWS, DN_HEAD_DIM), BF16),
        pltpu.VMEM((2, n_chain, MIX_ROWS, DN_HEAD_DIM), BF16),
        pltpu.VMEM((2, n_chain, MIX_CHUNKS * SUBLANES, DN_HEAD_DIM), F32),
    ]


N_HANDOFF = 8


def _mixer_stage_a(x_ref, mg_ref, win_ref, wba_ref, lng_ref, lnb_ref, wcat_ref, bias_ref, cw_ref,
                   alog_ref, dtb_ref, cin_ref, qkv_ref, out):
    ya_ref, zg_ref, m_ref, attn_ref, rhs_ref, qd_ref, kd_ref, egl_ref = out
    n_seq = x_ref.shape[0]
    rows = MIX_ROWS
    all_rows = n_seq * rows
    n_blocks = all_rows // GM_BLOCK

    x = x_ref[...].reshape(all_rows, D_MODEL)
    hb = _rmsnorm(x, mg_ref[...]).astype(BF16)
    yield

    def project(lo, hi):
        parts = []
        for c0 in range(lo, hi, PROJ_COLS):
            parts.append(jnp.dot(hb, win_ref[:, c0:min(c0 + PROJ_COLS, hi)], **ACC))
            yield
        return jnp.concatenate(parts, axis=1)

    for b in range(n_seq):
        for s in range(3 * DN_WIDTH // LANES):
            cin_ref[b, s, 0:CONV_TAIL, :] = cin_ref[b, s, rows:rows + CONV_TAIL, :]
    for c0 in range(0, 3 * DN_WIDTH, PROJ_COLS):
        blk = jnp.dot(hb, win_ref[:, 2 * GM_WIDTH + c0:2 * GM_WIDTH + c0 + PROJ_COLS], **ACC)
        for b in range(n_seq):
            for j in range(PROJ_COLS // LANES):
                cin_ref[b, c0 // LANES + j, CONV_TAIL:, :] = blk[b * rows:(b + 1) * rows, j * LANES:(j + 1) * LANES]
        yield
    pba = jnp.dot(hb, wba_ref[...], **ACC)
    yield
    half = CONV_BLOCK // 2
    for b in range(n_seq):
        for s in range(3 * DN_WIDTH // LANES):
            src, dst = cin_ref.at[b, s], qkv_ref.at[b, s]
            w = cw_ref[:, s * LANES:(s + 1) * LANES]
            for r0 in range(0, rows, CONV_BLOCK):
                tap = lambda off: src[pl.ds(CONV_TAIL + r0 + off, half, stride=2), :]
                e0, e2, o1, om1, om3 = tap(0), tap(-2), tap(1), tap(-1), tap(-3)
                even = w[3:4] * e0 + w[2:3] * om1 + w[1:2] * e2 + w[0:1] * om3
                odd = w[3:4] * o1 + w[2:3] * e0 + w[1:2] * om1 + w[0:1] * e2
                dst[pl.ds(r0, half, stride=2), :] = jax.nn.silu(even)
                dst[pl.ds(r0 + 1, half, stride=2), :] = jax.nn.silu(odd)
    beta_all = jax.nn.sigmoid(pba)
    g_all = -jnp.exp(alog_ref[...]) * _softplus(pba + dtb_ref[...])

    v_a = jax.nn.gelu((yield from project(GM_WIDTH, 2 * GM_WIDTH)))
    zg_ref[...] = jax.nn.silu((yield from project(2 * GM_WIDTH + 3 * DN_WIDTH, 2 * GM_WIDTH + 4 * DN_WIDTH)))

    ri = lax.broadcasted_iota(jnp.int32, (rows, rows), 0)
    ci = lax.broadcasted_iota(jnp.int32, (rows, rows), 1)
    bd_mask = (ri // CHUNK) == (ci // CHUNK)
    tri_b = jnp.where(bd_mask & (ci <= ri), 1.0, 0.0).astype(BF16)
    blk_b = jnp.where(bd_mask, 1.0, 0.0).astype(BF16)
    g_hi, g_lo = _split_hi_lo(jnp.concatenate([g_all[b * rows:(b + 1) * rows] for b in range(n_seq)], axis=1))
    gc_all = jnp.dot(tri_b, g_hi, **ACC) + jnp.dot(tri_b, g_lo, **ACC)
    gl_all = jnp.dot(blk_b, g_hi, **ACC) + jnp.dot(blk_b, g_lo, **ACC)
    yield
    e_gc_all = jnp.exp(gc_all)
    e_glgc_all = jnp.exp(gl_all - gc_all)
    e_gl_all = jnp.exp(gl_all)
    gc_rows = [gc_all[:, b * LANES:(b + 1) * LANES].T for b in range(n_seq)]

    mu = jnp.mean(v_a, axis=-1, keepdims=True)
    vc = v_a - mu
    var = jnp.mean(vc * vc, axis=-1, keepdims=True)
    v_n = (vc * lax.rsqrt(var + EPS)) * lng_ref[...] + lnb_ref[...]

    wi = lax.broadcasted_iota(jnp.int32, (GM_BLOCK, 2 * GM_BLOCK), 0) // CHUNK
    wj = (lax.broadcasted_iota(jnp.int32, (GM_BLOCK, 2 * GM_BLOCK), 1) % GM_BLOCK) // CHUNK
    w_mask = wj <= wi
    lane_lo128 = lax.broadcasted_iota(jnp.int32, (GM_BLOCK, LANES), 1) < GM_GROUP_DIM
    mixed_cols = []
    for p in range(GM_GROUPS // 2):
        v_p = v_n[:, p * LANES:(p + 1) * LANES]
        rhs = []
        for blk_i in range(n_blocks):
            blk = v_p[blk_i * GM_BLOCK:(blk_i + 1) * GM_BLOCK]
            rhs.append(jnp.concatenate(
                [jnp.where(lane_lo128, blk, 0.0), jnp.where(lane_lo128, 0.0, blk)], axis=0))
        rhs = jnp.concatenate(rhs, axis=1).astype(BF16)
        w_pair = jnp.concatenate([wcat_ref[2 * p], wcat_ref[2 * p + 1]], axis=1)
        w_p = jnp.where(w_mask, w_pair, 0.0).astype(BF16)
        res = jnp.dot(w_p, rhs, **ACC)
        mixed_cols.append(jnp.concatenate(
            [res[:, i * LANES:(i + 1) * LANES] for i in range(n_blocks)], axis=0))
        if p % 2 == 1:
            yield
    bias = jnp.concatenate([bias_ref[...]] * n_blocks, axis=0)
    u_a = jax.nn.gelu((yield from project(0, GM_WIDTH)))
    ya_ref[...] = (u_a * (jnp.concatenate(mixed_cols, axis=1) + bias)).astype(BF16)

    wr = lax.broadcasted_iota(jnp.int32, (CHUNK, rows), 0)
    wc = lax.broadcasted_iota(jnp.int32, (CHUNK, rows), 1) % CHUNK
    tri_w = wc <= wr
    strict_w = wc < wr
    lane_lo = lax.broadcasted_iota(jnp.int32, (CHUNK, LANES), 1) < CHUNK
    scale = DN_HEAD_DIM ** -0.5

    zeros_k = jnp.zeros((2 * CHUNK, DN_HEAD_DIM), BF16)
    for b in range(n_seq):
        rs = slice(b * rows, (b + 1) * rows)
        for h0 in range(0, DN_HEADS, 2):
            q, k, k_beta = {}, {}, {}
            for h in (h0, h0 + 1):
                chain = b * DN_HEADS + h
                col = b * LANES + DN_HEADS + h
                q_h, k_h, v_h = qkv_ref[b, h], qkv_ref[b, DN_HEADS + h], qkv_ref[b, 2 * DN_HEADS + h]
                q[h] = q_h * lax.rsqrt(jnp.sum(q_h * q_h, axis=-1, keepdims=True) + EPS) * scale
                k[h] = k_h * lax.rsqrt(jnp.sum(k_h * k_h, axis=-1, keepdims=True) + EPS)
                beta = _lane_bcast(beta_all[rs], h)
                e_gc = _lane_bcast(e_gc_all, col)
                k_beta[h] = k[h] * beta
                rhs_ref[chain] = jnp.concatenate([v_h * beta, k_beta[h] * e_gc], axis=1).astype(BF16)
                qd_ref[chain] = (q[h] * e_gc).astype(BF16)
                kd_ref[chain] = (k[h] * _lane_bcast(e_glgc_all, col)).astype(BF16)
                for ch in range(MIX_CHUNKS):
                    egl_ref[chain, ch * SUBLANES:(ch + 1) * SUBLANES, :] = _lane_bcast(
                        e_gl_all[ch * CHUNK:ch * CHUNK + SUBLANES], col)
            kk_w = {h: [] for h in (h0, h0 + 1)}
            qk_w = {h: [] for h in (h0, h0 + 1)}
            for half_i in range(rows // LANES):
                hr = slice(half_i * LANES, (half_i + 1) * LANES)
                lhs = jnp.concatenate(
                    [jnp.concatenate([k_beta[h][hr], q[h][hr]], axis=0) for h in (h0, h0 + 1)], axis=1)
                kb0, kb1 = k[h0][hr].astype(BF16), k[h0 + 1][hr].astype(BF16)
                rhs_k = jnp.concatenate([jnp.concatenate([kb0, zeros_k], axis=1),
                                         jnp.concatenate([zeros_k, kb1], axis=1)], axis=0)
                out = lax.dot_general(lhs.astype(BF16), rhs_k, _NT, **ACC)
                for i, h in enumerate((h0, h0 + 1)):
                    kk = out[0:LANES, i * LANES:(i + 1) * LANES]
                    qk = out[LANES:, i * LANES:(i + 1) * LANES]
                    kk_w[h].append(jnp.where(lane_lo, kk[0:CHUNK], kk[CHUNK:]))
                    qk_w[h].append(jnp.where(lane_lo, qk[0:CHUNK], qk[CHUNK:]))
            for h in (h0, h0 + 1):
                chain = b * DN_HEADS + h
                col = b * LANES + DN_HEADS + h
                dlog = (_col_to_wide(_lane_bcast(gc_all, col), lane_lo)
                        - jnp.broadcast_to(gc_rows[b][DN_HEADS + h:DN_HEADS + h + 1, :], (CHUNK, rows)))
                decay = jnp.exp(jnp.where(tri_w, dlog, -jnp.inf))
                m_ref[chain] = -jnp.where(strict_w, jnp.concatenate(kk_w[h], axis=1) * decay, 0.0)
                attn_ref[chain] = (jnp.concatenate(qk_w[h], axis=1) * decay).astype(BF16)
            yield


def _mixer_stage_b(x_ref, dng_ref, wout_ref, s_ref, o_ref, inp):
    ya_ref, zg_ref, m_ref, attn_ref, rhs_ref, qd_ref, kd_ref, egl_ref = inp
    n_seq = x_ref.shape[0]
    rows = MIX_ROWS
    all_rows = n_seq * rows
    chains = range(n_seq * DN_HEADS)

    ri = lax.broadcasted_iota(jnp.int32, (rows, rows), 0)
    ci = lax.broadcasted_iota(jnp.int32, (rows, rows), 1)
    bd_mask = (ri // CHUNK) == (ci // CHUNK)
    wr = lax.broadcasted_iota(jnp.int32, (CHUNK, rows), 0)
    wc = lax.broadcasted_iota(jnp.int32, (CHUNK, rows), 1) % CHUNK
    eye_w = jnp.where(wc == wr, 1.0, 0.0)

    m_w = {c: m_ref[c] for c in chains}
    inv_w = {c: eye_w + m_w[c] for c in chains}
    m_bd = {c: _block_diag(m_w[c], bd_mask) for c in chains}
    for _ in range(5):
        for c in chains:
            m_w[c] = jnp.dot(m_w[c].astype(BF16), m_bd[c], **ACC)
        yield
        for c in chains:
            m_bd[c] = _block_diag(m_w[c], bd_mask)
        for c in chains:
            inv_w[c] = inv_w[c] + jnp.dot(inv_w[c].astype(BF16), m_bd[c], **ACC)
        yield
    uw = {c: jnp.dot(_block_diag(inv_w[c], bd_mask), rhs_ref[c], **ACC) for c in chains}
    yield

    state = {c: s_ref[c] for c in chains}
    v_new = {c: [] for c in chains}
    o_inter = {c: [] for c in chains}
    pairs = [(c, c + 1) for c in range(0, n_seq * DN_HEADS, 2)]
    zeros_s = jnp.zeros((DN_HEAD_DIM, DN_HEAD_DIM), BF16)
    zeros_v = jnp.zeros((CHUNK, DN_HEAD_DIM), BF16)

    def block_diag2(a, b, zeros):
        return jnp.concatenate([jnp.concatenate([a, zeros], axis=1), jnp.concatenate([zeros, b], axis=1)], axis=0)

    for ch in range(MIX_CHUNKS):
        cs = slice(ch * CHUNK, (ch + 1) * CHUNK)
        r = {}
        for pair in pairs:
            wq = jnp.concatenate(
                [jnp.concatenate([uw[c][cs, DN_HEAD_DIM:].astype(BF16), qd_ref[c, cs, :]], axis=0) for c in pair],
                axis=1)
            s_bd = block_diag2(state[pair[0]].astype(BF16), state[pair[1]].astype(BF16), zeros_s)
            r[pair] = jnp.dot(wq, s_bd, **ACC)
        yield
        for pair in pairs:
            vn = {}
            for i, c in enumerate(pair):
                r_c = r[pair][:, i * DN_HEAD_DIM:(i + 1) * DN_HEAD_DIM]
                vn[c] = uw[c][cs, :DN_HEAD_DIM] - r_c[:CHUNK]
                v_new[c].append(vn[c])
                o_inter[c].append(r_c[CHUNK:])
            kd = jnp.concatenate([kd_ref[c, cs, :] for c in pair], axis=0)
            vn_bd = block_diag2(vn[pair[0]].astype(BF16), vn[pair[1]].astype(BF16), zeros_v)
            upd = lax.dot_general(kd, vn_bd, _TN, **ACC)
            for i, c in enumerate(pair):
                decay_s = jnp.concatenate([egl_ref[c, ch * SUBLANES:(ch + 1) * SUBLANES, :]] * (DN_HEAD_DIM // SUBLANES), axis=0)
                state[c] = state[c] * decay_s + upd[:, i * DN_HEAD_DIM:(i + 1) * DN_HEAD_DIM]
        yield
    y_b = []
    for b in range(n_seq):
        heads = []
        for h in range(DN_HEADS):
            c = b * DN_HEADS + h
            s_ref[c] = state[c]
            o = (jnp.concatenate(o_inter[c], axis=0)
                 + jnp.dot(_block_diag(attn_ref[c], bd_mask),
                           jnp.concatenate(v_new[c], axis=0).astype(BF16), **ACC))
            heads.append(_rmsnorm(o, dng_ref[...])
                         * zg_ref[b * rows:(b + 1) * rows, h * DN_HEAD_DIM:(h + 1) * DN_HEAD_DIM])
        y_b.append(jnp.concatenate(heads, axis=1).astype(BF16))
        yield
    y = jnp.concatenate([ya_ref[...], jnp.concatenate(y_b, axis=0)], axis=1)
    for c0 in range(0, D_MODEL, PROJ_COLS):
        cols = slice(c0, c0 + PROJ_COLS)
        x = x_ref[:, :, cols].reshape(all_rows, PROJ_COLS)
        o_ref[:, :, cols] = (x + jnp.dot(y, wout_ref[:, cols], **ACC)).reshape(n_seq, rows, PROJ_COLS)
        yield


def _round_robin(*stages):
    stages = list(stages)
    while stages:
        for stage in list(stages):
            try:
                next(stage)
            except StopIteration:
                stages.remove(stage)


def _mixer_body(xa_ref, xb_ref, mg_ref, win_ref, wba_ref, lng_ref, lnb_ref, wcat_ref, bias_ref,
                cw_ref, alog_ref, dtb_ref, dng_ref, wout_ref, o_ref, s_ref, cin_ref, qkv_ref, *handoff):
    t = pl.program_id(0)
    n_tiles = pl.num_programs(0) - 1
    slot = lax.rem(t, 2)
    write_set = [ref.at[slot] for ref in handoff]
    read_set = [ref.at[1 - slot] for ref in handoff]
    stage_a = functools.partial(_mixer_stage_a, xa_ref, mg_ref, win_ref, wba_ref, lng_ref, lnb_ref, wcat_ref,
                                bias_ref, cw_ref, alog_ref, dtb_ref, cin_ref, qkv_ref)
    stage_b = functools.partial(_mixer_stage_b, xb_ref, dng_ref, wout_ref, s_ref, o_ref)

    @pl.when(t == 0)
    def _():
        s_ref[...] = jnp.zeros(s_ref.shape, F32)
        cin_ref[...] = jnp.zeros(cin_ref.shape, F32)
        _round_robin(stage_a(write_set))

    @pl.when(jnp.logical_and(t > 0, t < n_tiles))
    def _():
        _round_robin(stage_b(read_set), stage_a(write_set))

    @pl.when(t == n_tiles)
    def _():
        _round_robin(stage_b(read_set))


def _mixer_call(x, mix_norm, w_in, gm_ln_g, gm_ln_b, gm_w_s, gm_b_s, dn_conv_w, dn_a_log,
                dn_dt_bias, dn_norm, w_out):
    batch, seq, _ = x.shape
    n_tiles = seq // MIX_ROWS
    n_main = 2 * GM_WIDTH + 4 * DN_WIDTH
    w_all = w_in.astype(BF16)
    w_ba = jnp.pad(w_in[:, n_main:], ((0, 0), (0, LANES - 2 * DN_HEADS))).astype(BF16)
    bias = jnp.repeat(gm_b_s.T, GM_GROUP_DIM, axis=1)
    pad_a = lambda a: jnp.pad(a.reshape(1, DN_HEADS), ((0, 0), (DN_HEADS, LANES - 2 * DN_HEADS)))
    const2 = lambda t: (0, 0)
    tile_a = lambda t: (0, jnp.minimum(t, n_tiles - 1), 0)
    tile_b = lambda t: (0, jnp.maximum(t - 1, 0), 0)
    in_specs = [
        pl.BlockSpec((batch, MIX_ROWS, D_MODEL), tile_a),
        pl.BlockSpec((batch, MIX_ROWS, D_MODEL), tile_b),
        pl.BlockSpec((1, D_MODEL), const2),
        pl.BlockSpec(w_in.shape, const2),
        pl.BlockSpec((D_MODEL, LANES), const2),
        pl.BlockSpec((1, GM_WIDTH), const2),
        pl.BlockSpec((1, GM_WIDTH), const2),
        pl.BlockSpec((GM_GROUPS, GM_BLOCK, GM_BLOCK), lambda t: (0, 0, 0)),
        pl.BlockSpec((GM_BLOCK, GM_WIDTH), const2),
        pl.BlockSpec((DN_CONV, 3 * DN_WIDTH), const2),
        pl.BlockSpec((1, LANES), const2),
        pl.BlockSpec((1, LANES), const2),
        pl.BlockSpec((1, DN_HEAD_DIM), const2),
        pl.BlockSpec((D_MODEL, D_MODEL), const2),
    ]
    return pl.pallas_call(
        _mixer_body,
        grid=(n_tiles + 1,),
        in_specs=in_specs,
        out_specs=pl.BlockSpec((batch, MIX_ROWS, D_MODEL), tile_b),
        out_shape=jax.ShapeDtypeStruct((batch, seq, D_MODEL), F32),
        scratch_shapes=[
            pltpu.VMEM((batch * DN_HEADS, DN_HEAD_DIM, DN_HEAD_DIM), F32),
            pltpu.VMEM((batch, 3 * DN_HEADS, CONV_TAIL + MIX_ROWS, LANES), F32),
            pltpu.VMEM((batch, 3 * DN_HEADS, MIX_ROWS, LANES), F32),
        ] + _handoff_shapes(batch),
        compiler_params=pltpu.CompilerParams(
            dimension_semantics=("arbitrary",), vmem_limit_bytes=VMEM_LIMIT_BYTES),
        name="mixer",
    )(x, x, mix_norm.reshape(1, D_MODEL), w_all, w_ba, gm_ln_g.reshape(1, GM_WIDTH),
      gm_ln_b.reshape(1, GM_WIDTH), gm_w_s, bias, dn_conv_w, pad_a(dn_a_log), pad_a(dn_dt_bias),
      dn_norm.reshape(1, DN_HEAD_DIM), w_out.astype(BF16))


def kernel(x, ffn1_norm, ffn1_w_gate, ffn1_w_up, ffn1_w_down, mix_norm, w_in, gm_ln_g, gm_ln_b,
           gm_w_s, gm_b_s, dn_conv_w, dn_a_log, dn_dt_bias, dn_norm, w_out, ffn2_norm,
           ffn2_w_gate, ffn2_w_up, ffn2_w_down, final_norm):
    batch, seq, d = x.shape
    depth = ffn1_norm.shape[0]
    for l in range(depth):
        x = _ffn_call(x.reshape(batch * seq, d), ffn1_norm[l], ffn1_w_gate[l], ffn1_w_up[l],
                      ffn1_w_down[l]).reshape(batch, seq, d)
        x = _mixer_call(x, mix_norm[l], w_in[l], gm_ln_g[l], gm_ln_b[l], gm_w_s[l], gm_b_s[l],
                        dn_conv_w[l], dn_a_log[l], dn_dt_bias[l], dn_norm[l], w_out[l])
        last = l == depth - 1
        x = _ffn_call(x.reshape(batch * seq, d), ffn2_norm[l], ffn2_w_gate[l], ffn2_w_up[l],
                      ffn2_w_down[l], final_norm if last else None).reshape(batch, seq, d)
    return x
```

```python
import functools

import jax
import jax.numpy as jnp
from jax import lax
from jax.experimental import pallas as pl
from jax.experimental.pallas import tpu as pltpu

F32 = jnp.float32
BF16 = jnp.bfloat16
ACC = dict(preferred_element_type=jnp.float32)

D_MODEL = 1024
D_FF = 2816
CHUNK = 64
GM_GROUPS = 8
GM_GROUP_DIM = 64
GM_WIDTH = GM_GROUPS * GM_GROUP_DIM
GM_BLOCK = 128
DN_HEADS = 4
DN_HEAD_DIM = 128
DN_WIDTH = DN_HEADS * DN_HEAD_DIM
DN_CONV = 4
EPS = 1e-6

LANES = 128
SUBLANES = 8
VMEM_LIMIT_BYTES = 60000 * 1024

FFN_ROWS = 512
FFN_COLS = 256
MIX_ROWS = 256
MIX_CHUNKS = MIX_ROWS // CHUNK
CONV_TAIL = SUBLANES
PROJ_COLS = 256
CONV_BLOCK = 64

_NT = (((1,), (1,)), ((), ()))
_TN = (((0,), (0,)), ((), ()))


def _rmsnorm(x, g):
    return (x * lax.rsqrt(jnp.mean(x * x, axis=-1, keepdims=True) + EPS)) * g


def _softplus(x):
    return jnp.maximum(x, 0.0) + jnp.log1p(jnp.exp(-jnp.abs(x)))


def _ffn_body(x_ref, g_ref, wg_hbm, wu_hbm, wd_hbm, *rest, final_norm):
    if final_norm:
        fg_ref, o_ref, wg_ref, wu_ref, wd_ref, stage_gu, stage_d, sem = rest
    else:
        o_ref, wg_ref, wu_ref, wd_ref, stage_gu, stage_d, sem = rest
    n_chunks = D_FF // FFN_COLS

    def chunk_copies(j, slot):
        cols = pl.ds(j * FFN_COLS, FFN_COLS)
        return (pltpu.make_async_copy(wg_hbm.at[:, cols], stage_gu.at[slot, 0], sem.at[slot, 0]),
                pltpu.make_async_copy(wu_hbm.at[:, cols], stage_gu.at[slot, 1], sem.at[slot, 1]),
                pltpu.make_async_copy(wd_hbm.at[cols, :], stage_d.at[slot], sem.at[slot, 2]))

    def run(load_weights):
        x = x_ref[...]
        hb = _rmsnorm(x, g_ref[...]).astype(BF16)
        acc = jnp.zeros(x.shape, F32)
        if load_weights:
            for cp in chunk_copies(0, 0):
                cp.start()
        for j in range(n_chunks):
            cols = slice(j * FFN_COLS, (j + 1) * FFN_COLS)
            if load_weights:
                slot = j % 2
                if j + 1 < n_chunks:
                    for cp in chunk_copies(j + 1, 1 - slot):
                        cp.start()
                for cp in chunk_copies(j, slot):
                    cp.wait()
                wg_ref[:, cols] = stage_gu[slot, 0].astype(BF16)
                wu_ref[:, cols] = stage_gu[slot, 1].astype(BF16)
                wd_ref[cols, :] = stage_d[slot].astype(BF16)
            gate = jnp.dot(hb, wg_ref[:, cols], **ACC)
            up = jnp.dot(hb, wu_ref[:, cols], **ACC)
            act = (jax.nn.silu(gate) * up).astype(BF16)
            acc = acc + jnp.dot(act, wd_ref[cols, :], **ACC)
        y = x + 0.5 * acc
        if final_norm:
            y = _rmsnorm(y, fg_ref[...])
        o_ref[...] = y

    first = pl.program_id(0) == 0
    pl.when(first)(functools.partial(run, True))
    pl.when(jnp.logical_not(first))(functools.partial(run, False))


def _ffn_call(x2d, gain, w_gate, w_up, w_down, final_gain=None):
    rows = x2d.shape[0]
    final_norm = final_gain is not None
    const = lambda i: (0, 0)
    hbm = pl.BlockSpec(memory_space=pl.ANY)
    in_specs = [
        pl.BlockSpec((FFN_ROWS, D_MODEL), lambda i: (i, 0)),
        pl.BlockSpec((1, D_MODEL), const),
        hbm, hbm, hbm,
    ]
    args = [x2d, gain.reshape(1, D_MODEL), w_gate, w_up, w_down]
    if final_norm:
        in_specs.append(pl.BlockSpec((1, D_MODEL), const))
        args.append(final_gain.reshape(1, D_MODEL))
    return pl.pallas_call(
        functools.partial(_ffn_body, final_norm=final_norm),
        grid=(rows // FFN_ROWS,),
        in_specs=in_specs,
        out_specs=pl.BlockSpec((FFN_ROWS, D_MODEL), lambda i: (i, 0)),
        out_shape=jax.ShapeDtypeStruct((rows, D_MODEL), F32),
        scratch_shapes=[
            pltpu.VMEM((D_MODEL, D_FF), BF16),
            pltpu.VMEM((D_MODEL, D_FF), BF16),
            pltpu.VMEM((D_FF, D_MODEL), BF16),
            pltpu.VMEM((2, 2, D_MODEL, FFN_COLS), F32),
            pltpu.VMEM((2, FFN_COLS, D_MODEL), F32),
            pltpu.SemaphoreType.DMA((2, 3)),
        ],
        compiler_params=pltpu.CompilerParams(
            dimension_semantics=("arbitrary",), vmem_limit_bytes=VMEM_LIMIT_BYTES),
        name="ffn_final" if final_norm else "ffn",
    )(*args)


STREAM_ROWS = 1024


def _ffn_stream_body(x_ref, g_ref, wg_ref, wu_ref, wd_ref, *rest, final_norm):
    if final_norm:
        fg_ref, o_ref, hb_ref, acc_ref = rest
    else:
        o_ref, hb_ref, acc_ref = rest
    j = pl.program_id(1)

    @pl.when(j == 0)
    def _():
        hb_ref[...] = _rmsnorm(x_ref[...], g_ref[...]).astype(BF16)
        acc_ref[...] = jnp.zeros(acc_ref.shape, F32)

    hb = hb_ref[...]
    gate = jnp.dot(hb, wg_ref[...].astype(BF16), **ACC)
    up = jnp.dot(hb, wu_ref[...].astype(BF16), **ACC)
    act = (jax.nn.silu(gate) * up).astype(BF16)
    acc_ref[...] += jnp.dot(act, wd_ref[...].astype(BF16), **ACC)

    @pl.when(j == pl.num_programs(1) - 1)
    def _():
        y = x_ref[...] + 0.5 * acc_ref[...]
        if final_norm:
            y = _rmsnorm(y, fg_ref[...])
        o_ref[...] = y


def _ffn_stream_call(x2d, gain, w_gate, w_up, w_down, final_gain=None):
    rows = x2d.shape[0]
    final_norm = final_gain is not None
    in_specs = [
        pl.BlockSpec((STREAM_ROWS, D_MODEL), lambda i, j: (i, 0)),
        pl.BlockSpec((1, D_MODEL), lambda i, j: (0, 0)),
        pl.BlockSpec((D_MODEL, FFN_COLS), lambda i, j: (0, j)),
        pl.BlockSpec((D_MODEL, FFN_COLS), lambda i, j: (0, j)),
        pl.BlockSpec((FFN_COLS, D_MODEL), lambda i, j: (j, 0)),
    ]
    args = [x2d, gain.reshape(1, D_MODEL), w_gate, w_up, w_down]
    if final_norm:
        in_specs.append(pl.BlockSpec((1, D_MODEL), lambda i, j: (0, 0)))
        args.append(final_gain.reshape(1, D_MODEL))
    return pl.pallas_call(
        functools.partial(_ffn_stream_body, final_norm=final_norm),
        grid=(rows // STREAM_ROWS, D_FF // FFN_COLS),
        in_specs=in_specs,
        out_specs=pl.BlockSpec((STREAM_ROWS, D_MODEL), lambda i, j: (i, 0)),
        out_shape=jax.ShapeDtypeStruct((rows, D_MODEL), F32),
        scratch_shapes=[pltpu.VMEM((STREAM_ROWS, D_MODEL), BF16), pltpu.VMEM((STREAM_ROWS, D_MODEL), F32)],
        compiler_params=pltpu.CompilerParams(
            dimension_semantics=("arbitrary", "arbitrary"), vmem_limit_bytes=VMEM_LIMIT_BYTES),
        name="ffn_final" if final_norm else "ffn",
    )(*args)


def _split_hi_lo(x):
    hi = x.astype(BF16)
    lo = (x - hi.astype(F32)).astype(BF16)
    return hi, lo


def _lane_bcast(x, col):
    return jnp.broadcast_to(x[:, col:col + 1], (x.shape[0], LANES))


def _col_to_wide(col, lane_lo):
    left = jnp.where(lane_lo, col[0:CHUNK], col[CHUNK:2 * CHUNK])
    right = jnp.where(lane_lo, col[2 * CHUNK:3 * CHUNK], col[3 * CHUNK:])
    return jnp.concatenate([left, right], axis=1)


def _block_diag(wide, bd_mask):
    return jnp.where(bd_mask, jnp.concatenate([wide.astype(F32)] * MIX_CHUNKS, axis=0), 0.0).astype(BF16)


def _handoff_shapes(n_seq):
    n_chain = n_seq * DN_HEADS
    return [
        pltpu.VMEM((2, n_seq * MIX_ROWS, GM_WIDTH), BF16),
        pltpu.VMEM((2, n_seq * MIX_ROWS, DN_WIDTH), F32),
        pltpu.VMEM((2, n_chain, CHUNK, MIX_ROWS), F32),
        pltpu.VMEM((2, n_chain, CHUNK, MIX_ROWS), BF16),
        pltpu.VMEM((2, n_chain, MIX_ROWS, 2 * DN_HEAD_DIM), BF16),
        pltpu.VMEM((2, n_chain, MIX_ROWS, DN_HEAD_DIM), BF16),
        pltpu.VMEM((2, n_chain, MIX_ROWS, DN_HEAD_DIM), BF16),
        pltpu.VMEM((2, n_chain, MIX_CHUNKS * SUBLANES, DN_HEAD_DIM), F32),
    ]


N_HANDOFF = 8


def _mixer_stage_a(x_ref, mg_ref, win_ref, wba_ref, lng_ref, lnb_ref, wcat_ref, bias_ref, cw_ref,
                   alog_ref, dtb_ref, cin_ref, qkv_ref, out):
    ya_ref, zg_ref, m_ref, attn_ref, rhs_ref, qd_ref, kd_ref, egl_ref = out
    n_seq = x_ref.shape[0]
    rows = MIX_ROWS
    all_rows = n_seq * rows
    n_blocks = all_rows // GM_BLOCK

    x = x_ref[...].reshape(all_rows, D_MODEL)
    hb = _rmsnorm(x, mg_ref[...]).astype(BF16)
    yield

    def project(lo, hi):
        parts = []
        for c0 in range(lo, hi, PROJ_COLS):
            parts.append(jnp.dot(hb, win_ref[:, c0:min(c0 + PROJ_COLS, hi)], **ACC))
            yield
        return jnp.concatenate(parts, axis=1)

    for b in range(n_seq):
        for s in range(3 * DN_WIDTH // LANES):
            cin_ref[b, s, 0:CONV_TAIL, :] = cin_ref[b, s, rows:rows + CONV_TAIL, :]
    for c0 in range(0, 3 * DN_WIDTH, PROJ_COLS):
        blk = jnp.dot(hb, win_ref[:, 2 * GM_WIDTH + c0:2 * GM_WIDTH + c0 + PROJ_COLS], **ACC)
        for b in range(n_seq):
            for j in range(PROJ_COLS // LANES):
                cin_ref[b, c0 // LANES + j, CONV_TAIL:, :] = blk[b * rows:(b + 1) * rows, j * LANES:(j + 1) * LANES]
        yield
    pba = jnp.dot(hb, wba_ref[...], **ACC)
    yield
    half = CONV_BLOCK // 2
    for b in range(n_seq):
        for s in range(3 * DN_WIDTH // LANES):
            src, dst = cin_ref.at[b, s], qkv_ref.at[b, s]
            w = cw_ref[:, s * LANES:(s + 1) * LANES]
            for r0 in range(0, rows, CONV_BLOCK):
                tap = lambda off: src[pl.ds(CONV_TAIL + r0 + off, half, stride=2), :]
                e0, e2, o1, om1, om3 = tap(0), tap(-2), tap(1), tap(-1), tap(-3)
                even = w[3:4] * e0 + w[2:3] * om1 + w[1:2] * e2 + w[0:1] * om3
                odd = w[3:4] * o1 + w[2:3] * e0 + w[1:2] * om1 + w[0:1] * e2
                dst[pl.ds(r0, half, stride=2), :] = jax.nn.silu(even)
                dst[pl.ds(r0 + 1, half, stride=2), :] = jax.nn.silu(odd)
    beta_all = jax.nn.sigmoid(pba)
    g_all = -jnp.exp(alog_ref[...]) * _softplus(pba + dtb_ref[...])

    v_a = jax.nn.gelu((yield from project(GM_WIDTH, 2 * GM_WIDTH)))
    zg_ref[...] = jax.nn.silu((yield from project(2 * GM_WIDTH + 3 * DN_WIDTH, 2 * GM_WIDTH + 4 * DN_WIDTH)))

    ri = lax.broadcasted_iota(jnp.int32, (rows, rows), 0)
    ci = lax.broadcasted_iota(jnp.int32, (rows, rows), 1)
    bd_mask = (ri // CHUNK) == (ci // CHUNK)
    tri_b = jnp.where(bd_mask & (ci <= ri), 1.0, 0.0).astype(BF16)
    blk_b = jnp.where(bd_mask, 1.0, 0.0).astype(BF16)
    g_hi, g_lo = _split_hi_lo(jnp.concatenate([g_all[b * rows:(b + 1) * rows] for b in range(n_seq)], axis=1))
    gc_all = jnp.dot(tri_b, g_hi, **ACC) + jnp.dot(tri_b, g_lo, **ACC)
    gl_all = jnp.dot(blk_b, g_hi, **ACC) + jnp.dot(blk_b, g_lo, **ACC)
    yield
    e_gc_all = jnp.exp(gc_all)
    e_glgc_all = jnp.exp(gl_all - gc_all)
    e_gl_all = jnp.exp(gl_all)
    gc_rows = [gc_all[:, b * LANES:(b + 1) * LANES].T for b in range(n_seq)]

    mu = jnp.mean(v_a, axis=-1, keepdims=True)
    vc = v_a - mu
    var = jnp.mean(vc * vc, axis=-1, keepdims=True)
    v_n = (vc * lax.rsqrt(var + EPS)) * lng_ref[...] + lnb_ref[...]

    wi = lax.broadcasted_iota(jnp.int32, (GM_BLOCK, 2 * GM_BLOCK), 0) // CHUNK
    wj = (lax.broadcasted_iota(jnp.int32, (GM_BLOCK, 2 * GM_BLOCK), 1) % GM_BLOCK) // CHUNK
    w_mask = wj <= wi
    lane_lo128 = lax.broadcasted_iota(jnp.int32, (GM_BLOCK, LANES), 1) < GM_GROUP_DIM
    mixed_cols = []
    for p in range(GM_GROUPS // 2):
        v_p = v_n[:, p * LANES:(p + 1) * LANES]
        rhs = []
        for blk_i in range(n_blocks):
            blk = v_p[blk_i * GM_BLOCK:(blk_i + 1) * GM_BLOCK]
            rhs.append(jnp.concatenate(
                [jnp.where(lane_lo128, blk, 0.0), jnp.where(lane_lo128, 0.0, blk)], axis=0))
        rhs = jnp.concatenate(rhs, axis=1).astype(BF16)
        w_pair = jnp.concatenate([wcat_ref[2 * p], wcat_ref[2 * p + 1]], axis=1)
        w_p = jnp.where(w_mask, w_pair, 0.0).astype(BF16)
        res = jnp.dot(w_p, rhs, **ACC)
        mixed_cols.append(jnp.concatenate(
            [res[:, i * LANES:(i + 1) * LANES] for i in range(n_blocks)], axis=0))
        if p % 2 == 1:
            yield
    bias = jnp.concatenate([bias_ref[...]] * n_blocks, axis=0)
    u_a = jax.nn.gelu((yield from project(0, GM_WIDTH)))
    ya_ref[...] = (u_a * (jnp.concatenate(mixed_cols, axis=1) + bias)).astype(BF16)

    wr = lax.broadcasted_iota(jnp.int32, (CHUNK, rows), 0)
    wc = lax.broadcasted_iota(jnp.int32, (CHUNK, rows), 1) % CHUNK
    tri_w = wc <= wr
    strict_w = wc < wr
    lane_lo = lax.broadcasted_iota(jnp.int32, (CHUNK, LANES), 1) < CHUNK
    scale = DN_HEAD_DIM ** -0.5

    zeros_k = jnp.zeros((2 * CHUNK, DN_HEAD_DIM), BF16)
    for b in range(n_seq):
        rs = slice(b * rows, (b + 1) * rows)
        for h0 in range(0, DN_HEADS, 2):
            q, k, k_beta = {}, {}, {}
            for h in (h0, h0 + 1):
                chain = b * DN_HEADS + h
                col = b * LANES + DN_HEADS + h
                q_h, k_h, v_h = qkv_ref[b, h], qkv_ref[b, DN_HEADS + h], qkv_ref[b, 2 * DN_HEADS + h]
                q[h] = q_h * lax.rsqrt(jnp.sum(q_h * q_h, axis=-1, keepdims=True) + EPS) * scale
                k[h] = k_h * lax.rsqrt(jnp.sum(k_h * k_h, axis=-1, keepdims=True) + EPS)
                beta = _lane_bcast(beta_all[rs], h)
                e_gc = _lane_bcast(e_gc_all, col)
                k_beta[h] = k[h] * beta
                rhs_ref[chain] = jnp.concatenate([v_h * beta, k_beta[h] * e_gc], axis=1).astype(BF16)
                qd_ref[chain] = (q[h] * e_gc).astype(BF16)
                kd_ref[chain] = (k[h] * _lane_bcast(e_glgc_all, col)).astype(BF16)
                for ch in range(MIX_CHUNKS):
                    egl_ref[chain, ch * SUBLANES:(ch + 1) * SUBLANES, :] = _lane_bcast(
                        e_gl_all[ch * CHUNK:ch * CHUNK + SUBLANES], col)
            kk_w = {h: [] for h in (h0, h0 + 1)}
            qk_w = {h: [] for h in (h0, h0 + 1)}
            for half_i in range(rows // LANES):
                hr = slice(half_i * LANES, (half_i + 1) * LANES)
                lhs = jnp.concatenate(
                    [jnp.concatenate([k_beta[h][hr], q[h][hr]], axis=0) for h in (h0, h0 + 1)], axis=1)
                kb0, kb1 = k[h0][hr].astype(BF16), k[h0 + 1][hr].astype(BF16)
                rhs_k = jnp.concatenate([jnp.concatenate([kb0, zeros_k], axis=1),
                                         jnp.concatenate([zeros_k, kb1], axis=1)], axis=0)
                out = lax.dot_general(lhs.astype(BF16), rhs_k, _NT, **ACC)
                for i, h in enumerate((h0, h0 + 1)):
                    kk = out[0:LANES, i * LANES:(i + 1) * LANES]
                    qk = out[LANES:, i * LANES:(i + 1) * LANES]
                    kk_w[h].append(jnp.where(lane_lo, kk[0:CHUNK], kk[CHUNK:]))
                    qk_w[h].append(jnp.where(lane_lo, qk[0:CHUNK], qk[CHUNK:]))
            for h in (h0, h0 + 1):
                chain = b * DN_HEADS + h
                col = b * LANES + DN_HEADS + h
                dlog = (_col_to_wide(_lane_bcast(gc_all, col), lane_lo)
                        - jnp.broadcast_to(gc_rows[b][DN_HEADS + h:DN_HEADS + h + 1, :], (CHUNK, rows)))
                decay = jnp.exp(jnp.where(tri_w, dlog, -jnp.inf))
                m_ref[chain] = -jnp.where(strict_w, jnp.concatenate(kk_w[h], axis=1) * decay, 0.0)
                attn_ref[chain] = (jnp.concatenate(qk_w[h], axis=1) * decay).astype(BF16)
            yield


def _mixer_stage_b(x_ref, dng_ref, wout_ref, s_ref, o_ref, inp):
    ya_ref, zg_ref, m_ref, attn_ref, rhs_ref, qd_ref, kd_ref, egl_ref = inp
    n_seq = x_ref.shape[0]
    rows = MIX_ROWS
    all_rows = n_seq * rows
    chains = range(n_seq * DN_HEADS)

    ri = lax.broadcasted_iota(jnp.int32, (rows, rows), 0)
    ci = lax.broadcasted_iota(jnp.int32, (rows, rows), 1)
    bd_mask = (ri // CHUNK) == (ci // CHUNK)
    wr = lax.broadcasted_iota(jnp.int32, (CHUNK, rows), 0)
    wc = lax.broadcasted_iota(jnp.int32, (CHUNK, rows), 1) % CHUNK
    eye_w = jnp.where(wc == wr, 1.0, 0.0)

    m_w = {c: m_ref[c] for c in chains}
    inv_w = {c: eye_w + m_w[c] for c in chains}
    m_bd = {c: _block_diag(m_w[c], bd_mask) for c in chains}
    for _ in range(5):
        for c in chains:
            m_w[c] = jnp.dot(m_w[c].astype(BF16), m_bd[c], **ACC)
        yield
        for c in chains:
            m_bd[c] = _block_diag(m_w[c], bd_mask)
        for c in chains:
            inv_w[c] = inv_w[c] + jnp.dot(inv_w[c].astype(BF16), m_bd[c], **ACC)
        yield
    uw = {c: jnp.dot(_block_diag(inv_w[c], bd_mask), rhs_ref[c], **ACC) for c in chains}
    yield

    state = {c: s_ref[c] for c in chains}
    v_new = {c: [] for c in chains}
    o_inter = {c: [] for c in chains}
    pairs = [(c, c + 1) for c in range(0, n_seq * DN_HEADS, 2)]
    zeros_s = jnp.zeros((DN_HEAD_DIM, DN_HEAD_DIM), BF16)
    zeros_v = jnp.zeros((CHUNK, DN_HEAD_DIM), BF16)

    def block_diag2(a, b, zeros):
        return jnp.concatenate([jnp.concatenate([a, zeros], axis=1), jnp.concatenate([zeros, b], axis=1)], axis=0)

    for ch in range(MIX_CHUNKS):
        cs = slice(ch * CHUNK, (ch + 1) * CHUNK)
        r = {}
        for pair in pairs:
            wq = jnp.concatenate(
                [jnp.concatenate([uw[c][cs, DN_HEAD_DIM:].astype(BF16), qd_ref[c, cs, :]], axis=0) for c in pair],
                axis=1)
            s_bd = block_diag2(state[pair[0]].astype(BF16), state[pair[1]].astype(BF16), zeros_s)
            r[pair] = jnp.dot(wq, s_bd, **ACC)
        yield
        for pair in pairs:
            vn = {}
            for i, c in enumerate(pair):
                r_c = r[pair][:, i * DN_HEAD_DIM:(i + 1) * DN_HEAD_DIM]
                vn[c] = uw[c][cs, :DN_HEAD_DIM] - r_c[:CHUNK]
                v_new[c].append(vn[c])
                o_inter[c].append(r_c[CHUNK:])
            kd = jnp.concatenate([kd_ref[c, cs, :] for c in pair], axis=0)
            vn_bd = block_diag2(vn[pair[0]].astype(BF16), vn[pair[1]].astype(BF16), zeros_v)
            upd = lax.dot_general(kd, vn_bd, _TN, **ACC)
            for i, c in enumerate(pair):
                decay_s = jnp.concatenate([egl_ref[c, ch * SUBLANES:(ch + 1) * SUBLANES, :]] * (DN_HEAD_DIM // SUBLANES), axis=0)
                state[c] = state[c] * decay_s + upd[:, i * DN_HEAD_DIM:(i + 1) * DN_HEAD_DIM]
        yield
    y_b = []
    for b in range(n_seq):
        heads = []
        for h in range(DN_HEADS):
            c = b * DN_HEADS + h
            s_ref[c] = state[c]
            o = (jnp.concatenate(o_inter[c], axis=0)
                 + jnp.dot(_block_diag(attn_ref[c], bd_mask),
                           jnp.concatenate(v_new[c], axis=0).astype(BF16), **ACC))
            heads.append(_rmsnorm(o, dng_ref[...])
                         * zg_ref[b * rows:(b + 1) * rows, h * DN_HEAD_DIM:(h + 1) * DN_HEAD_DIM])
        y_b.append(jnp.concatenate(heads, axis=1).astype(BF16))
        yield
    y = jnp.concatenate([ya_ref[...], jnp.concatenate(y_b, axis=0)], axis=1)
    for c0 in range(0, D_MODEL, PROJ_COLS):
        cols = slice(c0, c0 + PROJ_COLS)
        x = x_ref[:, :, cols].reshape(all_rows, PROJ_COLS)
        o_ref[:, :, cols] = (x + jnp.dot(y, wout_ref[:, cols], **ACC)).reshape(n_seq, rows, PROJ_COLS)
        yield


def _round_robin(*stages):
    stages = list(stages)
    while stages:
        for stage in list(stages):
            try:
                next(stage)
            except StopIteration:
                stages.remove(stage)


def _mixer_body(xa_ref, xb_ref, mg_ref, win_ref, wba_ref, lng_ref, lnb_ref, wcat_ref, bias_ref,
                cw_ref, alog_ref, dtb_ref, dng_ref, wout_ref, o_ref, s_ref, cin_ref, qkv_ref, *handoff):
    t = pl.program_id(0)
    n_tiles = pl.num_programs(0) - 1
    slot = lax.rem(t, 2)
    write_set = [ref.at[slot] for ref in handoff]
    read_set = [ref.at[1 - slot] for ref in handoff]
    stage_a = functools.partial(_mixer_stage_a, xa_ref, mg_ref, win_ref, wba_ref, lng_ref, lnb_ref, wcat_ref,
                                bias_ref, cw_ref, alog_ref, dtb_ref, cin_ref, qkv_ref)
    stage_b = functools.partial(_mixer_stage_b, xb_ref, dng_ref, wout_ref, s_ref, o_ref)

    @pl.when(t == 0)
    def _():
        s_ref[...] = jnp.zeros(s_ref.shape, F32)
        cin_ref[...] = jnp.zeros(cin_ref.shape, F32)
        _round_robin(stage_a(write_set))

    @pl.when(jnp.logical_and(t > 0, t < n_tiles))
    def _():
        _round_robin(stage_b(read_set), stage_a(write_set))

    @pl.when(t == n_tiles)
    def _():
        _round_robin(stage_b(read_set))


def _mixer_call(x, mix_norm, w_in, gm_ln_g, gm_ln_b, gm_w_s, gm_b_s, dn_conv_w, dn_a_log,
                dn_dt_bias, dn_norm, w_out):
    batch, seq, _ = x.shape
    n_tiles = seq // MIX_ROWS
    n_main = 2 * GM_WIDTH + 4 * DN_WIDTH
    w_all = w_in.astype(BF16)
    w_ba = jnp.pad(w_in[:, n_main:], ((0, 0), (0, LANES - 2 * DN_HEADS))).astype(BF16)
    bias = jnp.repeat(gm_b_s.T, GM_GROUP_DIM, axis=1)
    pad_a = lambda a: jnp.pad(a.reshape(1, DN_HEADS), ((0, 0), (DN_HEADS, LANES - 2 * DN_HEADS)))
    const2 = lambda t: (0, 0)
    tile_a = lambda t: (0, jnp.minimum(t, n_tiles - 1), 0)
    tile_b = lambda t: (0, jnp.maximum(t - 1, 0), 0)
    in_specs = [
        pl.BlockSpec((batch, MIX_ROWS, D_MODEL), tile_a),
        pl.BlockSpec((batch, MIX_ROWS, D_MODEL), tile_b),
        pl.BlockSpec((1, D_MODEL), const2),
        pl.BlockSpec(w_in.shape, const2),
        pl.BlockSpec((D_MODEL, LANES), const2),
        pl.BlockSpec((1, GM_WIDTH), const2),
        pl.BlockSpec((1, GM_WIDTH), const2),
        pl.BlockSpec((GM_GROUPS, GM_BLOCK, GM_BLOCK), lambda t: (0, 0, 0)),
        pl.BlockSpec((GM_BLOCK, GM_WIDTH), const2),
        pl.BlockSpec((DN_CONV, 3 * DN_WIDTH), const2),
        pl.BlockSpec((1, LANES), const2),
        pl.BlockSpec((1, LANES), const2),
        pl.BlockSpec((1, DN_HEAD_DIM), const2),
        pl.BlockSpec((D_MODEL, D_MODEL), const2),
    ]
    return pl.pallas_call(
        _mixer_body,
        grid=(n_tiles + 1,),
        in_specs=in_specs,
        out_specs=pl.BlockSpec((batch, MIX_ROWS, D_MODEL), tile_b),
        out_shape=jax.ShapeDtypeStruct((batch, seq, D_MODEL), F32),
        scratch_shapes=[
            pltpu.VMEM((batch * DN_HEADS, DN_HEAD_DIM, DN_HEAD_DIM), F32),
            pltpu.VMEM((batch, 3 * DN_HEADS, CONV_TAIL + MIX_ROWS, LANES), F32),
            pltpu.VMEM((batch, 3 * DN_HEADS, MIX_ROWS, LANES), F32),
        ] + _handoff_shapes(batch),
        compiler_params=pltpu.CompilerParams(
            dimension_semantics=("arbitrary",), vmem_limit_bytes=VMEM_LIMIT_BYTES),
        name="mixer",
    )(x, x, mix_norm.reshape(1, D_MODEL), w_all, w_ba, gm_ln_g.reshape(1, GM_WIDTH),
      gm_ln_b.reshape(1, GM_WIDTH), gm_w_s, bias, dn_conv_w, pad_a(dn_a_log), pad_a(dn_dt_bias),
      dn_norm.reshape(1, DN_HEAD_DIM), w_out.astype(BF16))


def kernel(x, ffn1_norm, ffn1_w_gate, ffn1_w_up, ffn1_w_down, mix_norm, w_in, gm_ln_g, gm_ln_b,
           gm_w_s, gm_b_s, dn_conv_w, dn_a_log, dn_dt_bias, dn_norm, w_out, ffn2_norm,
           ffn2_w_gate, ffn2_w_up, ffn2_w_down, final_norm):
    batch, seq, d = x.shape
    depth = ffn1_norm.shape[0]
    for l in range(depth):
        x = _ffn_stream_call(x.reshape(batch * seq, d), ffn1_norm[l], ffn1_w_gate[l], ffn1_w_up[l],
                      ffn1_w_down[l]).reshape(batch, seq, d)
        x = _mixer_call(x, mix_norm[l], w_in[l], gm_ln_g[l], gm_ln_b[l], gm_w_s[l], gm_b_s[l],
                        dn_conv_w[l], dn_a_log[l], dn_dt_bias[l], dn_norm[l], w_out[l])
        last = l == depth - 1
        x = _ffn_stream_call(x.reshape(batch * seq, d), ffn2_norm[l], ffn2_w_gate[l], ffn2_w_up[l],
                      ffn2_w_down[l], final_norm if last else None).reshape(batch, seq, d)
    return x
```

```python
import functools

import jax
import jax.numpy as jnp
from jax import lax
from jax.experimental import pallas as pl
from jax.experimental.pallas import tpu as pltpu

F32 = jnp.float32
BF16 = jnp.bfloat16
ACC = dict(preferred_element_type=jnp.float32)

D_MODEL = 1024
D_FF = 2816
CHUNK = 64
GM_GROUPS = 8
GM_GROUP_DIM = 64
GM_WIDTH = GM_GROUPS * GM_GROUP_DIM
GM_BLOCK = 128
DN_HEADS = 4
DN_HEAD_DIM = 128
DN_WIDTH = DN_HEADS * DN_HEAD_DIM
DN_CONV = 4
EPS = 1e-6

LANES = 128
SUBLANES = 8
VMEM_LIMIT_BYTES = 60000 * 1024

FFN_ROWS = 512
FFN_COLS = 256
MIX_ROWS = 256
MIX_CHUNKS = MIX_ROWS // CHUNK
CONV_TAIL = SUBLANES
PROJ_COLS = 256
CONV_BLOCK = 64

_NT = (((1,), (1,)), ((), ()))
_TN = (((0,), (0,)), ((), ()))


def _rmsnorm(x, g):
    return (x * lax.rsqrt(jnp.mean(x * x, axis=-1, keepdims=True) + EPS)) * g


def _softplus(x):
    return jnp.maximum(x, 0.0) + jnp.log1p(jnp.exp(-jnp.abs(x)))


def _ffn_body(x_ref, g_ref, wg_hbm, wu_hbm, wd_hbm, *rest, final_norm):
    if final_norm:
        fg_ref, o_ref, wg_ref, wu_ref, wd_ref, stage_gu, stage_d, sem = rest
    else:
        o_ref, wg_ref, wu_ref, wd_ref, stage_gu, stage_d, sem = rest
    n_chunks = D_FF // FFN_COLS

    def chunk_copies(j, slot):
        cols = pl.ds(j * FFN_COLS, FFN_COLS)
        return (pltpu.make_async_copy(wg_hbm.at[:, cols], stage_gu.at[slot, 0], sem.at[slot, 0]),
                pltpu.make_async_copy(wu_hbm.at[:, cols], stage_gu.at[slot, 1], sem.at[slot, 1]),
                pltpu.make_async_copy(wd_hbm.at[cols, :], stage_d.at[slot], sem.at[slot, 2]))

    def run(load_weights):
        x = x_ref[...]
        hb = _rmsnorm(x, g_ref[...]).astype(BF16)
        acc = jnp.zeros(x.shape, F32)
        if load_weights:
            for cp in chunk_copies(0, 0):
                cp.start()
        for j in range(n_chunks):
            cols = slice(j * FFN_COLS, (j + 1) * FFN_COLS)
            if load_weights:
                slot = j % 2
                if j + 1 < n_chunks:
                    for cp in chunk_copies(j + 1, 1 - slot):
                        cp.start()
                for cp in chunk_copies(j, slot):
                    cp.wait()
                wg_ref[:, cols] = stage_gu[slot, 0].astype(BF16)
                wu_ref[:, cols] = stage_gu[slot, 1].astype(BF16)
                wd_ref[cols, :] = stage_d[slot].astype(BF16)
            gate = jnp.dot(hb, wg_ref[:, cols], **ACC)
            up = jnp.dot(hb, wu_ref[:, cols], **ACC)
            act = (jax.nn.silu(gate) * up).astype(BF16)
            acc = acc + jnp.dot(act, wd_ref[cols, :], **ACC)
        y = x + 0.5 * acc
        if final_norm:
            y = _rmsnorm(y, fg_ref[...])
        o_ref[...] = y

    first = pl.program_id(0) == 0
    pl.when(first)(functools.partial(run, True))
    pl.when(jnp.logical_not(first))(functools.partial(run, False))


def _ffn_call(x2d, gain, w_gate, w_up, w_down, final_gain=None):
    rows = x2d.shape[0]
    final_norm = final_gain is not None
    const = lambda i: (0, 0)
    hbm = pl.BlockSpec(memory_space=pl.ANY)
    in_specs = [
        pl.BlockSpec((FFN_ROWS, D_MODEL), lambda i: (i, 0)),
        pl.BlockSpec((1, D_MODEL), const),
        hbm, hbm, hbm,
    ]
    args = [x2d, gain.reshape(1, D_MODEL), w_gate, w_up, w_down]
    if final_norm:
        in_specs.append(pl.BlockSpec((1, D_MODEL), const))
        args.append(final_gain.reshape(1, D_MODEL))
    return pl.pallas_call(
        functools.partial(_ffn_body, final_norm=final_norm),
        grid=(rows // FFN_ROWS,),
        in_specs=in_specs,
        out_specs=pl.BlockSpec((FFN_ROWS, D_MODEL), lambda i: (i, 0)),
        out_shape=jax.ShapeDtypeStruct((rows, D_MODEL), F32),
        scratch_shapes=[
            pltpu.VMEM((D_MODEL, D_FF), BF16),
            pltpu.VMEM((D_MODEL, D_FF), BF16),
            pltpu.VMEM((D_FF, D_MODEL), BF16),
            pltpu.VMEM((2, 2, D_MODEL, FFN_COLS), F32),
            pltpu.VMEM((2, FFN_COLS, D_MODEL), F32),
            pltpu.SemaphoreType.DMA((2, 3)),
        ],
        compiler_params=pltpu.CompilerParams(
            dimension_semantics=("arbitrary",), vmem_limit_bytes=VMEM_LIMIT_BYTES),
        name="ffn_final" if final_norm else "ffn",
    )(*args)


def _split_hi_lo(x):
    hi = x.astype(BF16)
    lo = (x - hi.astype(F32)).astype(BF16)
    return hi, lo


def _lane_bcast(x, col):
    return jnp.broadcast_to(x[:, col:col + 1], (x.shape[0], LANES))


def _col_to_wide(col, lane_lo):
    left = jnp.where(lane_lo, col[0:CHUNK], col[CHUNK:2 * CHUNK])
    right = jnp.where(lane_lo, col[2 * CHUNK:3 * CHUNK], col[3 * CHUNK:])
    return jnp.concatenate([left, right], axis=1)


def _block_diag(wide, bd_mask):
    return jnp.where(bd_mask, jnp.concatenate([wide.astype(F32)] * MIX_CHUNKS, axis=0), 0.0).astype(BF16)


def _handoff_shapes(n_seq):
    n_chain = n_seq * DN_HEADS
    return [
        pltpu.VMEM((2, n_seq * MIX_ROWS, GM_WIDTH), BF16),
        pltpu.VMEM((2, n_seq * MIX_ROWS, DN_WIDTH), F32),
        pltpu.VMEM((2, n_chain, CHUNK, MIX_ROWS), F32),
        pltpu.VMEM((2, n_chain, CHUNK, MIX_ROWS), BF16),
        pltpu.VMEM((2, n_chain, MIX_ROWS, 2 * DN_HEAD_DIM), BF16),
        pltpu.VMEM((2, n_chain, MIX_ROWS, DN_HEAD_DIM), BF16),
        pltpu.VMEM((2, n_chain, MIX_ROWS, DN_HEAD_DIM), BF16),
        pltpu.VMEM((2, n_chain, MIX_CHUNKS * SUBLANES, DN_HEAD_DIM), F32),
    ]


N_HANDOFF = 8


def _mixer_stage_a(x_ref, mg_ref, win_ref, wba_ref, lng_ref, lnb_ref, wcat_ref, bias_ref, cw_ref,
                   alog_ref, dtb_ref, cin_ref, qkv_ref, out):
    ya_ref, zg_ref, m_ref, attn_ref, rhs_ref, qd_ref, kd_ref, egl_ref = out
    n_seq = x_ref.shape[0]
    rows = MIX_ROWS
    all_rows = n_seq * rows
    n_blocks = all_rows // GM_BLOCK

    x = x_ref[...].reshape(all_rows, D_MODEL)
    hb = _rmsnorm(x, mg_ref[...]).astype(BF16)
    yield

    def project(lo, hi):
        parts = []
        for c0 in range(lo, hi, PROJ_COLS):
            parts.append(jnp.dot(hb, win_ref[:, c0:min(c0 + PROJ_COLS, hi)], **ACC))
            yield
        return jnp.concatenate(parts, axis=1)

    for b in range(n_seq):
        for s in range(3 * DN_WIDTH // LANES):
            cin_ref[b, s, 0:CONV_TAIL, :] = cin_ref[b, s, rows:rows + CONV_TAIL, :]
    for c0 in range(0, 3 * DN_WIDTH, PROJ_COLS):
        blk = jnp.dot(hb, win_ref[:, 2 * GM_WIDTH + c0:2 * GM_WIDTH + c0 + PROJ_COLS], **ACC)
        for b in range(n_seq):
            for j in range(PROJ_COLS // LANES):
                cin_ref[b, c0 // LANES + j, CONV_TAIL:, :] = blk[b * rows:(b + 1) * rows, j * LANES:(j + 1) * LANES]
        yield
    pba = jnp.dot(hb, wba_ref[...], **ACC)
    yield
    half = CONV_BLOCK // 2
    for b in range(n_seq):
        for s in range(3 * DN_WIDTH // LANES):
            src, dst = cin_ref.at[b, s], qkv_ref.at[b, s]
            w = cw_ref[:, s * LANES:(s + 1) * LANES]
            for r0 in range(0, rows, CONV_BLOCK):
                tap = lambda off: src[pl.ds(CONV_TAIL + r0 + off, half, stride=2), :]
                e0, e2, o1, om1, om3 = tap(0), tap(-2), tap(1), tap(-1), tap(-3)
                even = w[3:4] * e0 + w[2:3] * om1 + w[1:2] * e2 + w[0:1] * om3
                odd = w[3:4] * o1 + w[2:3] * e0 + w[1:2] * om1 + w[0:1] * e2
                dst[pl.ds(r0, half, stride=2), :] = jax.nn.silu(even)
                dst[pl.ds(r0 + 1, half, stride=2), :] = jax.nn.silu(odd)
    beta_all = jax.nn.sigmoid(pba)
    g_all = -jnp.exp(alog_ref[...]) * _softplus(pba + dtb_ref[...])

    v_a = jax.nn.gelu((yield from project(GM_WIDTH, 2 * GM_WIDTH)))
    zg_ref[...] = jax.nn.silu((yield from project(2 * GM_WIDTH + 3 * DN_WIDTH, 2 * GM_WIDTH + 4 * DN_WIDTH)))

    ri = lax.broadcasted_iota(jnp.int32, (rows, rows), 0)
    ci = lax.broadcasted_iota(jnp.int32, (rows, rows), 1)
    bd_mask = (ri // CHUNK) == (ci // CHUNK)
    tri_b = jnp.where(bd_mask & (ci <= ri), 1.0, 0.0).astype(BF16)
    blk_b = jnp.where(bd_mask, 1.0, 0.0).astype(BF16)
    g_hi, g_lo = _split_hi_lo(jnp.concatenate([g_all[b * rows:(b + 1) * rows] for b in range(n_seq)], axis=1))
    gc_all = jnp.dot(tri_b, g_hi, **ACC) + jnp.dot(tri_b, g_lo, **ACC)
    gl_all = jnp.dot(blk_b, g_hi, **ACC) + jnp.dot(blk_b, g_lo, **ACC)
    yield
    e_gc_all = jnp.exp(gc_all)
    e_glgc_all = jnp.exp(gl_all - gc_all)
    e_gl_all = jnp.exp(gl_all)
    gc_rows = [gc_all[:, b * LANES:(b + 1) * LANES].T for b in range(n_seq)]

    mu = jnp.mean(v_a, axis=-1, keepdims=True)
    vc = v_a - mu
    var = jnp.mean(vc * vc, axis=-1, keepdims=True)
    v_n = (vc * lax.rsqrt(var + EPS)) * lng_ref[...] + lnb_ref[...]

    wi = lax.broadcasted_iota(jnp.int32, (GM_BLOCK, 2 * GM_BLOCK), 0) // CHUNK
    wj = (lax.broadcasted_iota(jnp.int32, (GM_BLOCK, 2 * GM_BLOCK), 1) % GM_BLOCK) // CHUNK
    w_mask = wj <= wi
    lane_lo128 = lax.broadcasted_iota(jnp.int32, (GM_BLOCK, LANES), 1) < GM_GROUP_DIM
    mixed_cols = []
    for p in range(GM_GROUPS // 2):
        v_p = v_n[:, p * LANES:(p + 1) * LANES]
        rhs = []
        for blk_i in range(n_blocks):
            blk = v_p[blk_i * GM_BLOCK:(blk_i + 1) * GM_BLOCK]
            rhs.append(jnp.concatenate(
                [jnp.where(lane_lo128, blk, 0.0), jnp.where(lane_lo128, 0.0, blk)], axis=0))
        rhs = jnp.concatenate(rhs, axis=1).astype(BF16)
        w_pair = jnp.concatenate([wcat_ref[2 * p], wcat_ref[2 * p + 1]], axis=1)
        w_p = jnp.where(w_mask, w_pair, 0.0).astype(BF16)
        res = jnp.dot(w_p, rhs, **ACC)
        mixed_cols.append(jnp.concatenate(
            [res[:, i * LANES:(i + 1) * LANES] for i in range(n_blocks)], axis=0))
        if p % 2 == 1:
            yield
    bias = jnp.concatenate([bias_ref[...]] * n_blocks, axis=0)
    u_a = jax.nn.gelu((yield from project(0, GM_WIDTH)))
    ya_ref[...] = (u_a * (jnp.concatenate(mixed_cols, axis=1) + bias)).astype(BF16)

    wr = lax.broadcasted_iota(jnp.int32, (CHUNK, rows), 0)
    wc = lax.broadcasted_iota(jnp.int32, (CHUNK, rows), 1) % CHUNK
    tri_w = wc <= wr
    strict_w = wc < wr
    lane_lo = lax.broadcasted_iota(jnp.int32, (CHUNK, LANES), 1) < CHUNK
    scale = DN_HEAD_DIM ** -0.5

    zeros_k = jnp.zeros((2 * CHUNK, DN_HEAD_DIM), BF16)
    for b in range(n_seq):
        rs = slice(b * rows, (b + 1) * rows)
        for h0 in range(0, DN_HEADS, 2):
            q, k, k_beta = {}, {}, {}
            for h in (h0, h0 + 1):
                chain = b * DN_HEADS + h
                col = b * LANES + DN_HEADS + h
                q_h, k_h, v_h = qkv_ref[b, h], qkv_ref[b, DN_HEADS + h], qkv_ref[b, 2 * DN_HEADS + h]
                q[h] = q_h * lax.rsqrt(jnp.sum(q_h * q_h, axis=-1, keepdims=True) + EPS) * scale
                k[h] = k_h * lax.rsqrt(jnp.sum(k_h * k_h, axis=-1, keepdims=True) + EPS)
                beta = _lane_bcast(beta_all[rs], h)
                e_gc = _lane_bcast(e_gc_all, col)
                k_beta[h] = k[h] * beta
                rhs_ref[chain] = jnp.concatenate([v_h * beta, k_beta[h] * e_gc], axis=1).astype(BF16)
                qd_ref[chain] = (q[h] * e_gc).astype(BF16)
                kd_ref[chain] = (k[h] * _lane_bcast(e_glgc_all, col)).astype(BF16)
                for ch in range(MIX_CHUNKS):
                    egl_ref[chain, ch * SUBLANES:(ch + 1) * SUBLANES, :] = _lane_bcast(
                        e_gl_all[ch * CHUNK:ch * CHUNK + SUBLANES], col)
            kk_w = {h: [] for h in (h0, h0 + 1)}
            qk_w = {h: [] for h in (h0, h0 + 1)}
            for half_i in range(rows // LANES):
                hr = slice(half_i * LANES, (half_i + 1) * LANES)
                lhs = jnp.concatenate(
                    [jnp.concatenate([k_beta[h][hr], q[h][hr]], axis=0) for h in (h0, h0 + 1)], axis=1)
                kb0, kb1 = k[h0][hr].astype(BF16), k[h0 + 1][hr].astype(BF16)
                rhs_k = jnp.concatenate([jnp.concatenate([kb0, zeros_k], axis=1),
                                         jnp.concatenate([zeros_k, kb1], axis=1)], axis=0)
                out = lax.dot_general(lhs.astype(BF16), rhs_k, _NT, **ACC)
                for i, h in enumerate((h0, h0 + 1)):
                    kk = out[0:LANES, i * LANES:(i + 1) * LANES]
                    qk = out[LANES:, i * LANES:(i + 1) * LANES]
                    kk_w[h].append(jnp.where(lane_lo, kk[0:CHUNK], kk[CHUNK:]))
                    qk_w[h].append(jnp.where(lane_lo, qk[0:CHUNK], qk[CHUNK:]))
            for h in (h0, h0 + 1):
                chain = b * DN_HEADS + h
                col = b * LANES + DN_HEADS + h
                dlog = (_col_to_wide(_lane_bcast(gc_all, col), lane_lo)
                        - jnp.broadcast_to(gc_rows[b][DN_HEADS + h:DN_HEADS + h + 1, :], (CHUNK, rows)))
                decay = jnp.exp(jnp.where(tri_w, dlog, -jnp.inf))
                m_ref[chain] = -jnp.where(strict_w, jnp.concatenate(kk_w[h], axis=1) * decay, 0.0)
                attn_ref[chain] = (jnp.concatenate(qk_w[h], axis=1) * decay).astype(BF16)
            yield


def _mixer_stage_b(x_ref, dng_ref, wout_ref, s_ref, o_ref, inp):
    ya_ref, zg_ref, m_ref, attn_ref, rhs_ref, qd_ref, kd_ref, egl_ref = inp
    n_seq = x_ref.shape[0]
    rows = MIX_ROWS
    all_rows = n_seq * rows
    chains = range(n_seq * DN_HEADS)

    ri = lax.broadcasted_iota(jnp.int32, (rows, rows), 0)
    ci = lax.broadcasted_iota(jnp.int32, (rows, rows), 1)
    bd_mask = (ri // CHUNK) == (ci // CHUNK)
    wr = lax.broadcasted_iota(jnp.int32, (CHUNK, rows), 0)
    wc = lax.broadcasted_iota(jnp.int32, (CHUNK, rows), 1) % CHUNK
    eye_w = jnp.where(wc == wr, 1.0, 0.0)

    m_w = {c: m_ref[c] for c in chains}
    inv_w = {c: eye_w + m_w[c] for c in chains}
    m_bd = {c: _block_diag(m_w[c], bd_mask) for c in chains}
    for _ in range(5):
        for c in chains:
            m_w[c] = jnp.dot(m_w[c].astype(BF16), m_bd[c], **ACC)
        yield
        for c in chains:
            m_bd[c] = _block_diag(m_w[c], bd_mask)
        for c in chains:
            inv_w[c] = inv_w[c] + jnp.dot(inv_w[c].astype(BF16), m_bd[c], **ACC)
        yield
    uw = {c: jnp.dot(_block_diag(inv_w[c], bd_mask), rhs_ref[c], **ACC) for c in chains}
    yield

    state = {c: s_ref[c] for c in chains}
    v_new = {c: [] for c in chains}
    o_inter = {c: [] for c in chains}
    pairs = [(c, c + 1) for c in range(0, n_seq * DN_HEADS, 2)]
    zeros_s = jnp.zeros((DN_HEAD_DIM, DN_HEAD_DIM), BF16)
    zeros_v = jnp.zeros((CHUNK, DN_HEAD_DIM), BF16)

    def block_diag2(a, b, zeros):
        return jnp.concatenate([jnp.concatenate([a, zeros], axis=1), jnp.concatenate([zeros, b], axis=1)], axis=0)

    for ch in range(MIX_CHUNKS):
        cs = slice(ch * CHUNK, (ch + 1) * CHUNK)
        r = {}
        for pair in pairs:
            wq = jnp.concatenate(
                [jnp.concatenate([uw[c][cs, DN_HEAD_DIM:].astype(BF16), qd_ref[c, cs, :]], axis=0) for c in pair],
                axis=1)
            s_bd = block_diag2(state[pair[0]].astype(BF16), state[pair[1]].astype(BF16), zeros_s)
            r[pair] = jnp.dot(wq, s_bd, **ACC)
        yield
        for pair in pairs:
            vn = {}
            for i, c in enumerate(pair):
                r_c = r[pair][:, i * DN_HEAD_DIM:(i + 1) * DN_HEAD_DIM]
                vn[c] = uw[c][cs, :DN_HEAD_DIM] - r_c[:CHUNK]
                v_new[c].append(vn[c])
                o_inter[c].append(r_c[CHUNK:])
            kd = jnp.concatenate([kd_ref[c, cs, :] for c in pair], axis=0)
            vn_bd = block_diag2(vn[pair[0]].astype(BF16), vn[pair[1]].astype(BF16), zeros_v)
            upd = lax.dot_general(kd, vn_bd, _TN, **ACC)
            for i, c in enumerate(pair):
                decay_s = jnp.concatenate([egl_ref[c, ch * SUBLANES:(ch + 1) * SUBLANES, :]] * (DN_HEAD_DIM // SUBLANES), axis=0)
                state[c] = state[c] * decay_s + upd[:, i * DN_HEAD_DIM:(i + 1) * DN_HEAD_DIM]
        yield
    y_b = []
    for b in range(n_seq):
        heads = []
        for h in range(DN_HEADS):
            c = b * DN_HEADS + h
            s_ref[c] = state[c]
            o = (jnp.concatenate(o_inter[c], axis=0)
                 + jnp.dot(_block_diag(attn_ref[c], bd_mask),
                           jnp.concatenate(v_new[c], axis=0).astype(BF16), **ACC))
            heads.append(_rmsnorm(o, dng_ref[...])
                         * zg_ref[b * rows:(b + 1) * rows, h * DN_HEAD_DIM:(h + 1) * DN_HEAD_DIM])
        y_b.append(jnp.concatenate(heads, axis=1).astype(BF16))
        yield
    y = jnp.concatenate([ya_ref[...], jnp.concatenate(y_b, axis=0)], axis=1)
    for c0 in range(0, D_MODEL, PROJ_COLS):
        cols = slice(c0, c0 + PROJ_COLS)
        x = x_ref[:, :, cols].reshape(all_rows, PROJ_COLS)
        o_ref[:, :, cols] = (x + jnp.dot(y, wout_ref[:, cols], **ACC)).reshape(n_seq, rows, PROJ_COLS)
        yield


def _round_robin(*stages):
    stages = list(stages)
    while stages:
        for stage in list(stages):
            try:
                next(stage)
            except StopIteration:
                stages.remove(stage)


def _mixer_body(xa_ref, xb_ref, mg_ref, win_ref, wba_ref, lng_ref, lnb_ref, wcat_ref, bias_ref,
                cw_ref, alog_ref, dtb_ref, dng_ref, wout_ref, o_ref, s_ref, cin_ref, qkv_ref, *handoff):
    t = pl.program_id(0)
    n_tiles = pl.num_programs(0) - 1
    slot = lax.rem(t, 2)
    write_set = [ref.at[slot] for ref in handoff]
    read_set = [ref.at[1 - slot] for ref in handoff]
    stage_a = functools.partial(_mixer_stage_a, xa_ref, mg_ref, win_ref, wba_ref, lng_ref, lnb_ref, wcat_ref,
                                bias_ref, cw_ref, alog_ref, dtb_ref, cin_ref, qkv_ref)
    stage_b = functools.partial(_mixer_stage_b, xb_ref, dng_ref, wout_ref, s_ref, o_ref)

    @pl.when(t == 0)
    def _():
        s_ref[...] = jnp.zeros(s_ref.shape, F32)
        cin_ref[...] = jnp.zeros(cin_ref.shape, F32)
        _round_robin(stage_a(write_set))

    @pl.when(jnp.logical_and(t > 0, t < n_tiles))
    def _():
        _round_robin(stage_b(read_set), stage_a(write_set))

    @pl.when(t == n_tiles)
    def _():
        _round_robin(stage_b(read_set))


def _mixer_call(x, mix_norm, w_in, gm_ln_g, gm_ln_b, gm_w_s, gm_b_s, dn_conv_w, dn_a_log,
                dn_dt_bias, dn_norm, w_out):
    batch, seq, _ = x.shape
    n_tiles = seq // MIX_ROWS
    n_main = 2 * GM_WIDTH + 4 * DN_WIDTH
    w_all = w_in.astype(BF16)
    w_ba = jnp.pad(w_in[:, n_main:], ((0, 0), (0, LANES - 2 * DN_HEADS))).astype(BF16)
    bias = jnp.repeat(gm_b_s.T, GM_GROUP_DIM, axis=1)
    pad_a = lambda a: jnp.pad(a.reshape(1, DN_HEADS), ((0, 0), (DN_HEADS, LANES - 2 * DN_HEADS)))
    const2 = lambda t: (0, 0)
    tile_a = lambda t: (0, jnp.minimum(t, n_tiles - 1), 0)
    tile_b = lambda t: (0, jnp.maximum(t - 1, 0), 0)
    in_specs = [
        pl.BlockSpec((batch, MIX_ROWS, D_MODEL), tile_a),
        pl.BlockSpec((batch, MIX_ROWS, D_MODEL), tile_b),
        pl.BlockSpec((1, D_MODEL), const2),
        pl.BlockSpec(w_in.shape, const2, pipeline_mode=pl.Buffered(1)),
        pl.BlockSpec((D_MODEL, LANES), const2),
        pl.BlockSpec((1, GM_WIDTH), const2),
        pl.BlockSpec((1, GM_WIDTH), const2),
        pl.BlockSpec((GM_GROUPS, GM_BLOCK, GM_BLOCK), lambda t: (0, 0, 0)),
        pl.BlockSpec((GM_BLOCK, GM_WIDTH), const2),
        pl.BlockSpec((DN_CONV, 3 * DN_WIDTH), const2),
        pl.BlockSpec((1, LANES), const2),
        pl.BlockSpec((1, LANES), const2),
        pl.BlockSpec((1, DN_HEAD_DIM), const2),
        pl.BlockSpec((D_MODEL, D_MODEL), const2, pipeline_mode=pl.Buffered(1)),
    ]
    return pl.pallas_call(
        _mixer_body,
        grid=(n_tiles + 1,),
        in_specs=in_specs,
        out_specs=pl.BlockSpec((batch, MIX_ROWS, D_MODEL), tile_b),
        out_shape=jax.ShapeDtypeStruct((batch, seq, D_MODEL), F32),
        scratch_shapes=[
            pltpu.VMEM((batch * DN_HEADS, DN_HEAD_DIM, DN_HEAD_DIM), F32),
            pltpu.VMEM((batch, 3 * DN_HEADS, CONV_TAIL + MIX_ROWS, LANES), F32),
            pltpu.VMEM((batch, 3 * DN_HEADS, MIX_ROWS, LANES), F32),
        ] + _handoff_shapes(batch),
        compiler_params=pltpu.CompilerParams(
            dimension_semantics=("arbitrary",), vmem_limit_bytes=VMEM_LIMIT_BYTES),
        name="mixer",
    )(x, x, mix_norm.reshape(1, D_MODEL), w_all, w_ba, gm_ln_g.reshape(1, GM_WIDTH),
      gm_ln_b.reshape(1, GM_WIDTH), gm_w_s, bias, dn_conv_w, pad_a(dn_a_log), pad_a(dn_dt_bias),
      dn_norm.reshape(1, DN_HEAD_DIM), w_out.astype(BF16))


def kernel(x, ffn1_norm, ffn1_w_gate, ffn1_w_up, ffn1_w_down, mix_norm, w_in, gm_ln_g, gm_ln_b,
           gm_w_s, gm_b_s, dn_conv_w, dn_a_log, dn_dt_bias, dn_norm, w_out, ffn2_norm,
           ffn2_w_gate, ffn2_w_up, ffn2_w_down, final_norm):
    batch, seq, d = x.shape
    depth = ffn1_norm.shape[0]
    for l in range(depth):
        x = _ffn_call(x.reshape(batch * seq, d), ffn1_norm[l], ffn1_w_gate[l], ffn1_w_up[l],
                      ffn1_w_down[l]).reshape(batch, seq, d)
        x = _mixer_call(x, mix_norm[l], w_in[l], gm_ln_g[l], gm_ln_b[l], gm_w_s[l], gm_b_s[l],
                        dn_conv_w[l], dn_a_log[l], dn_dt_bias[l], dn_norm[l], w_out[l])
        last = l == depth - 1
        x = _ffn_call(x.reshape(batch * seq, d), ffn2_norm[l], ffn2_w_gate[l], ffn2_w_up[l],
                      ffn2_w_down[l], final_norm if last else None).reshape(batch, seq, d)
    return x
```
